```python
import jax, jax.numpy as jnp
from jax import lax
import numpy as np

D_MODEL = 1024
BATCH = 8
SEQ = 8192
DEPTH = 1

CONF_WIDTH = 1024
CONF_KERNEL = 31
GDN_HEADS = 8
GDN_HEAD_K = 128
GDN_HEAD_V = 128
GDN_KEY_WIDTH = GDN_HEADS * GDN_HEAD_K
GDN_VAL_WIDTH = GDN_HEADS * GDN_HEAD_V
GDN_CONV = 4
GDN_CHUNK = 64
LN_EPS = 1e-5
RMS_EPS = 1e-6
L2_EPS = 1e-6
DN_ALPHA = (2 * DEPTH) ** 0.25
DN_BETA = (8 * DEPTH) ** -0.25

IN_SPLITS = (CONF_WIDTH, CONF_WIDTH, CONF_WIDTH,
             GDN_KEY_WIDTH, GDN_KEY_WIDTH, GDN_VAL_WIDTH, GDN_VAL_WIDTH,
             GDN_HEADS, GDN_HEADS,
             D_MODEL, D_MODEL)
IN_WIDTH = sum(IN_SPLITS)

kernel_name = "hybrid_conformer_gdn_deepnorm"


def _split_cols(t, widths):
    out, start = [], 0
    for w in widths:
        out.append(t[..., start:start + w])
        start += w
    return out


def _layernorm(x, g, b):
    xf = x.astype(jnp.float32)
    mu = xf.mean(-1, keepdims=True)
    var = jnp.square(xf - mu).mean(-1, keepdims=True)
    y = (xf - mu) * lax.rsqrt(var + LN_EPS) * g.astype(jnp.float32) + b.astype(jnp.float32)
    return y.astype(x.dtype)


def _causal_depthwise_conv(x, w, b=None):
    K, C = w.shape
    y = lax.conv_general_dilated(
        x, w[:, None, :].astype(x.dtype), window_strides=(1,), padding=((K - 1, 0),),
        dimension_numbers=('NWC', 'WIO', 'NWC'), feature_group_count=C)
    if b is not None:
        y = y + b.astype(x.dtype)
    return y


def _chunk_gated_delta_rule(q, k, v, g, beta):
    B_, S, H, Dk = q.shape
    Dv = v.shape[-1]
    C = GDN_CHUNK
    N = S // C

    def chunk4(t):
        return jnp.moveaxis(t.reshape(B_, N, C, H, t.shape[-1]), 3, 2)

    def chunk3(t):
        return jnp.moveaxis(t.reshape(B_, N, C, H), 3, 2)

    q, k, v = chunk4(q), chunk4(k), chunk4(v)
    g, beta = chunk3(g), chunk3(beta)
    gc = jnp.cumsum(g, axis=-1)

    idx = jnp.arange(C)
    causal = idx[:, None] >= idx[None, :]
    strict = idx[:, None] > idx[None, :]
    decay = jnp.exp(jnp.where(causal, gc[..., :, None] - gc[..., None, :], -jnp.inf))

    kb = k * beta[..., None]
    L = jnp.where(strict, jnp.einsum('bnhik,bnhjk->bnhij', kb, k) * decay, 0.0)
    eye = jnp.eye(C, dtype=q.dtype)
    rhs = jnp.concatenate([v * beta[..., None], kb * jnp.exp(gc)[..., None]], axis=-1)
    sol = lax.linalg.triangular_solve(eye + L, rhs, left_side=True, lower=True, unit_diagonal=True)
    u, w = sol[..., :Dv], sol[..., Dv:]

    intra = jnp.where(causal, jnp.einsum('bnhik,bnhjk->bnhij', q, k) * decay, 0.0)
    q_dec = q * jnp.exp(gc)[..., None]
    g_last = gc[..., -1]
    k_dec = k * jnp.exp(g_last[..., None] - gc)[..., None]
    chunk_decay = jnp.exp(g_last)

    def step(state, inp):
        q_i, k_i, u_i, w_i, a_i, d_i = inp
        v_new = u_i - jnp.einsum('bhck,bhkv->bhcv', w_i, state)
        o = jnp.einsum('bhck,bhkv->bhcv', q_i, state) + jnp.einsum('bhij,bhjv->bhiv', a_i, v_new)
        state = state * d_i[..., None, None] + jnp.einsum('bhck,bhcv->bhkv', k_i, v_new)
        return state, o

    xs = tuple(jnp.moveaxis(t, 1, 0) for t in (q_dec, k_dec, u, w, intra, chunk_decay))
    state0 = jnp.zeros((B_, H, Dk, Dv), q.dtype)
    _, o = lax.scan(step, state0, xs)
    o = jnp.moveaxis(o, 0, 1)
    return jnp.moveaxis(o, 2, 3).reshape(B_, S, H, Dv)


def _fwd_setup_inputs(seed: int = 0) -> dict:
    key = jax.random.key(seed)
    ks = jax.random.split(key, 16)
    f32 = jnp.float32
    x = jax.random.normal(ks[0], (BATCH, SEQ, D_MODEL), f32)
    col_scale = jnp.concatenate([
        jnp.full((w,), DN_BETA if i in (0, 5) else 1.0, f32) for i, w in enumerate(IN_SPLITS)])
    w_in = jax.random.normal(ks[1], (D_MODEL, IN_WIDTH), f32) * D_MODEL ** -0.5 * col_scale
    conf_dw_w = jax.random.normal(ks[2], (CONF_KERNEL, CONF_WIDTH), f32) * CONF_KERNEL ** -0.5
    conf_dw_b = 0.01 * jax.random.normal(ks[3], (CONF_WIDTH,), f32)
    conf_ln_g = 1.0 + 0.02 * jax.random.normal(ks[4], (CONF_WIDTH,), f32)
    conf_ln_b = 0.02 * jax.random.normal(ks[5], (CONF_WIDTH,), f32)
    conf_w_out = jax.random.normal(ks[6], (CONF_WIDTH, D_MODEL), f32) * CONF_WIDTH ** -0.5 * DN_BETA
    gdn_conv_w = jax.random.normal(ks[7], (GDN_CONV, 2 * GDN_KEY_WIDTH + GDN_VAL_WIDTH), f32) * GDN_CONV ** -0.5
    gdn_A_log = jnp.log(jax.random.uniform(ks[8], (GDN_HEADS,), f32, 1.0, 16.0))
    dt = jnp.exp(jax.random.uniform(ks[9], (GDN_HEADS,), f32, np.log(1e-3), np.log(1e-1)))
    gdn_dt_bias = dt + jnp.log(-jnp.expm1(-dt))
    gdn_norm_g = 1.0 + 0.02 * jax.random.normal(ks[10], (GDN_HEAD_V,), f32)
    gdn_w_out = jax.random.normal(ks[11], (GDN_VAL_WIDTH, D_MODEL), f32) * GDN_VAL_WIDTH ** -0.5 * DN_BETA
    w_o = jax.random.normal(ks[12], (D_MODEL, D_MODEL), f32) * D_MODEL ** -0.5 * DN_BETA
    post_ln_g = 1.0 + 0.02 * jax.random.normal(ks[13], (D_MODEL,), f32)
    post_ln_b = 0.02 * jax.random.normal(ks[14], (D_MODEL,), f32)
    return {"x": x, "w_in": w_in, "conf_dw_w": conf_dw_w, "conf_dw_b": conf_dw_b,
            "conf_ln_g": conf_ln_g, "conf_ln_b": conf_ln_b, "conf_w_out": conf_w_out,
            "gdn_conv_w": gdn_conv_w, "gdn_A_log": gdn_A_log, "gdn_dt_bias": gdn_dt_bias,
            "gdn_norm_g": gdn_norm_g, "gdn_w_out": gdn_w_out, "w_o": w_o,
            "post_ln_g": post_ln_g, "post_ln_b": post_ln_b}


def _fwd_reference(x, w_in, conf_dw_w, conf_dw_b, conf_ln_g, conf_ln_b, conf_w_out,
              gdn_conv_w, gdn_A_log, gdn_dt_bias, gdn_norm_g, gdn_w_out, w_o,
              post_ln_g, post_ln_b):
    f32 = jnp.float32
    for _ in range(DEPTH):
        B_, S, _ = x.shape
        proj = jnp.einsum('bsd,de->bse', x, w_in)
        (c_val, c_glu, c_z, q, k, v, g_z, b_logit, a_logit,
         gate_c, gate_g) = _split_cols(proj, IN_SPLITS)

        a = c_val * jax.nn.sigmoid(c_glu)
        a = _causal_depthwise_conv(a, conf_dw_w, conf_dw_b)
        a = jax.nn.silu(_layernorm(a, conf_ln_g, conf_ln_b))
        y_conf = jnp.einsum('bsc,cd->bsd', a * jax.nn.silu(c_z), conf_w_out)

        qkv = jax.nn.silu(_causal_depthwise_conv(jnp.concatenate([q, k, v], axis=-1), gdn_conv_w))
        q, k, v = _split_cols(qkv, (GDN_KEY_WIDTH, GDN_KEY_WIDTH, GDN_VAL_WIDTH))
        q = q.reshape(B_, S, GDN_HEADS, GDN_HEAD_K).astype(f32)
        k = k.reshape(B_, S, GDN_HEADS, GDN_HEAD_K).astype(f32)
        v = v.reshape(B_, S, GDN_HEADS, GDN_HEAD_V).astype(f32)
        q = q * lax.rsqrt(jnp.sum(q * q, -1, keepdims=True) + L2_EPS) * (GDN_HEAD_K ** -0.5)
        k = k * lax.rsqrt(jnp.sum(k * k, -1, keepdims=True) + L2_EPS)
        beta = jax.nn.sigmoid(b_logit.astype(f32))
        g = -jnp.exp(gdn_A_log.astype(f32)) * jax.nn.softplus(a_logit.astype(f32) + gdn_dt_bias.astype(f32))
        o = _chunk_gated_delta_rule(q, k, v, g, beta)
        o = o * lax.rsqrt(jnp.mean(o * o, -1, keepdims=True) + RMS_EPS) * gdn_norm_g.astype(f32)
        o = o.reshape(B_, S, GDN_VAL_WIDTH).astype(x.dtype) * jax.nn.silu(g_z)
        y_gdn = jnp.einsum('bsc,cd->bsd', o, gdn_w_out)

        h = jax.nn.sigmoid(gate_c) * y_conf + jax.nn.sigmoid(gate_g) * y_gdn
        sub = jnp.einsum('bsd,de->bse', h, w_o)
        x = _layernorm(DN_ALPHA * x + sub, post_ln_g, post_ln_b)
    return x


import jax as _jax
import jax.numpy as _jnp

TWIN_FORMAT = 'train_step'
FWD_PARAMS = ['x', 'w_in', 'conf_dw_w', 'conf_dw_b', 'conf_ln_g', 'conf_ln_b', 'conf_w_out', 'gdn_conv_w', 'gdn_A_log', 'gdn_dt_bias', 'gdn_norm_g', 'gdn_w_out', 'w_o', 'post_ln_g', 'post_ln_b']
TWIN_WEIGHTS = ['w_in', 'conf_dw_w', 'conf_dw_b', 'conf_ln_g', 'conf_ln_b', 'conf_w_out', 'gdn_conv_w', 'gdn_A_log', 'gdn_dt_bias', 'gdn_norm_g', 'gdn_w_out', 'w_o', 'post_ln_g', 'post_ln_b']
TWIN_DIFF_INPUT = 'x'
TWIN_INPUTS = ['x', 'w_in', 'conf_dw_w', 'conf_dw_b', 'conf_ln_g', 'conf_ln_b', 'conf_w_out', 'gdn_conv_w', 'gdn_A_log', 'gdn_dt_bias', 'gdn_norm_g', 'gdn_w_out', 'w_o', 'post_ln_g', 'post_ln_b', 'loss_target', 'm_w_in', 'm_conf_dw_w', 'm_conf_dw_b', 'm_conf_ln_g', 'm_conf_ln_b', 'm_conf_w_out', 'm_gdn_conv_w', 'm_gdn_A_log', 'm_gdn_dt_bias', 'm_gdn_norm_g', 'm_gdn_w_out', 'm_w_o', 'm_post_ln_g', 'm_post_ln_b', 'v_w_in', 'v_conf_dw_w', 'v_conf_dw_b', 'v_conf_ln_g', 'v_conf_ln_b', 'v_conf_w_out', 'v_gdn_conv_w', 'v_gdn_A_log', 'v_gdn_dt_bias', 'v_gdn_norm_g', 'v_gdn_w_out', 'v_w_o', 'v_post_ln_g', 'v_post_ln_b']
TWIN_OUTPUTS = ['loss', 'grad_x', 'grad_w_in', 'grad_conf_dw_w', 'grad_conf_dw_b', 'grad_conf_ln_g', 'grad_conf_ln_b', 'grad_conf_w_out', 'grad_gdn_conv_w', 'grad_gdn_A_log', 'grad_gdn_dt_bias', 'grad_gdn_norm_g', 'grad_gdn_w_out', 'grad_w_o', 'grad_post_ln_g', 'grad_post_ln_b', 'delta_w_in', 'delta_conf_dw_w', 'delta_conf_dw_b', 'delta_conf_ln_g', 'delta_conf_ln_b', 'delta_conf_w_out', 'delta_gdn_conv_w', 'delta_gdn_A_log', 'delta_gdn_dt_bias', 'delta_gdn_norm_g', 'delta_gdn_w_out', 'delta_w_o', 'delta_post_ln_g', 'delta_post_ln_b', 'new_m_w_in', 'new_m_conf_dw_w', 'new_m_conf_dw_b', 'new_m_conf_ln_g', 'new_m_conf_ln_b', 'new_m_conf_w_out', 'new_m_gdn_conv_w', 'new_m_gdn_A_log', 'new_m_gdn_dt_bias', 'new_m_gdn_norm_g', 'new_m_gdn_w_out', 'new_m_w_o', 'new_m_post_ln_g', 'new_m_post_ln_b', 'new_v_w_in', 'new_v_conf_dw_w', 'new_v_conf_dw_b', 'new_v_conf_ln_g', 'new_v_conf_ln_b', 'new_v_conf_w_out', 'new_v_gdn_conv_w', 'new_v_gdn_A_log', 'new_v_gdn_dt_bias', 'new_v_gdn_norm_g', 'new_v_gdn_w_out', 'new_v_w_o', 'new_v_post_ln_g', 'new_v_post_ln_b']
TWIN_LEAF_KINDS = {'loss': 'loss', 'grad_x': 'grad_x', 'grad_w_in': 'grad_w', 'grad_conf_dw_w': 'grad_w', 'grad_conf_dw_b': 'grad_w', 'grad_conf_ln_g': 'grad_w', 'grad_conf_ln_b': 'grad_w', 'grad_conf_w_out': 'grad_w', 'grad_gdn_conv_w': 'grad_w', 'grad_gdn_A_log': 'grad_w', 'grad_gdn_dt_bias': 'grad_w', 'grad_gdn_norm_g': 'grad_w', 'grad_gdn_w_out': 'grad_w', 'grad_w_o': 'grad_w', 'grad_post_ln_g': 'grad_w', 'grad_post_ln_b': 'grad_w', 'delta_w_in': 'delta_w', 'delta_conf_dw_w': 'delta_w', 'delta_conf_dw_b': 'delta_w', 'delta_conf_ln_g': 'delta_w', 'delta_conf_ln_b': 'delta_w', 'delta_conf_w_out': 'delta_w', 'delta_gdn_conv_w': 'delta_w', 'delta_gdn_A_log': 'delta_w', 'delta_gdn_dt_bias': 'delta_w', 'delta_gdn_norm_g': 'delta_w', 'delta_gdn_w_out': 'delta_w', 'delta_w_o': 'delta_w', 'delta_post_ln_g': 'delta_w', 'delta_post_ln_b': 'delta_w', 'new_m_w_in': 'new_m', 'new_m_conf_dw_w': 'new_m', 'new_m_conf_dw_b': 'new_m', 'new_m_conf_ln_g': 'new_m', 'new_m_conf_ln_b': 'new_m', 'new_m_conf_w_out': 'new_m', 'new_m_gdn_conv_w': 'new_m', 'new_m_gdn_A_log': 'new_m', 'new_m_gdn_dt_bias': 'new_m', 'new_m_gdn_norm_g': 'new_m', 'new_m_gdn_w_out': 'new_m', 'new_m_w_o': 'new_m', 'new_m_post_ln_g': 'new_m', 'new_m_post_ln_b': 'new_m', 'new_v_w_in': 'new_v', 'new_v_conf_dw_w': 'new_v', 'new_v_conf_dw_b': 'new_v', 'new_v_conf_ln_g': 'new_v', 'new_v_conf_ln_b': 'new_v', 'new_v_conf_w_out': 'new_v', 'new_v_gdn_conv_w': 'new_v', 'new_v_gdn_A_log': 'new_v', 'new_v_gdn_dt_bias': 'new_v', 'new_v_gdn_norm_g': 'new_v', 'new_v_gdn_w_out': 'new_v', 'new_v_w_o': 'new_v', 'new_v_post_ln_g': 'new_v', 'new_v_post_ln_b': 'new_v'}


def _forward(args):
    return _fwd_reference(*[args[k] for k in FWD_PARAMS])


def _output_shape():
    def fwd():
        inp = _fwd_setup_inputs(0)
        return _fwd_reference(*[inp[k] for k in FWD_PARAMS])
    out = _jax.eval_shape(fwd)
    return out.shape, out.dtype

N_MICROBATCH = 1
ADAM_LR = 0.001
ADAM_B1 = 0.9
ADAM_B2 = 0.999
ADAM_EPS = 1e-08
ADAM_WD = 0.01
ADAM_STEP = 10
PER_EXAMPLE_BATCH_AXIS = {'x': 0, 'loss_target': 0}
SHARED_INPUTS = []
_WEIGHT_DTYPES = {'w_in': _jnp.float32, 'conf_dw_w': _jnp.float32, 'conf_dw_b': _jnp.float32, 'conf_ln_g': _jnp.float32, 'conf_ln_b': _jnp.float32, 'conf_w_out': _jnp.float32, 'gdn_conv_w': _jnp.float32, 'gdn_A_log': _jnp.float32, 'gdn_dt_bias': _jnp.float32, 'gdn_norm_g': _jnp.float32, 'gdn_w_out': _jnp.float32, 'w_o': _jnp.float32, 'post_ln_g': _jnp.float32, 'post_ln_b': _jnp.float32}
MOMENT_SCALE = {'w_in': 1.944708e-02, 'conf_dw_w': 1.489048e-02, 'conf_dw_b': 5.433113e-02, 'conf_ln_g': 1.799912e-02, 'conf_ln_b': 1.787853e-02, 'conf_w_out': 2.493491e-02, 'gdn_conv_w': 1.932860e-02, 'gdn_A_log': 1.029848e-01, 'gdn_dt_bias': 9.834776e-02, 'gdn_norm_g': 6.294637e-02, 'gdn_w_out': 4.141594e-02, 'w_o': 4.895071e-02, 'post_ln_g': 6.399247e+01, 'post_ln_b': 1.228537e+00}


def _to_microbatches(a, axis):
    t = _jnp.moveaxis(a, axis, 0)
    t = t.reshape((N_MICROBATCH, t.shape[0] // N_MICROBATCH) + t.shape[1:])
    return _jnp.moveaxis(t, 1, axis + 1)


def setup_inputs(seed: int = 0) -> dict:
    inp = _fwd_setup_inputs(seed)
    key = _jax.random.fold_in(_jax.random.key(seed), 7919)
    shape, _ = _output_shape()
    out = dict(inp)
    out["loss_target"] = _jax.random.normal(_jax.random.fold_in(key, 0), shape, _jnp.float32)
    for i, name in enumerate(TWIN_WEIGHTS):
        w = inp[name].astype(_jnp.float32)
        if MOMENT_SCALE is None:
            s = _jnp.sqrt(_jnp.mean(_jnp.square(w)) + 1e-30)
        else:
            s = MOMENT_SCALE[name]
        km, kv = _jax.random.split(_jax.random.fold_in(key, i + 1))
        out[name] = w
        out["m_" + name] = s * _jax.random.normal(km, w.shape, _jnp.float32)
        out["v_" + name] = (s * s) * _jax.random.uniform(kv, w.shape, _jnp.float32, 0.5, 1.5)
    if N_MICROBATCH > 1:
        for name, axis in PER_EXAMPLE_BATCH_AXIS.items():
            out[name] = _to_microbatches(out[name], axis)
    return {'x': out['x'], 'w_in': out['w_in'], 'conf_dw_w': out['conf_dw_w'], 'conf_dw_b': out['conf_dw_b'], 'conf_ln_g': out['conf_ln_g'], 'conf_ln_b': out['conf_ln_b'], 'conf_w_out': out['conf_w_out'], 'gdn_conv_w': out['gdn_conv_w'], 'gdn_A_log': out['gdn_A_log'], 'gdn_dt_bias': out['gdn_dt_bias'], 'gdn_norm_g': out['gdn_norm_g'], 'gdn_w_out': out['gdn_w_out'], 'w_o': out['w_o'], 'post_ln_g': out['post_ln_g'], 'post_ln_b': out['post_ln_b'], 'loss_target': out['loss_target'], 'm_w_in': out['m_w_in'], 'm_conf_dw_w': out['m_conf_dw_w'], 'm_conf_dw_b': out['m_conf_dw_b'], 'm_conf_ln_g': out['m_conf_ln_g'], 'm_conf_ln_b': out['m_conf_ln_b'], 'm_conf_w_out': out['m_conf_w_out'], 'm_gdn_conv_w': out['m_gdn_conv_w'], 'm_gdn_A_log': out['m_gdn_A_log'], 'm_gdn_dt_bias': out['m_gdn_dt_bias'], 'm_gdn_norm_g': out['m_gdn_norm_g'], 'm_gdn_w_out': out['m_gdn_w_out'], 'm_w_o': out['m_w_o'], 'm_post_ln_g': out['m_post_ln_g'], 'm_post_ln_b': out['m_post_ln_b'], 'v_w_in': out['v_w_in'], 'v_conf_dw_w': out['v_conf_dw_w'], 'v_conf_dw_b': out['v_conf_dw_b'], 'v_conf_ln_g': out['v_conf_ln_g'], 'v_conf_ln_b': out['v_conf_ln_b'], 'v_conf_w_out': out['v_conf_w_out'], 'v_gdn_conv_w': out['v_gdn_conv_w'], 'v_gdn_A_log': out['v_gdn_A_log'], 'v_gdn_dt_bias': out['v_gdn_dt_bias'], 'v_gdn_norm_g': out['v_gdn_norm_g'], 'v_gdn_w_out': out['v_gdn_w_out'], 'v_w_o': out['v_w_o'], 'v_post_ln_g': out['v_post_ln_g'], 'v_post_ln_b': out['v_post_ln_b']}


def _loss(weights, diff, rest, loss_target):
    with _jax.named_scope("forward"):
        args = {**rest, TWIN_DIFF_INPUT: diff, **{k: w.astype(_WEIGHT_DTYPES[k]) for k, w in weights.items()}}
        y = _forward(args)
    with _jax.named_scope("loss_head"):
        err = _jnp.square(y.astype(_jnp.float32) - loss_target)
        return 0.5 * _jnp.sum(_jnp.mean(err, axis=-1)) if err.ndim else 0.5 * err


def _adamw(w, g, m, v):
    m = ADAM_B1 * m + (1.0 - ADAM_B1) * g
    v = ADAM_B2 * v + (1.0 - ADAM_B2) * _jnp.square(g)
    m_hat = m / (1.0 - ADAM_B1 ** ADAM_STEP)
    v_hat = v / (1.0 - ADAM_B2 ** ADAM_STEP)
    delta = -ADAM_LR * (m_hat / (_jnp.sqrt(v_hat) + ADAM_EPS) + ADAM_WD * w)
    return delta, m, v


def reference(x, w_in, conf_dw_w, conf_dw_b, conf_ln_g, conf_ln_b, conf_w_out, gdn_conv_w, gdn_A_log, gdn_dt_bias, gdn_norm_g, gdn_w_out, w_o, post_ln_g, post_ln_b, loss_target, m_w_in, m_conf_dw_w, m_conf_dw_b, m_conf_ln_g, m_conf_ln_b, m_conf_w_out, m_gdn_conv_w, m_gdn_A_log, m_gdn_dt_bias, m_gdn_norm_g, m_gdn_w_out, m_w_o, m_post_ln_g, m_post_ln_b, v_w_in, v_conf_dw_w, v_conf_dw_b, v_conf_ln_g, v_conf_ln_b, v_conf_w_out, v_gdn_conv_w, v_gdn_A_log, v_gdn_dt_bias, v_gdn_norm_g, v_gdn_w_out, v_w_o, v_post_ln_g, v_post_ln_b):
    given = dict(x=x, w_in=w_in, conf_dw_w=conf_dw_w, conf_dw_b=conf_dw_b, conf_ln_g=conf_ln_g, conf_ln_b=conf_ln_b, conf_w_out=conf_w_out, gdn_conv_w=gdn_conv_w, gdn_A_log=gdn_A_log, gdn_dt_bias=gdn_dt_bias, gdn_norm_g=gdn_norm_g, gdn_w_out=gdn_w_out, w_o=w_o, post_ln_g=post_ln_g, post_ln_b=post_ln_b, loss_target=loss_target, m_w_in=m_w_in, m_conf_dw_w=m_conf_dw_w, m_conf_dw_b=m_conf_dw_b, m_conf_ln_g=m_conf_ln_g, m_conf_ln_b=m_conf_ln_b, m_conf_w_out=m_conf_w_out, m_gdn_conv_w=m_gdn_conv_w, m_gdn_A_log=m_gdn_A_log, m_gdn_dt_bias=m_gdn_dt_bias, m_gdn_norm_g=m_gdn_norm_g, m_gdn_w_out=m_gdn_w_out, m_w_o=m_w_o, m_post_ln_g=m_post_ln_g, m_post_ln_b=m_post_ln_b, v_w_in=v_w_in, v_conf_dw_w=v_conf_dw_w, v_conf_dw_b=v_conf_dw_b, v_conf_ln_g=v_conf_ln_g, v_conf_ln_b=v_conf_ln_b, v_conf_w_out=v_conf_w_out, v_gdn_conv_w=v_gdn_conv_w, v_gdn_A_log=v_gdn_A_log, v_gdn_dt_bias=v_gdn_dt_bias, v_gdn_norm_g=v_gdn_norm_g, v_gdn_w_out=v_gdn_w_out, v_w_o=v_w_o, v_post_ln_g=v_post_ln_g, v_post_ln_b=v_post_ln_b)
    weights = {n: given[n] for n in TWIN_WEIGHTS}
    shared = {n: given[n] for n in SHARED_INPUTS}
    per_example = {n: given[n] for n in ['x']}
    grad_fn = _jax.value_and_grad(_loss, argnums=(0, 1))

    def one_microbatch(ex, loss_target):
        ex = dict(ex)
        diff = ex.pop(TWIN_DIFF_INPUT)
        return grad_fn(weights, diff, {**shared, **ex}, loss_target)

    if N_MICROBATCH == 1:
        loss, (grad_w, grad_x) = one_microbatch(per_example, given["loss_target"])
    else:
        def body(carry, xs):
            loss_sum, grad_sum = carry
            l_k, (gw_k, gx_k) = one_microbatch(xs[0], xs[1])
            with _jax.named_scope("update"):
                return (loss_sum + l_k, _jax.tree.map(_jnp.add, grad_sum, gw_k)), gx_k

        init = (_jnp.zeros((), _jnp.float32), _jax.tree.map(_jnp.zeros_like, weights))
        (loss, grad_w), grad_x = _jax.lax.scan(body, init, (per_example, given["loss_target"]))
    with _jax.named_scope("update"):
        delta_w, new_m, new_v = {}, {}, {}
        for n in TWIN_WEIGHTS:
            delta_w[n], new_m[n], new_v[n] = _adamw(weights[n], grad_w[n], given["m_" + n], given["v_" + n])
    return (loss, grad_x, *[grad_w[n] for n in TWIN_WEIGHTS], *[delta_w[n] for n in TWIN_WEIGHTS],
            *[new_m[n] for n in TWIN_WEIGHTS], *[new_v[n] for n in TWIN_WEIGHTS])
```

```python
import functools

import jax
import jax.numpy as jnp
from jax import lax
from jax.experimental import pallas as pl
from jax.experimental.pallas import tpu as pltpu

F32 = jnp.float32
BF16 = jnp.bfloat16

N_DEV = 8
D = 1024
HEADS = 8
HEAD_DIM = 128
CHUNK = 64
K_CONF = 31
K_GDN = 4
HALO_CONF = 32
HALO_GDN = 8
LN_EPS = 1e-5
RMS_EPS = 1e-6
L2_EPS = 1e-6
DN_ALPHA = 2.0 ** 0.25
ADAM_LR = 0.001
ADAM_B1 = 0.9
ADAM_B2 = 0.999
ADAM_EPS = 1e-08
ADAM_WD = 0.01
ADAM_STEP = 10

W_IN_COLS = 9232
W_IN_SHARD = W_IN_COLS // N_DEV
ROW_WIN = 0
ROW_CWO = 1168
ROW_GWO = ROW_CWO + 128
ROW_WO = ROW_GWO + 128
ROW_DW = ROW_WO + 128
ROW_GC = ROW_DW + 16
PACK_ROWS = 1600
PACK_TILE = 160
SMALL_ROWS = 16
CONVW_ROWS = 16

VMEM_LIMIT = 56 * 1024 * 1024

_NN = ((1,), (0,))
_NT = ((1,), (1,))
_TN = ((0,), (0,))


def _cparams(sem=None):
    return pltpu.CompilerParams(dimension_semantics=sem, vmem_limit_bytes=VMEM_LIMIT)


def _dot(a, b, dims, hi=False):
    dn = (dims, ((), ()))
    a_hi = a.astype(BF16)
    b_hi = b.astype(BF16)
    if not hi:
        return lax.dot_general(a_hi, b_hi, dn, preferred_element_type=F32)
    a_lo = (a - a_hi.astype(F32)).astype(BF16)
    b_lo = (b - b_hi.astype(F32)).astype(BF16)
    d = lambda p, q: lax.dot_general(p, q, dn, preferred_element_type=F32)
    return d(a_hi, b_hi) + (d(a_hi, b_lo) + d(a_lo, b_hi))


def _make_mm(kind, hi):
    dims = {"nn": _NN, "nt": _NT, "tn": _TN}[kind]

    @jax.custom_vjp
    def mm(a, b):
        return _dot(a, b, dims, hi)

    def fwd(a, b):
        return _dot(a, b, dims, hi), (a, b)

    def bwd(res, g):
        a, b = res
        if kind == "nn":
            return _dot(g, b, _NT, hi), _dot(a, g, _TN, hi)
        if kind == "nt":
            return _dot(g, b, _NN, hi), _dot(g, a, _TN, hi)
        return _dot(b, g, _NT, hi), _dot(a, g, _NN, hi)

    mm.defvjp(fwd, bwd)
    return mm


_mm_nn = _make_mm("nn", False)
_mm_nt = _make_mm("nt", False)
_mm_tn = _make_mm("tn", False)
_mm_nn_hi = _make_mm("nn", True)


def _tri_inv_impl(low):
    c = low.shape[0]
    eye = (lax.broadcasted_iota(jnp.int32, (c, c), 0) == lax.broadcasted_iota(jnp.int32, (c, c), 1)).astype(F32)
    m = -low
    p = eye + m
    steps = max(c.bit_length() - 2, 0)
    for _ in range(steps):
        m = _dot(m, m, _NN, True)
        p = p + _dot(p, m, _NN, True)
    return p


@jax.custom_vjp
def _tri_inv(low):
    return _tri_inv_impl(low)


def _tri_inv_fwd(low):
    x = _tri_inv_impl(low)
    return x, x


def _tri_inv_bwd(x, dx):
    t = _dot(x, dx, _TN, True)
    return (-_dot(t, x, _NT, True),)


_tri_inv.defvjp(_tri_inv_fwd, _tri_inv_bwd)


def _sigmoid(x):
    return jax.nn.sigmoid(x)


def _silu(x):
    return x * jax.nn.sigmoid(x)


def _softplus(x):
    u = jnp.exp(-jnp.abs(x))
    log1p_u = jnp.where(u < 1e-3, u * (1.0 - u * (0.5 - u * (1.0 / 3.0))), jnp.log(1.0 + u))
    return jnp.maximum(x, 0.0) + log1p_u


def _layernorm(x, g, b):
    mu = jnp.mean(x, axis=-1, keepdims=True)
    xc = x - mu
    var = jnp.mean(xc * xc, axis=-1, keepdims=True)
    return xc * lax.rsqrt(var + LN_EPS) * g + b


def _pick_lane(x, lane):
    idx = lax.broadcasted_iota(jnp.int32, x.shape, 1)
    return jnp.sum(jnp.where(idx == lane, x, 0.0), axis=1, keepdims=True)


def _gdn_chunk(q_list, k_list, v_list, ba, gz_list, s_list, a_row, dt_row, ng_row):
    c = ba.shape[0]
    rows = lax.broadcasted_iota(jnp.int32, (c, c), 0)
    cols = lax.broadcasted_iota(jnp.int32, (c, c), 1)
    causal = rows >= cols
    strict = rows > cols
    tril = causal.astype(F32)
    eye = (rows == cols).astype(F32)
    ones = jnp.ones((c, c), F32)
    last_row = lax.broadcasted_iota(jnp.int32, (c, 1), 0) == c - 1

    beta_all = _sigmoid(ba)
    g_all = -jnp.exp(a_row) * _softplus(ba + dt_row)
    gc_all = _mm_nn_hi(tril, g_all)

    o_list, s_new = [], []
    for h in range(HEADS):
        q = _silu(q_list[h])
        k = _silu(k_list[h])
        v = _silu(v_list[h])
        q = q * lax.rsqrt(jnp.sum(q * q, axis=-1, keepdims=True) + L2_EPS) * (HEAD_DIM ** -0.5)
        k = k * lax.rsqrt(jnp.sum(k * k, axis=-1, keepdims=True) + L2_EPS)
        beta = _pick_lane(beta_all, h)
        gc = _pick_lane(gc_all, HEADS + h)
        gc_cols = _mm_nn_hi(ones, eye * gc)
        diff = gc - gc_cols
        decay = jnp.where(causal, jnp.exp(jnp.where(causal, diff, 0.0)), 0.0)
        kb = k * beta
        low = jnp.where(strict, _mm_nt(kb, k) * decay, 0.0)
        x = _tri_inv(low)
        eg = jnp.exp(gc)
        u = _mm_nn_hi(x, v * beta)
        w = _mm_nn_hi(x, kb * eg)
        intra = _mm_nt(q, k) * decay
        q_dec = q * eg
        g_last = jnp.sum(jnp.where(last_row, gc, 0.0), axis=0, keepdims=True)
        k_dec = k * jnp.exp(g_last - gc)
        s = s_list[h]
        v_new = u - _mm_nn(w, s)
        o = _mm_nn(q_dec, s) + _mm_nn(intra, v_new)
        s_new.append(s * jnp.exp(g_last) + _mm_tn(k_dec, v_new))
        o = o * lax.rsqrt(jnp.mean(o * o, axis=-1, keepdims=True) + RMS_EPS) * ng_row
        o_list.append(o * _silu(gz_list[h]))
    return o_list, s_new


def _conf_post(cpre, cz, g, b):
    return _silu(_layernorm(cpre, g, b)) * _silu(cz)


def _matmul_nn(a, b, name, tm=512, tn=1024):
    m, k = a.shape
    n = b.shape[1]
    tn = min(tn, n)

    def body(a_ref, b_ref, o_ref):
        o_ref[...] = jnp.dot(a_ref[...], b_ref[...], preferred_element_type=F32)

    return pl.pallas_call(
        body, name=name, grid=(n // tn, m // tm),
        in_specs=[pl.BlockSpec((tm, k), lambda j, i: (i, 0)), pl.BlockSpec((k, tn), lambda j, i: (0, j))],
        out_specs=pl.BlockSpec((tm, tn), lambda j, i: (i, j)),
        out_shape=jax.ShapeDtypeStruct((m, n), F32),
        compiler_params=_cparams(("parallel", "parallel")),
    )(a, b)


def _matmul_tn(a, b, name, tt=512, tn=1024):
    t, k1 = a.shape
    n = b.shape[1]
    tn = min(tn, n)

    def body(a_ref, b_ref, o_ref):
        @pl.when(pl.program_id(1) == 0)
        def _():
            o_ref[...] = jnp.zeros_like(o_ref)

        o_ref[...] += lax.dot_general(a_ref[...], b_ref[...], (_TN, ((), ())), preferred_element_type=F32)

    return pl.pallas_call(
        body, name=name, grid=(n // tn, t // tt),
        in_specs=[pl.BlockSpec((tt, k1), lambda j, i: (i, 0)), pl.BlockSpec((tt, tn), lambda j, i: (i, j))],
        out_specs=pl.BlockSpec((k1, tn), lambda j, i: (0, j)),
        out_shape=jax.ShapeDtypeStruct((k1, n), F32),
        compiler_params=_cparams(("parallel", "arbitrary")),
    )(a, b)


def _matmul_nt_acc(init, a, w, name, tm=512, tk=1024):
    m, n = a.shape
    k1 = w.shape[0]
    tk = min(tk, n)

    def body(i_ref, a_ref, w_ref, o_ref):
        @pl.when(pl.program_id(1) == 0)
        def _():
            o_ref[...] = i_ref[...]

        o_ref[...] += lax.dot_general(a_ref[...], w_ref[...], (_NT, ((), ())), preferred_element_type=F32)

    return pl.pallas_call(
        body, name=name, grid=(m // tm, n // tk),
        in_specs=[pl.BlockSpec((tm, k1), lambda i, j: (i, 0)), pl.BlockSpec((tm, tk), lambda i, j: (i, j)),
                  pl.BlockSpec((k1, tk), lambda i, j: (0, j))],
        out_specs=pl.BlockSpec((tm, k1), lambda i, j: (i, 0)),
        out_shape=jax.ShapeDtypeStruct((m, k1), F32),
        input_output_aliases={0: 0},
        compiler_params=_cparams(("parallel", "arbitrary")),
    )(init, a, w)


def _build_bank(bank_ref, shifts):
    ext = bank_ref[0]
    rows = ext.shape[0]
    for s in shifts:
        if s:
            bank_ref[s] = pltpu.roll(ext, rows - s, axis=0)


def _conv_taps(bank_ref, w_ref, offsets, n_rows, width, emit):
    def piece(rc, carry):
        r0 = pl.multiple_of(rc * 16, 16)
        for cb in range(width // 128):
            lanes = slice(cb * 128, (cb + 1) * 128)
            acc = jnp.zeros((16, 128), F32)
            for k, off in enumerate(offsets):
                m, s = divmod(off, 8)
                acc = acc + bank_ref[s, pl.ds(r0 + 8 * m, 16), lanes] * w_ref[k:k + 1, lanes]
            emit(r0, lanes, acc)
        return carry

    lax.fori_loop(0, n_rows // 16, piece, 0)


def _conv_dw(bank_ref, d_ref, offsets, n_rows, width, emit):
    ms = [divmod(off, 8) for off in offsets]
    for cb in range(width // 128):
        lanes = slice(cb * 128, (cb + 1) * 128)

        def piece(rc, accs, lanes=lanes):
            r0 = pl.multiple_of(rc * 8, 8)
            d = d_ref[pl.ds(r0, 8), lanes]
            return tuple(acc + d * bank_ref[s, pl.ds(r0 + 8 * m, 8), lanes] for acc, (m, s) in zip(accs, ms))

        accs = lax.fori_loop(0, n_rows // 8, piece, tuple(jnp.zeros((8, 128), F32) for _ in offsets))
        for k, acc in enumerate(accs):
            emit(k, lanes, jnp.sum(acc, axis=0, keepdims=True))


def _conf_fwd(proj_conf, dw_w, dw_b, ln_g, ln_b, tt=256):
    t = proj_conf.shape[0]
    hb = tt // HALO_CONF
    offsets = [HALO_CONF - (K_CONF - 1) + k for k in range(K_CONF)]

    def body(cv_ref, cg_ref, cz_ref, cvh_ref, cgh_ref, w_ref, b_ref, g_ref, bb_ref, cpre_ref, aout_ref, bank_ref):
        first = pl.program_id(0) == 0
        halo = cvh_ref[...] * _sigmoid(cgh_ref[...])
        bank_ref[0, 0:HALO_CONF, :] = jnp.where(first, 0.0, halo)
        bank_ref[0, HALO_CONF:, :] = cv_ref[...] * _sigmoid(cg_ref[...])
        _build_bank(bank_ref, range(8))

        def emit(r0, lanes, acc):
            cpre_ref[pl.ds(r0, 16), lanes] = acc + b_ref[0:1, lanes]

        _conv_taps(bank_ref, w_ref, offsets, tt, D, emit)
        aout_ref[...] = _conf_post(cpre_ref[...], cz_ref[...], g_ref[...], bb_ref[...]).astype(BF16)

    row = pl.BlockSpec((1, D), lambda i: (0, 0))
    return pl.pallas_call(
        body, name="conf_fwd", grid=(t // tt,),
        in_specs=[pl.BlockSpec((tt, D), lambda i: (i, 0)), pl.BlockSpec((tt, D), lambda i: (i, 1)),
                  pl.BlockSpec((tt, D), lambda i: (i, 2)),
                  pl.BlockSpec((HALO_CONF, D), lambda i: (jnp.maximum(i * hb - 1, 0), 0)),
                  pl.BlockSpec((HALO_CONF, D), lambda i: (jnp.maximum(i * hb - 1, 0), 1)),
                  pl.BlockSpec((32, D), lambda i: (0, 0)), row, row, row],
        out_specs=[pl.BlockSpec((tt, D), lambda i: (i, 0)), pl.BlockSpec((tt, D), lambda i: (i, 0))],
        out_shape=[jax.ShapeDtypeStruct((t, D), F32), jax.ShapeDtypeStruct((t, D), BF16)],
        scratch_shapes=[pltpu.VMEM((8, tt + HALO_CONF, D), F32)],
        compiler_params=_cparams(("parallel",)),
    )(proj_conf, proj_conf, proj_conf, proj_conf, proj_conf, dw_w, dw_b, ln_g, ln_b)


def _conf_bwd_post(cpre, proj_conf, da_out, ln_g, ln_b, tt=256):
    t = cpre.shape[0]

    def body(c_ref, z_ref, da_ref, g_ref, b_ref, dc_ref, dz_ref, dg_ref, db_ref):
        @pl.when(pl.program_id(0) == 0)
        def _():
            dg_ref[...] = jnp.zeros_like(dg_ref)
            db_ref[...] = jnp.zeros_like(db_ref)

        _, vjp = jax.vjp(_conf_post, c_ref[...], z_ref[...], g_ref[...], b_ref[...])
        dc, dz, dg, db = vjp(da_ref[...])
        dc_ref[...] = dc
        dz_ref[...] = dz.astype(BF16)
        dg_ref[0:1, :] += dg
        db_ref[0:1, :] += db

    row = pl.BlockSpec((1, D), lambda i: (0, 0))
    acc = pl.BlockSpec((8, D), lambda i: (0, 0))
    return pl.pallas_call(
        body, name="conf_bwd_post", grid=(t // tt,),
        in_specs=[pl.BlockSpec((tt, D), lambda i: (i, 0)), pl.BlockSpec((tt, D), lambda i: (i, 2)),
                  pl.BlockSpec((tt, D), lambda i: (i, 0)), row, row],
        out_specs=[pl.BlockSpec((tt, D), lambda i: (i, 0)), pl.BlockSpec((tt, D), lambda i: (i, 0)), acc, acc],
        out_shape=[jax.ShapeDtypeStruct((t, D), F32), jax.ShapeDtypeStruct((t, D), BF16),
                   jax.ShapeDtypeStruct((8, D), F32), jax.ShapeDtypeStruct((8, D), F32)],
        compiler_params=_cparams(("arbitrary",)),
    )(cpre, proj_conf, da_out, ln_g, ln_b)


def _conf_bwd_conv(dcpre, proj_conf, dcz, dw_w, tt=128):
    t = dcpre.shape[0]
    n_tiles = t // tt
    hb = tt // HALO_CONF
    n_hb = t // HALO_CONF
    fwd_offsets = [HALO_CONF - (K_CONF - 1) + k for k in range(K_CONF)]
    bwd_offsets = [K_CONF - 1 - k for k in range(K_CONF)]

    def body(d_ref, dn_ref, cv_ref, cg_ref, cvh_ref, cgh_ref, dz_ref, w_ref, dp_ref, dw_ref, db_ref,
             bank_a, bank_d, da_scr):
        i = pl.program_id(0)

        @pl.when(i == 0)
        def _():
            dw_ref[...] = jnp.zeros_like(dw_ref)
            db_ref[...] = jnp.zeros_like(db_ref)

        cv = cv_ref[...]
        sg = _sigmoid(cg_ref[...])
        bank_a[0, 0:HALO_CONF, :] = jnp.where(i == 0, 0.0, cvh_ref[...] * _sigmoid(cgh_ref[...]))
        bank_a[0, HALO_CONF:, :] = cv * sg
        _build_bank(bank_a, range(8))
        bank_d[0, 0:tt, :] = d_ref[...]
        bank_d[0, tt:, :] = jnp.where(i == n_tiles - 1, 0.0, dn_ref[...])
        _build_bank(bank_d, range(8))

        def emit_da(r0, lanes, acc):
            da_scr[pl.ds(r0, 16), lanes] = acc

        _conv_taps(bank_d, w_ref, bwd_offsets, tt, D, emit_da)
        da = da_scr[...]
        dp_ref[:, 0:D] = (da * sg).astype(BF16)
        dp_ref[:, D:2 * D] = (da * cv * sg * (1.0 - sg)).astype(BF16)
        dp_ref[:, 2 * D:3 * D] = dz_ref[...]

        def emit_dw(k, lanes, row):
            dw_ref[k:k + 1, lanes] += row

        _conv_dw(bank_a, d_ref, fwd_offsets, tt, D, emit_dw)
        db_ref[0:1, :] += jnp.sum(d_ref[...], axis=0, keepdims=True)

    prev = lambda i: jnp.maximum(i * hb - 1, 0)
    nxt = lambda i: jnp.minimum((i + 1) * hb, n_hb - 1)
    return pl.pallas_call(
        body, name="conf_bwd_conv", grid=(n_tiles,),
        in_specs=[pl.BlockSpec((tt, D), lambda i: (i, 0)), pl.BlockSpec((HALO_CONF, D), lambda i: (nxt(i), 0)),
                  pl.BlockSpec((tt, D), lambda i: (i, 0)), pl.BlockSpec((tt, D), lambda i: (i, 1)),
                  pl.BlockSpec((HALO_CONF, D), lambda i: (prev(i), 0)),
                  pl.BlockSpec((HALO_CONF, D), lambda i: (prev(i), 1)),
                  pl.BlockSpec((tt, D), lambda i: (i, 0)), pl.BlockSpec((32, D), lambda i: (0, 0))],
        out_specs=[pl.BlockSpec((tt, 3 * D), lambda i: (i, 0)), pl.BlockSpec((32, D), lambda i: (0, 0)),
                   pl.BlockSpec((8, D), lambda i: (0, 0))],
        out_shape=[jax.ShapeDtypeStruct((t, 3 * D), BF16), jax.ShapeDtypeStruct((32, D), F32),
                   jax.ShapeDtypeStruct((8, D), F32)],
        scratch_shapes=[pltpu.VMEM((8, tt + HALO_CONF, D), F32), pltpu.VMEM((8, tt + HALO_CONF, D), F32),
                        pltpu.VMEM((tt, D), F32)],
        compiler_params=_cparams(("arbitrary",)),
    )(dcpre, dcpre, proj_conf, proj_conf, proj_conf, proj_conf, dcz, dw_w)


def _gdn_conv_fwd(proj_qkv, conv_w, tt=256):
    t, width = proj_qkv.shape
    hb = tt // HALO_GDN
    offsets = [HALO_GDN - (K_GDN - 1) + k for k in range(K_GDN)]
    shifts = sorted({off % 8 for off in offsets})

    def body(x_ref, xh_ref, w_ref, o_ref, bank_ref):
        bank_ref[0, 0:HALO_GDN, :] = jnp.where(pl.program_id(1) == 0, 0.0, xh_ref[...])
        bank_ref[0, HALO_GDN:, :] = x_ref[...]
        _build_bank(bank_ref, shifts)

        def emit(r0, lanes, acc):
            o_ref[pl.ds(r0, 16), lanes] = acc

        _conv_taps(bank_ref, w_ref, offsets, tt, D, emit)

    return pl.pallas_call(
        body, name="gdn_conv_fwd", grid=(width // D, t // tt),
        in_specs=[pl.BlockSpec((tt, D), lambda j, i: (i, j)),
                  pl.BlockSpec((HALO_GDN, D), lambda j, i: (jnp.maximum(i * hb - 1, 0), j)),
                  pl.BlockSpec((8, D), lambda j, i: (0, j))],
        out_specs=pl.BlockSpec((tt, D), lambda j, i: (i, j)),
        out_shape=jax.ShapeDtypeStruct((t, width), F32),
        scratch_shapes=[pltpu.VMEM((8, tt + HALO_GDN, D), F32)],
        compiler_params=_cparams(("parallel", "parallel")),
    )(proj_qkv, proj_qkv, conv_w)


def _gdn_conv_bwd(dqkv_c, proj_qkv, conv_w, tt=256):
    t, width = proj_qkv.shape
    n_tiles = t // tt
    hb = tt // HALO_GDN
    n_hb = t // HALO_GDN
    fwd_offsets = [HALO_GDN - (K_GDN - 1) + k for k in range(K_GDN)]
    bwd_offsets = [K_GDN - 1 - k for k in range(K_GDN)]

    def body(d_ref, dn_ref, x_ref, xh_ref, w_ref, dx_ref, dw_ref, bank_x, bank_d):
        i = pl.program_id(1)

        @pl.when(i == 0)
        def _():
            dw_ref[...] = jnp.zeros_like(dw_ref)

        bank_x[0, 0:HALO_GDN, :] = jnp.where(i == 0, 0.0, xh_ref[...])
        bank_x[0, HALO_GDN:, :] = x_ref[...]
        _build_bank(bank_x, sorted({off % 8 for off in fwd_offsets}))
        bank_d[0, 0:tt, :] = d_ref[...]
        bank_d[0, tt:, :] = jnp.where(i == n_tiles - 1, 0.0, dn_ref[...])
        _build_bank(bank_d, sorted({off % 8 for off in bwd_offsets}))

        def emit_dx(r0, lanes, acc):
            dx_ref[pl.ds(r0, 16), lanes] = acc.astype(BF16)

        _conv_taps(bank_d, w_ref, bwd_offsets, tt, D, emit_dx)

        def emit_dw(k, lanes, row):
            dw_ref[k:k + 1, lanes] += row

        _conv_dw(bank_x, d_ref, fwd_offsets, tt, D, emit_dw)

    return pl.pallas_call(
        body, name="gdn_conv_bwd", grid=(width // D, n_tiles),
        in_specs=[pl.BlockSpec((tt, D), lambda j, i: (i, j)),
                  pl.BlockSpec((HALO_GDN, D), lambda j, i: (jnp.minimum((i + 1) * hb, n_hb - 1), j)),
                  pl.BlockSpec((tt, D), lambda j, i: (i, j)),
                  pl.BlockSpec((HALO_GDN, D), lambda j, i: (jnp.maximum(i * hb - 1, 0), j)),
                  pl.BlockSpec((8, D), lambda j, i: (0, j))],
        out_specs=[pl.BlockSpec((tt, D), lambda j, i: (i, j)), pl.BlockSpec((8, D), lambda j, i: (0, j))],
        out_shape=[jax.ShapeDtypeStruct((t, width), BF16), jax.ShapeDtypeStruct((8, width), F32)],
        scratch_shapes=[pltpu.VMEM((8, tt + HALO_GDN, D), F32), pltpu.VMEM((8, tt + HALO_GDN, D), F32)],
        compiler_params=_cparams(("parallel", "arbitrary")),
    )(dqkv_c, dqkv_c, proj_qkv, proj_qkv, conv_w)


def _head_slices(ref):
    return [ref[:, h * HEAD_DIM:(h + 1) * HEAD_DIM] for h in range(HEADS)]


def _gdn_chunk_fwd(qkv_c, proj_ba, proj_gz, a_row, dt_row, ng_row):
    t = qkv_c.shape[0]
    n_chunks = t // CHUNK

    def body(q_ref, k_ref, v_ref, ba_ref, gz_ref, a_ref, dt_ref, ng_ref, o_ref, ssave_ref, s_scr):
        @pl.when(pl.program_id(0) == 0)
        def _():
            s_scr[...] = jnp.zeros_like(s_scr)

        s_list = [s_scr[h] for h in range(HEADS)]
        for h in range(HEADS):
            ssave_ref[0, h] = s_list[h]
        o_list, s_new = _gdn_chunk(_head_slices(q_ref), _head_slices(k_ref), _head_slices(v_ref), ba_ref[...],
                                   _head_slices(gz_ref), s_list, a_ref[...], dt_ref[...], ng_ref[...])
        for h in range(HEADS):
            o_ref[:, h * HEAD_DIM:(h + 1) * HEAD_DIM] = o_list[h].astype(BF16)
            s_scr[h] = s_new[h]

    row = pl.BlockSpec((1, HEAD_DIM), lambda i: (0, 0))
    return pl.pallas_call(
        body, name="gdn_chunk_fwd", grid=(n_chunks,),
        in_specs=[pl.BlockSpec((CHUNK, D), lambda i: (i, 0)), pl.BlockSpec((CHUNK, D), lambda i: (i, 1)),
                  pl.BlockSpec((CHUNK, D), lambda i: (i, 2)), pl.BlockSpec((CHUNK, HEAD_DIM), lambda i: (i, 0)),
                  pl.BlockSpec((CHUNK, D), lambda i: (i, 0)), row, row, row],
        out_specs=[pl.BlockSpec((CHUNK, D), lambda i: (i, 0)),
                   pl.BlockSpec((1, HEADS, HEAD_DIM, HEAD_DIM), lambda i: (i, 0, 0, 0))],
        out_shape=[jax.ShapeDtypeStruct((t, D), BF16),
                   jax.ShapeDtypeStruct((n_chunks, HEADS, HEAD_DIM, HEAD_DIM), F32)],
        scratch_shapes=[pltpu.VMEM((HEADS, HEAD_DIM, HEAD_DIM), F32)],
        compiler_params=_cparams(("arbitrary",)),
    )(qkv_c, qkv_c, qkv_c, proj_ba, proj_gz, a_row, dt_row, ng_row)


def _gdn_chunk_bwd(qkv_c, proj_ba, proj_gz, s_saved, do_gated, a_row, dt_row, ng_row):
    t = qkv_c.shape[0]
    n_chunks = t // CHUNK

    def body(q_ref, k_ref, v_ref, ba_ref, gz_ref, s_ref, do_ref, a_ref, dt_ref, ng_ref,
             dqkv_ref, dba_ref, dgz_ref, da_ref, ddt_ref, dng_ref, ds_scr):
        @pl.when(pl.program_id(0) == 0)
        def _():
            ds_scr[...] = jnp.zeros_like(ds_scr)
            da_ref[...] = jnp.zeros_like(da_ref)
            ddt_ref[...] = jnp.zeros_like(ddt_ref)
            dng_ref[...] = jnp.zeros_like(dng_ref)

        s_list = [s_ref[0, h] for h in range(HEADS)]
        _, vjp = jax.vjp(_gdn_chunk, _head_slices(q_ref), _head_slices(k_ref), _head_slices(v_ref), ba_ref[...],
                         _head_slices(gz_ref), s_list, a_ref[...], dt_ref[...], ng_ref[...])
        do_list = [do_ref[:, h * HEAD_DIM:(h + 1) * HEAD_DIM] for h in range(HEADS)]
        ds_list = [ds_scr[h] for h in range(HEADS)]
        dq, dk, dv, dba, dgz, ds_in, da, ddt, dng = vjp((do_list, ds_list))
        for h in range(HEADS):
            lanes = slice(h * HEAD_DIM, (h + 1) * HEAD_DIM)
            dqkv_ref[:, h * HEAD_DIM:(h + 1) * HEAD_DIM] = dq[h]
            dqkv_ref[:, D + h * HEAD_DIM:D + (h + 1) * HEAD_DIM] = dk[h]
            dqkv_ref[:, 2 * D + h * HEAD_DIM:2 * D + (h + 1) * HEAD_DIM] = dv[h]
            dgz_ref[:, lanes] = dgz[h].astype(BF16)
            ds_scr[h] = ds_in[h]
        dba_ref[...] = dba.astype(BF16)
        da_ref[0:1, :] += da
        ddt_ref[0:1, :] += ddt
        dng_ref[0:1, :] += dng

    rev = lambda i: n_chunks - 1 - i
    row = pl.BlockSpec((1, HEAD_DIM), lambda i: (0, 0))
    acc = pl.BlockSpec((8, HEAD_DIM), lambda i: (0, 0))
    outs = pl.pallas_call(
        body, name="gdn_chunk_bwd", grid=(n_chunks,),
        in_specs=[pl.BlockSpec((CHUNK, D), lambda i: (rev(i), 0)), pl.BlockSpec((CHUNK, D), lambda i: (rev(i), 1)),
                  pl.BlockSpec((CHUNK, D), lambda i: (rev(i), 2)),
                  pl.BlockSpec((CHUNK, HEAD_DIM), lambda i: (rev(i), 0)),
                  pl.BlockSpec((CHUNK, D), lambda i: (rev(i), 0)),
                  pl.BlockSpec((1, HEADS, HEAD_DIM, HEAD_DIM), lambda i: (rev(i), 0, 0, 0)),
                  pl.BlockSpec((CHUNK, D), lambda i: (rev(i), 0)), row, row, row],
        out_specs=[pl.BlockSpec((CHUNK, 3 * D), lambda i: (rev(i), 0)),
                   pl.BlockSpec((CHUNK, HEAD_DIM), lambda i: (rev(i), 0)),
                   pl.BlockSpec((CHUNK, D), lambda i: (rev(i), 0)), acc, acc, acc],
        out_shape=[jax.ShapeDtypeStruct((t, 3 * D), F32)]
        + [jax.ShapeDtypeStruct((t, HEAD_DIM), BF16), jax.ShapeDtypeStruct((t, D), BF16)]
        + [jax.ShapeDtypeStruct((8, HEAD_DIM), F32)] * 3,
        scratch_shapes=[pltpu.VMEM((HEADS, HEAD_DIM, HEAD_DIM), F32)],
        compiler_params=_cparams(("arbitrary",)),
    )(qkv_c, qkv_c, qkv_c, proj_ba, proj_gz, s_saved, do_gated, a_row, dt_row, ng_row)
    return outs


def _merge(a_out, o_gated, proj_gate, x, target, w_conf, w_gdn, w_o, pg, pb, tt=256):
    t = x.shape[0]

    def body(a_ref, o_ref, gt_ref, x_ref, y_ref, wc_ref, wg_ref, wo_ref, pg_ref, pb_ref,
             loss_ref, dpg_ref, dpb_ref, dx_ref, dgt_ref, da_ref, do_ref, h_ref, ds_ref, dyc_ref, dyg_ref):
        @pl.when(pl.program_id(0) == 0)
        def _():
            loss_ref[...] = jnp.zeros_like(loss_ref)
            dpg_ref[...] = jnp.zeros_like(dpg_ref)
            dpb_ref[...] = jnp.zeros_like(dpb_ref)

        wc, wg, wo = wc_ref[...], wg_ref[...], wo_ref[...]
        y_conf = _dot(a_ref[...], wc, _NN)
        y_gdn = _dot(o_ref[...], wg, _NN)
        sc = _sigmoid(gt_ref[:, 0:D])
        sg = _sigmoid(gt_ref[:, D:2 * D])
        h = sc * y_conf + sg * y_gdn
        z = DN_ALPHA * x_ref[...] + _dot(h, wo, _NN)
        mu = jnp.mean(z, axis=-1, keepdims=True)
        zc = z - mu
        rstd = lax.rsqrt(jnp.mean(zc * zc, axis=-1, keepdims=True) + LN_EPS)
        xhat = zc * rstd
        gain = pg_ref[...]
        err = xhat * gain + pb_ref[...] - y_ref[...]
        tok = jnp.mean(err * err, axis=-1, keepdims=True)
        loss_ref[...] += 0.5 * jnp.sum(tok, axis=0, keepdims=True)

        dy = err * (1.0 / D)
        dpg_ref[0:1, :] += jnp.sum(dy * xhat, axis=0, keepdims=True)
        dpb_ref[0:1, :] += jnp.sum(dy, axis=0, keepdims=True)
        dxh = dy * gain
        dz = rstd * (dxh - jnp.mean(dxh, axis=-1, keepdims=True)
                     - xhat * jnp.mean(dxh * xhat, axis=-1, keepdims=True))
        dx_ref[...] = DN_ALPHA * dz
        dh = _dot(dz, wo, _NT)
        dyc = dh * sc
        dyg = dh * sg
        dgt_ref[:, 0:D] = (dh * y_conf * sc * (1.0 - sc)).astype(BF16)
        dgt_ref[:, D:2 * D] = (dh * y_gdn * sg * (1.0 - sg)).astype(BF16)
        da_ref[...] = _dot(dyc, wc, _NT)
        do_ref[...] = _dot(dyg, wg, _NT)
        h_ref[...] = h.astype(BF16)
        ds_ref[...] = dz.astype(BF16)
        dyc_ref[...] = dyc.astype(BF16)
        dyg_ref[...] = dyg.astype(BF16)

    tile = pl.BlockSpec((tt, D), lambda i: (i, 0))
    wide = pl.BlockSpec((tt, 2 * D), lambda i: (i, 0))
    mat = pl.BlockSpec((D, D), lambda i: (0, 0))
    row = pl.BlockSpec((1, D), lambda i: (0, 0))
    acc = pl.BlockSpec((8, D), lambda i: (0, 0))
    act = lambda dt: jax.ShapeDtypeStruct((t, D), dt)
    return pl.pallas_call(
        body, name="merge", grid=(t // tt,),
        in_specs=[tile, tile, wide, tile, tile, mat, mat, mat, row, row],
        out_specs=[pl.BlockSpec((8, 128), lambda i: (0, 0)), acc, acc, tile, wide, tile, tile, tile, tile, tile, tile],
        out_shape=[jax.ShapeDtypeStruct((8, 128), F32), jax.ShapeDtypeStruct((8, D), F32),
                   jax.ShapeDtypeStruct((8, D), F32), act(F32), jax.ShapeDtypeStruct((t, 2 * D), BF16),
                   act(F32), act(F32), act(BF16), act(BF16), act(BF16), act(BF16)],
        compiler_params=_cparams(("arbitrary",)),
    )(a_out, o_gated, proj_gate, x, target, w_conf, w_gdn, w_o, pg, pb)


def _mesh_place():
    x, y, c = lax.axis_index("x"), lax.axis_index("y"), lax.axis_index("c")
    return x, y, c


def _flat(px, py, pc):
    return 4 * px + 2 * py + pc


def _all_gather(shards):
    n = len(shards)

    def body(*refs):
        ins, outs = refs[:n], refs[n:2 * n]
        send_sems, recv_sems, local_sems = refs[2 * n:]
        x, y, c = _mesh_place()
        me, sibling = (x, y, c), (x, y, 1 - c)
        chips = [(1 - x, y), (x, 1 - y), (1 - x, 1 - y)]

        def copy(a, k, block, to, src=None):
            dst = outs[a].at[_flat(*block)]
            return pltpu.make_async_remote_copy(
                src_ref=dst if src is None else src, dst_ref=dst,
                send_sem=send_sems.at[a, k], recv_sem=recv_sems.at[a, k],
                device_id=to, device_id_type=pl.DeviceIdType.MESH)

        mine = [pltpu.make_async_copy(ins[a], outs[a].at[_flat(*me)], local_sems.at[a]) for a in range(n)]
        for cp in mine:
            cp.start()
        first = []
        for a in range(n):
            first.append(copy(a, 0, me, sibling, src=ins[a]))
            first += [copy(a, 1 + j, me, (*chip, c), src=ins[a]) for j, chip in enumerate(chips)]
        for cp in first:
            cp.start()
        passed = []
        for j, chip in enumerate(chips):
            for a in range(n):
                copy(a, 1 + j, (*chip, c), me).wait_recv()
                fwd = copy(a, 4 + j, (*chip, c), sibling)
                fwd.start()
                passed.append(fwd)
        for a in range(n):
            copy(a, 0, sibling, me).wait_recv()
            for j, chip in enumerate(chips):
                copy(a, 4 + j, (*chip, 1 - c), me).wait_recv()
        for cp in first + passed:
            cp.wait_send()
        for cp in mine:
            cp.wait()

    any_spec = pl.BlockSpec(memory_space=pl.ANY)
    return pl.pallas_call(
        body, name="all_gather_weights",
        in_specs=[any_spec] * n, out_specs=[any_spec] * n,
        out_shape=[jax.ShapeDtypeStruct((N_DEV,) + s.shape, s.dtype) for s in shards],
        scratch_shapes=[pltpu.SemaphoreType.DMA((n, 7)), pltpu.SemaphoreType.DMA((n, 7)),
                        pltpu.SemaphoreType.DMA((n,))],
    )(*shards)


def _exchange(grad_blocks, small):
    def body(g_ref, s_ref, land_ref, sall_ref, send_sems, recv_sems, local_sems):
        x, y, c = _mesh_place()
        me = _flat(x, y, c)
        mine = [pltpu.make_async_copy(g_ref.at[me], land_ref.at[me], local_sems.at[0]),
                pltpu.make_async_copy(s_ref, sall_ref.at[me], local_sems.at[1])]
        for cp in mine:
            cp.start()
        sends, recvs = [], []
        for k in range(7):
            mask = k + 1
            px = 1 - x if mask & 4 else x
            py = 1 - y if mask & 2 else y
            pc = 1 - c if mask & 1 else c
            peer = _flat(px, py, pc)
            kw = dict(send_sem=send_sems.at[0, k], recv_sem=recv_sems.at[0, k],
                      device_id=(px, py, pc), device_id_type=pl.DeviceIdType.MESH)
            sends.append(pltpu.make_async_remote_copy(src_ref=g_ref.at[peer], dst_ref=land_ref.at[me], **kw))
            recvs.append(pltpu.make_async_remote_copy(src_ref=g_ref.at[peer], dst_ref=land_ref.at[peer], **kw))
            kw = dict(send_sem=send_sems.at[1, k], recv_sem=recv_sems.at[1, k],
                      device_id=(px, py, pc), device_id_type=pl.DeviceIdType.MESH)
            sends.append(pltpu.make_async_remote_copy(src_ref=s_ref, dst_ref=sall_ref.at[me], **kw))
            recvs.append(pltpu.make_async_remote_copy(src_ref=s_ref, dst_ref=sall_ref.at[peer], **kw))
        for cp in sends:
            cp.start()
        for cp in recvs:
            cp.wait_recv()
        for cp in sends:
            cp.wait_send()
        for cp in mine:
            cp.wait()

    any_spec = pl.BlockSpec(memory_space=pl.ANY)
    return pl.pallas_call(
        body, name="exchange_grads",
        in_specs=[any_spec, any_spec], out_specs=[any_spec, any_spec],
        out_shape=[jax.ShapeDtypeStruct(grad_blocks.shape, grad_blocks.dtype),
                   jax.ShapeDtypeStruct((N_DEV,) + small.shape, small.dtype)],
        scratch_shapes=[pltpu.SemaphoreType.DMA((2, 7)), pltpu.SemaphoreType.DMA((2, 7)),
                        pltpu.SemaphoreType.DMA((2,))],
    )(grad_blocks, small)


def _adamw(parts, w, m, v, name, tile):
    rows, cols = w.shape

    def body(p_ref, w_ref, m_ref, v_ref, g_ref, d_ref, nm_ref, nv_ref):
        g = p_ref[0].astype(F32)
        for s in range(1, N_DEV):
            g = g + p_ref[s].astype(F32)
        nm = ADAM_B1 * m_ref[...] + (1.0 - ADAM_B1) * g
        nv = ADAM_B2 * v_ref[...] + (1.0 - ADAM_B2) * jnp.square(g)
        m_hat = nm / (1.0 - ADAM_B1 ** ADAM_STEP)
        v_hat = nv / (1.0 - ADAM_B2 ** ADAM_STEP)
        g_ref[...] = g
        d_ref[...] = -ADAM_LR * (m_hat / (jnp.sqrt(v_hat) + ADAM_EPS) + ADAM_WD * w_ref[...])
        nm_ref[...] = nm
        nv_ref[...] = nv

    blk = pl.BlockSpec((tile, cols), lambda i: (i, 0))
    out = jax.ShapeDtypeStruct((rows, cols), F32)
    return pl.pallas_call(
        body, name=name, grid=(rows // tile,),
        in_specs=[pl.BlockSpec((N_DEV, tile, cols), lambda i: (0, i, 0)), blk, blk, blk],
        out_specs=[blk, blk, blk, blk], out_shape=[out, out, out, out],
        compiler_params=_cparams(("parallel",)),
    )(parts, w, m, v)


def _rows_of(flat, n_rows):
    flat = flat.reshape(-1)
    return jnp.pad(flat, (0, n_rows * D - flat.shape[0])).reshape(n_rows, D)


def _pack_shards(w_in, conf_w_out, gdn_w_out, w_o, conf_dw_w, gdn_conv_w):
    parts = [_rows_of(w_in, ROW_CWO - ROW_WIN), conf_w_out, gdn_w_out, w_o,
             _rows_of(conf_dw_w, 16), _rows_of(gdn_conv_w, 16),
             jnp.zeros((PACK_ROWS - ROW_GC - 16, D), F32)]
    return jnp.concatenate(parts, axis=0)


def _unpack_shards(p):
    w_in = p[ROW_WIN:ROW_WIN + W_IN_SHARD].reshape(D, W_IN_SHARD)
    dw = p[ROW_DW:ROW_DW + 4].reshape(-1)[:K_CONF * 128].reshape(K_CONF, 128)
    gc = p[ROW_GC:ROW_GC + 2].reshape(-1)[:K_GDN * 384].reshape(K_GDN, 384)
    return w_in, p[ROW_CWO:ROW_CWO + 128], p[ROW_GWO:ROW_GWO + 128], p[ROW_WO:ROW_WO + 128], dw, gc


def _pack_small(dw_b, ln_g, ln_b, pg, pb, ng, a_log, dt_bias, loss=None):
    s = jnp.zeros((SMALL_ROWS, D), F32)
    for r, val in enumerate((dw_b, ln_g, ln_b, pg, pb, ng, a_log, dt_bias)):
        s = s.at[r, :val.shape[0]].set(val)
    if loss is not None:
        s = s.at[8, 0].set(loss)
    return s


def _unpack_small(s):
    return (s[0], s[1], s[2], s[3], s[4], s[5, :HEAD_DIM], s[6, :HEADS], s[7, :HEADS])


def _scatter_blocks(g_w_in, g_cwo, g_gwo, g_wo, g_dw, g_gc):
    win = g_w_in.reshape(D, N_DEV, W_IN_SHARD).transpose(1, 0, 2).reshape(N_DEV, W_IN_SHARD, D)
    win = jnp.pad(win, ((0, 0), (0, ROW_CWO - ROW_WIN - W_IN_SHARD), (0, 0)))
    dw = g_dw.reshape(K_CONF, N_DEV, 128).transpose(1, 0, 2).reshape(N_DEV, K_CONF * 128)
    dw = jnp.pad(dw, ((0, 0), (0, 16 * D - K_CONF * 128))).reshape(N_DEV, 16, D)
    gc = g_gc.reshape(K_GDN, N_DEV, 384).transpose(1, 0, 2).reshape(N_DEV, K_GDN * 384)
    gc = jnp.pad(gc, ((0, 0), (0, 16 * D - K_GDN * 384))).reshape(N_DEV, 16, D)
    tail = jnp.zeros((N_DEV, PACK_ROWS - ROW_GC - 16, D), F32)
    return jnp.concatenate([win, g_cwo.reshape(N_DEV, 128, D), g_gwo.reshape(N_DEV, 128, D),
                            g_wo.reshape(N_DEV, 128, D), dw, gc, tail], axis=1)


def kernel(x, w_in, conf_dw_w, conf_dw_b, conf_ln_g, conf_ln_b, conf_w_out, gdn_conv_w, gdn_A_log, gdn_dt_bias, gdn_norm_g, gdn_w_out, w_o, post_ln_g, post_ln_b, loss_target, m_w_in, m_conf_dw_w, m_conf_dw_b, m_conf_ln_g, m_conf_ln_b, m_conf_w_out, m_gdn_conv_w, m_gdn_A_log, m_gdn_dt_bias, m_gdn_norm_g, m_gdn_w_out, m_w_o, m_post_ln_g, m_post_ln_b, v_w_in, v_conf_dw_w, v_conf_dw_b, v_conf_ln_g, v_conf_ln_b, v_conf_w_out, v_gdn_conv_w, v_gdn_A_log, v_gdn_dt_bias, v_gdn_norm_g, v_gdn_w_out, v_w_o, v_post_ln_g, v_post_ln_b):
    t = x.shape[1]
    x2 = x.reshape(t, D)
    target = loss_target.reshape(t, D)
    x_bf = x2.astype(BF16)

    w_pack = _pack_shards(w_in, conf_w_out, gdn_w_out, w_o, conf_dw_w, gdn_conv_w)
    convw = jnp.concatenate([_rows_of(conf_dw_w, 8), _rows_of(gdn_conv_w, 8)], axis=0)
    all_w, all_convw = _all_gather([w_pack.astype(BF16), convw])
    w_full = all_w[:, ROW_WIN:ROW_WIN + W_IN_SHARD].reshape(N_DEV, D, W_IN_SHARD).transpose(1, 0, 2)
    w_full = w_full.reshape(D, W_IN_COLS)
    w_conf = w_full[:, 0:3 * D]
    w_qkv = w_full[:, 3 * D:6 * D]
    w_gz = w_full[:, 6 * D:7 * D]
    w_ba = jnp.pad(w_full[:, 7 * D:7 * D + 2 * HEADS], ((0, 0), (0, HEAD_DIM - 2 * HEADS)))
    w_gate = w_full[:, 7 * D + 2 * HEADS:]
    cwo_full = all_w[:, ROW_CWO:ROW_CWO + 128].reshape(D, D)
    gwo_full = all_w[:, ROW_GWO:ROW_GWO + 128].reshape(D, D)
    wo_full = all_w[:, ROW_WO:ROW_WO + 128].reshape(D, D)
    dw_full = all_convw[:, 0:4].reshape(N_DEV, 4 * D)[:, :K_CONF * 128].reshape(N_DEV, K_CONF, 128)
    dw_full = jnp.pad(dw_full.transpose(1, 0, 2).reshape(K_CONF, D), ((0, 32 - K_CONF), (0, 0)))
    gc_full = all_convw[:, 8:10].reshape(N_DEV, 2 * D)[:, :K_GDN * 384].reshape(N_DEV, K_GDN, 384)
    gc_full = jnp.pad(gc_full.transpose(1, 0, 2).reshape(K_GDN, 3 * D), ((0, 8 - K_GDN), (0, 0)))

    row = lambda vec: vec.reshape(1, -1)
    lane_row = lambda vec, at: jnp.zeros((1, HEAD_DIM), F32).at[0, at:at + vec.shape[0]].set(vec)
    a_row = lane_row(gdn_A_log, HEADS)
    dt_row = lane_row(gdn_dt_bias, HEADS)
    ng_row = row(gdn_norm_g)

    proj_conf = _matmul_nn(x_bf, w_conf, "proj_conf")
    proj_qkv = _matmul_nn(x_bf, w_qkv, "proj_qkv")
    proj_gz = _matmul_nn(x_bf, w_gz, "proj_gz")
    proj_gate = _matmul_nn(x_bf, w_gate, "proj_gate")
    proj_ba = _matmul_nn(x_bf, w_ba, "proj_ba")
    cpre, a_out = _conf_fwd(proj_conf, dw_full, row(conf_dw_b), row(conf_ln_g), row(conf_ln_b))
    qkv_c = _gdn_conv_fwd(proj_qkv, gc_full)
    o_gated, s_saved = _gdn_chunk_fwd(qkv_c, proj_ba, proj_gz, a_row, dt_row, ng_row)

    (loss_acc, d_pg, d_pb, dx, d_gate, da_out, do_gated, h_bf, dsub_bf, dyc_bf, dyg_bf) = _merge(
        a_out, o_gated, proj_gate, x2, target, cwo_full, gwo_full, wo_full, row(post_ln_g), row(post_ln_b))
    g_wo = _matmul_tn(h_bf, dsub_bf, "grad_w_o")
    g_cwo = _matmul_tn(a_out, dyc_bf, "grad_conf_w_out")
    g_gwo = _matmul_tn(o_gated, dyg_bf, "grad_gdn_w_out")

    dqkv_c, d_ba, d_gz, d_a_row, d_dt_row, d_ng_row = _gdn_chunk_bwd(
        qkv_c, proj_ba, proj_gz, s_saved, do_gated, a_row, dt_row, ng_row)
    d_qkv, g_gc = _gdn_conv_bwd(dqkv_c, proj_qkv, gc_full)

    dcpre, dcz, d_ln_g, d_ln_b = _conf_bwd_post(cpre, proj_conf, da_out, row(conf_ln_g), row(conf_ln_b))
    d_conf, g_dw, g_dwb = _conf_bwd_conv(dcpre, proj_conf, dcz, dw_full)

    segments = [(d_conf, w_conf, "conf"), (d_qkv, w_qkv, "qkv"), (d_gz, w_gz, "gz"),
                (d_ba, w_ba, "ba"), (d_gate, w_gate, "gate")]
    g_cols = {}
    for d_seg, w_seg, tag in segments:
        g_cols[tag] = _matmul_tn(x_bf, d_seg, "grad_w_in_" + tag)
        dx = _matmul_nt_acc(dx, d_seg, w_seg, "grad_x_" + tag)
    g_w_in = jnp.concatenate([g_cols["conf"], g_cols["qkv"], g_cols["gz"], g_cols["ba"][:, :2 * HEADS],
                              g_cols["gate"]], axis=1)

    blocks = _scatter_blocks(g_w_in, g_cwo, g_gwo, g_wo, g_dw[:K_CONF], g_gc[:K_GDN])
    small = _pack_small(g_dwb[0], d_ln_g[0], d_ln_b[0], d_pg[0], d_pb[0], d_ng_row[0],
                        d_a_row[0, HEADS:2 * HEADS], d_dt_row[0, HEADS:2 * HEADS], loss_acc[0, 0])
    landed, small_all = _exchange(blocks, small)

    m_pack = _pack_shards(m_w_in, m_conf_w_out, m_gdn_w_out, m_w_o, m_conf_dw_w, m_gdn_conv_w)
    v_pack = _pack_shards(v_w_in, v_conf_w_out, v_gdn_w_out, v_w_o, v_conf_dw_w, v_gdn_conv_w)
    big = _adamw(landed, w_pack, m_pack, v_pack, "adamw_shards", PACK_TILE)
    ws = _pack_small(conf_dw_b, conf_ln_g, conf_ln_b, post_ln_g, post_ln_b, gdn_norm_g, gdn_A_log, gdn_dt_bias)
    ms = _pack_small(m_conf_dw_b, m_conf_ln_g, m_conf_ln_b, m_post_ln_g, m_post_ln_b, m_gdn_norm_g, m_gdn_A_log,
                     m_gdn_dt_bias)
    vs = _pack_small(v_conf_dw_b, v_conf_ln_g, v_conf_ln_b, v_post_ln_g, v_post_ln_b, v_gdn_norm_g, v_gdn_A_log,
                     v_gdn_dt_bias)
    sml = _adamw(small_all, ws, ms, vs, "adamw_replicated", SMALL_ROWS)

    loss = sml[0][8, 0]
    outs = []
    for big_k, sml_k in zip(big, sml):
        b_w_in, b_cwo, b_gwo, b_wo, b_dw, b_gc = _unpack_shards(big_k)
        s_dwb, s_lng, s_lnb, s_pg, s_pb, s_ng, s_a, s_dt = _unpack_small(sml_k)
        outs.append([b_w_in, b_dw, s_dwb, s_lng, s_lnb, b_cwo, b_gc, s_a, s_dt, s_ng, b_gwo, b_wo, s_pg, s_pb])
    return (loss, dx.reshape(1, t, D), *outs[0], *outs[1], *outs[2], *outs[3])
```

```python
import functools

import jax
import jax.numpy as jnp
from jax import lax
from jax.experimental import pallas as pl
from jax.experimental.pallas import tpu as pltpu

F32 = jnp.float32
BF16 = jnp.bfloat16

N_DEV = 8
D = 1024
HEADS = 8
HEAD_DIM = 128
CHUNK = 64
K_CONF = 31
K_GDN = 4
HALO_CONF = 32
HALO_GDN = 8
LN_EPS = 1e-5
RMS_EPS = 1e-6
L2_EPS = 1e-6
DN_ALPHA = 2.0 ** 0.25
ADAM_LR = 0.001
ADAM_B1 = 0.9
ADAM_B2 = 0.999
ADAM_EPS = 1e-08
ADAM_WD = 0.01
ADAM_STEP = 10

W_IN_COLS = 9232
W_IN_SHARD = W_IN_COLS // N_DEV
ROW_CWO = 0
ROW_GWO = ROW_CWO + 128
ROW_WO = ROW_GWO + 128
ROW_DW = ROW_WO + 128
ROW_GC = ROW_DW + 16
PACK_ROWS = ROW_GC + 16
PACK_TILE = PACK_ROWS // 2
W_IN_TILE = 128
SMALL_ROWS = 16
CONVW_ROWS = 16

VMEM_LIMIT = 56 * 1024 * 1024

_NN = ((1,), (0,))
_NT = ((1,), (1,))
_TN = ((0,), (0,))


def _cparams(sem=None):
    return pltpu.CompilerParams(dimension_semantics=sem, vmem_limit_bytes=VMEM_LIMIT)


def _dot(a, b, dims, hi=False):
    dn = (dims, ((), ()))
    a_hi = a.astype(BF16)
    b_hi = b.astype(BF16)
    if not hi:
        return lax.dot_general(a_hi, b_hi, dn, preferred_element_type=F32)
    a_lo = (a - a_hi.astype(F32)).astype(BF16)
    b_lo = (b - b_hi.astype(F32)).astype(BF16)
    d = lambda p, q: lax.dot_general(p, q, dn, preferred_element_type=F32)
    return d(a_hi, b_hi) + (d(a_hi, b_lo) + d(a_lo, b_hi))


def _make_mm(kind, hi):
    dims = {"nn": _NN, "nt": _NT, "tn": _TN}[kind]

    @jax.custom_vjp
    def mm(a, b):
        return _dot(a, b, dims, hi)

    def fwd(a, b):
        return _dot(a, b, dims, hi), (a, b)

    def bwd(res, g):
        a, b = res
        if kind == "nn":
            return _dot(g, b, _NT, hi), _dot(a, g, _TN, hi)
        if kind == "nt":
            return _dot(g, b, _NN, hi), _dot(g, a, _TN, hi)
        return _dot(b, g, _NT, hi), _dot(a, g, _NN, hi)

    mm.defvjp(fwd, bwd)
    return mm


_mm_nn = _make_mm("nn", False)
_mm_nt = _make_mm("nt", False)
_mm_tn = _make_mm("tn", False)
_mm_nn_hi = _make_mm("nn", True)
_mm_tn_hi = _make_mm("tn", True)


def _tri_inv_impl(lows):
    c = lows[0].shape[0]
    eye = (lax.broadcasted_iota(jnp.int32, (c, c), 0) == lax.broadcasted_iota(jnp.int32, (c, c), 1)).astype(F32)
    ms = [-low for low in lows]
    ps = [eye + m for m in ms]
    steps = max(c.bit_length() - 2, 0)
    for _ in range(steps):
        ms = [_dot(m, m, _NN, True) for m in ms]
        ps = [p + _dot(p, m, _NN, True) for p, m in zip(ps, ms)]
    return ps


@jax.custom_vjp
def _tri_inv(lows):
    return _tri_inv_impl(lows)


def _tri_inv_fwd(lows):
    xs = _tri_inv_impl(lows)
    return xs, xs


def _tri_inv_bwd(xs, dxs):
    ts = [_dot(x, dx, _TN, True) for x, dx in zip(xs, dxs)]
    return ([-_dot(t, x, _NT, True) for t, x in zip(ts, xs)],)


_tri_inv.defvjp(_tri_inv_fwd, _tri_inv_bwd)


def _sigmoid(x):
    return jax.nn.sigmoid(x)


def _silu(x):
    return x * jax.nn.sigmoid(x)


def _softplus(x):
    u = jnp.exp(-jnp.abs(x))
    log1p_u = jnp.where(u < 1e-3, u * (1.0 - u * (0.5 - u * (1.0 / 3.0))), jnp.log(1.0 + u))
    return jnp.maximum(x, 0.0) + log1p_u


def _layernorm(x, g, b):
    mu = jnp.mean(x, axis=-1, keepdims=True)
    xc = x - mu
    var = jnp.mean(xc * xc, axis=-1, keepdims=True)
    return xc * lax.rsqrt(var + LN_EPS) * g + b


def _pick_lane(x, lane):
    idx = lax.broadcasted_iota(jnp.int32, x.shape, 1)
    return jnp.sum(jnp.where(idx == lane, x, 0.0), axis=1, keepdims=True)


def _gdn_chunk(q_list, k_list, v_list, ba, gz_list, s_list, a_row, dt_row, ng_row):
    c = ba.shape[0]
    heads = range(HEADS)
    rows = lax.broadcasted_iota(jnp.int32, (c, c), 0)
    cols = lax.broadcasted_iota(jnp.int32, (c, c), 1)
    causal = rows >= cols
    strict = rows > cols
    tril = causal.astype(F32)
    triu = (rows <= cols).astype(F32)
    last_row = lax.broadcasted_iota(jnp.int32, (c, 1), 0) == c - 1
    sub8 = lax.broadcasted_iota(jnp.int32, (HEADS, c), 0)

    beta_all = _sigmoid(ba)
    g_all = -jnp.exp(a_row) * _softplus(ba + dt_row)
    gc_all = _mm_nn_hi(tril, g_all)
    gc_t = _mm_tn_hi(g_all, triu)[HEADS:2 * HEADS, :]

    q = [_silu(a) for a in q_list]
    k = [_silu(a) for a in k_list]
    v = [_silu(a) for a in v_list]
    q = [a * lax.rsqrt(jnp.sum(a * a, axis=-1, keepdims=True) + L2_EPS) * (HEAD_DIM ** -0.5) for a in q]
    k = [a * lax.rsqrt(jnp.sum(a * a, axis=-1, keepdims=True) + L2_EPS) for a in k]
    beta = [_pick_lane(beta_all, h) for h in heads]
    gc = [_pick_lane(gc_all, HEADS + h) for h in heads]
    gc_cols = [jnp.sum(jnp.where(sub8 == h, gc_t, 0.0), axis=0, keepdims=True) for h in heads]
    decay = [jnp.where(causal, jnp.exp(jnp.where(causal, gc[h] - gc_cols[h], 0.0)), 0.0) for h in heads]
    kb = [k[h] * beta[h] for h in heads]
    low = [jnp.where(strict, _mm_nt(kb[h], k[h]) * decay[h], 0.0) for h in heads]
    x = _tri_inv(low)
    eg = [jnp.exp(gc[h]) for h in heads]
    u = [_mm_nn_hi(x[h], v[h] * beta[h]) for h in heads]
    w = [_mm_nn_hi(x[h], kb[h] * eg[h]) for h in heads]
    intra = [_mm_nt(q[h], k[h]) * decay[h] for h in heads]
    g_last = [jnp.sum(jnp.where(last_row, gc[h], 0.0), axis=0, keepdims=True) for h in heads]
    k_dec = [k[h] * jnp.exp(g_last[h] - gc[h]) for h in heads]
    v_new = [u[h] - _mm_nn(w[h], s_list[h]) for h in heads]
    o = [_mm_nn(q[h] * eg[h], s_list[h]) + _mm_nn(intra[h], v_new[h]) for h in heads]
    s_new = [s_list[h] * jnp.exp(g_last[h]) + _mm_tn(k_dec[h], v_new[h]) for h in heads]
    o = [a * lax.rsqrt(jnp.mean(a * a, axis=-1, keepdims=True) + RMS_EPS) * ng_row for a in o]
    o = [o[h] * _silu(gz_list[h]) for h in heads]
    return o, s_new


def _conf_post(cpre, cz, g, b):
    return _silu(_layernorm(cpre, g, b)) * _silu(cz)


def _matmul_nn(a, b, name, tm=512, tn=1024):
    m, k = a.shape
    n = b.shape[1]
    tn = min(tn, n)

    def body(a_ref, b_ref, o_ref):
        o_ref[...] = jnp.dot(a_ref[...], b_ref[...], preferred_element_type=F32)

    return pl.pallas_call(
        body, name=name, grid=(n // tn, m // tm),
        in_specs=[pl.BlockSpec((tm, k), lambda j, i: (i, 0)), pl.BlockSpec((k, tn), lambda j, i: (0, j))],
        out_specs=pl.BlockSpec((tm, tn), lambda j, i: (i, j)),
        out_shape=jax.ShapeDtypeStruct((m, n), F32),
        compiler_params=_cparams(("parallel", "parallel")),
    )(a, b)


def _matmul_tn(a, b, name, tt=512, tn=1024):
    t, k1 = a.shape
    n = b.shape[1]
    tn = min(tn, n)

    def body(a_ref, b_ref, o_ref):
        @pl.when(pl.program_id(1) == 0)
        def _():
            o_ref[...] = jnp.zeros_like(o_ref)

        o_ref[...] += lax.dot_general(a_ref[...], b_ref[...], (_TN, ((), ())), preferred_element_type=F32)

    return pl.pallas_call(
        body, name=name, grid=(n // tn, t // tt),
        in_specs=[pl.BlockSpec((tt, k1), lambda j, i: (i, 0)), pl.BlockSpec((tt, tn), lambda j, i: (i, j))],
        out_specs=pl.BlockSpec((k1, tn), lambda j, i: (0, j)),
        out_shape=jax.ShapeDtypeStruct((k1, n), F32),
        compiler_params=_cparams(("parallel", "arbitrary")),
    )(a, b)


def _matmul_nt_acc(init, a, w, name, tm=512, tk=1024):
    m, n = a.shape
    k1 = w.shape[0]
    tk = min(tk, n)

    def body(i_ref, a_ref, w_ref, o_ref):
        @pl.when(pl.program_id(1) == 0)
        def _():
            o_ref[...] = i_ref[...]

        o_ref[...] += lax.dot_general(a_ref[...], w_ref[...], (_NT, ((), ())), preferred_element_type=F32)

    return pl.pallas_call(
        body, name=name, grid=(m // tm, n // tk),
        in_specs=[pl.BlockSpec((tm, k1), lambda i, j: (i, 0)), pl.BlockSpec((tm, tk), lambda i, j: (i, j)),
                  pl.BlockSpec((k1, tk), lambda i, j: (0, j))],
        out_specs=pl.BlockSpec((tm, k1), lambda i, j: (i, 0)),
        out_shape=jax.ShapeDtypeStruct((m, k1), F32),
        input_output_aliases={0: 0},
        compiler_params=_cparams(("parallel", "arbitrary")),
    )(init, a, w)


def _build_bank(bank_ref, shifts):
    ext = bank_ref[0]
    rows = ext.shape[0]
    for s in shifts:
        if s:
            bank_ref[s] = pltpu.roll(ext, rows - s, axis=0)


def _conv_taps(bank_ref, w_ref, offsets, n_rows, width, emit):
    def piece(rc, carry):
        r0 = pl.multiple_of(rc * 16, 16)
        for cb in range(width // 128):
            lanes = slice(cb * 128, (cb + 1) * 128)
            acc = jnp.zeros((16, 128), F32)
            for k, off in enumerate(offsets):
                m, s = divmod(off, 8)
                acc = acc + bank_ref[s, pl.ds(r0 + 8 * m, 16), lanes] * w_ref[k:k + 1, lanes]
            emit(r0, lanes, acc)
        return carry

    lax.fori_loop(0, n_rows // 16, piece, 0)


def _conv_dw(bank_ref, d_ref, offsets, n_rows, width, emit):
    ms = [divmod(off, 8) for off in offsets]
    n_taps = len(offsets)
    group = max(1, 32 // n_taps)
    blocks = [slice(cb * 128, (cb + 1) * 128) for cb in range(width // 128)]
    for g0 in range(0, len(blocks), group):
        lane_group = blocks[g0:g0 + group]

        def piece(rc, accs, lane_group=lane_group):
            r0 = pl.multiple_of(rc * 8, 8)
            out = []
            for b, lanes in enumerate(lane_group):
                d = d_ref[pl.ds(r0, 8), lanes]
                out += [accs[b * n_taps + k] + d * bank_ref[s, pl.ds(r0 + 8 * m, 8), lanes]
                        for k, (m, s) in enumerate(ms)]
            return tuple(out)

        init = tuple(jnp.zeros((8, 128), F32) for _ in range(len(lane_group) * n_taps))
        accs = lax.fori_loop(0, n_rows // 8, piece, init)
        for b, lanes in enumerate(lane_group):
            for k in range(n_taps):
                emit(k, lanes, jnp.sum(accs[b * n_taps + k], axis=0, keepdims=True))


def _conf_fwd(proj_conf, dw_w, dw_b, ln_g, ln_b, tt=256):
    t = proj_conf.shape[0]
    hb = tt // HALO_CONF
    offsets = [HALO_CONF - (K_CONF - 1) + k for k in range(K_CONF)]

    def body(cv_ref, cg_ref, cz_ref, cvh_ref, cgh_ref, w_ref, b_ref, g_ref, bb_ref, cpre_ref, aout_ref, bank_ref):
        first = pl.program_id(0) == 0
        halo = cvh_ref[...] * _sigmoid(cgh_ref[...])
        bank_ref[0, 0:HALO_CONF, :] = jnp.where(first, 0.0, halo)
        bank_ref[0, HALO_CONF:, :] = cv_ref[...] * _sigmoid(cg_ref[...])
        _build_bank(bank_ref, range(8))

        def emit(r0, lanes, acc):
            cpre_ref[pl.ds(r0, 16), lanes] = acc + b_ref[0:1, lanes]

        _conv_taps(bank_ref, w_ref, offsets, tt, D, emit)
        aout_ref[...] = _conf_post(cpre_ref[...], cz_ref[...], g_ref[...], bb_ref[...]).astype(BF16)

    row = pl.BlockSpec((1, D), lambda i: (0, 0))
    return pl.pallas_call(
        body, name="conf_fwd", grid=(t // tt,),
        in_specs=[pl.BlockSpec((tt, D), lambda i: (i, 0)), pl.BlockSpec((tt, D), lambda i: (i, 1)),
                  pl.BlockSpec((tt, D), lambda i: (i, 2)),
                  pl.BlockSpec((HALO_CONF, D), lambda i: (jnp.maximum(i * hb - 1, 0), 0)),
                  pl.BlockSpec((HALO_CONF, D), lambda i: (jnp.maximum(i * hb - 1, 0), 1)),
                  pl.BlockSpec((32, D), lambda i: (0, 0)), row, row, row],
        out_specs=[pl.BlockSpec((tt, D), lambda i: (i, 0)), pl.BlockSpec((tt, D), lambda i: (i, 0))],
        out_shape=[jax.ShapeDtypeStruct((t, D), F32), jax.ShapeDtypeStruct((t, D), BF16)],
        scratch_shapes=[pltpu.VMEM((8, tt + HALO_CONF, D), F32)],
        compiler_params=_cparams(("parallel",)),
    )(proj_conf, proj_conf, proj_conf, proj_conf, proj_conf, dw_w, dw_b, ln_g, ln_b)


def _conf_bwd_post(cpre, proj_conf, da_out, ln_g, ln_b, tt=256):
    t = cpre.shape[0]

    def body(c_ref, z_ref, da_ref, g_ref, b_ref, dc_ref, dz_ref, dg_ref, db_ref):
        @pl.when(pl.program_id(0) == 0)
        def _():
            dg_ref[...] = jnp.zeros_like(dg_ref)
            db_ref[...] = jnp.zeros_like(db_ref)

        _, vjp = jax.vjp(_conf_post, c_ref[...], z_ref[...], g_ref[...], b_ref[...])
        dc, dz, dg, db = vjp(da_ref[...])
        dc_ref[...] = dc
        dz_ref[...] = dz.astype(BF16)
        dg_ref[0:1, :] += dg
        db_ref[0:1, :] += db

    row = pl.BlockSpec((1, D), lambda i: (0, 0))
    acc = pl.BlockSpec((8, D), lambda i: (0, 0))
    return pl.pallas_call(
        body, name="conf_bwd_post", grid=(t // tt,),
        in_specs=[pl.BlockSpec((tt, D), lambda i: (i, 0)), pl.BlockSpec((tt, D), lambda i: (i, 2)),
                  pl.BlockSpec((tt, D), lambda i: (i, 0)), row, row],
        out_specs=[pl.BlockSpec((tt, D), lambda i: (i, 0)), pl.BlockSpec((tt, D), lambda i: (i, 0)), acc, acc],
        out_shape=[jax.ShapeDtypeStruct((t, D), F32), jax.ShapeDtypeStruct((t, D), BF16),
                   jax.ShapeDtypeStruct((8, D), F32), jax.ShapeDtypeStruct((8, D), F32)],
        compiler_params=_cparams(("arbitrary",)),
    )(cpre, proj_conf, da_out, ln_g, ln_b)


def _conf_bwd_conv(dcpre, proj_conf, dcz, dw_w, tt=128):
    t = dcpre.shape[0]
    n_tiles = t // tt
    hb = tt // HALO_CONF
    n_hb = t // HALO_CONF
    fwd_offsets = [HALO_CONF - (K_CONF - 1) + k for k in range(K_CONF)]
    bwd_offsets = [K_CONF - 1 - k for k in range(K_CONF)]

    def body(d_ref, dn_ref, cv_ref, cg_ref, cvh_ref, cgh_ref, dz_ref, w_ref, dp_ref, dw_ref, db_ref,
             bank_a, bank_d, da_scr):
        i = pl.program_id(0)

        @pl.when(i == 0)
        def _():
            dw_ref[...] = jnp.zeros_like(dw_ref)
            db_ref[...] = jnp.zeros_like(db_ref)

        cv = cv_ref[...]
        sg = _sigmoid(cg_ref[...])
        bank_a[0, 0:HALO_CONF, :] = jnp.where(i == 0, 0.0, cvh_ref[...] * _sigmoid(cgh_ref[...]))
        bank_a[0, HALO_CONF:, :] = cv * sg
        _build_bank(bank_a, range(8))
        bank_d[0, 0:tt, :] = d_ref[...]
        bank_d[0, tt:, :] = jnp.where(i == n_tiles - 1, 0.0, dn_ref[...])
        _build_bank(bank_d, range(8))

        def emit_da(r0, lanes, acc):
            da_scr[pl.ds(r0, 16), lanes] = acc

        _conv_taps(bank_d, w_ref, bwd_offsets, tt, D, emit_da)
        da = da_scr[...]
        dp_ref[:, 0:D] = (da * sg).astype(BF16)
        dp_ref[:, D:2 * D] = (da * cv * sg * (1.0 - sg)).astype(BF16)
        dp_ref[:, 2 * D:3 * D] = dz_ref[...]

        def emit_dw(k, lanes, row):
            dw_ref[k:k + 1, lanes] += row

        _conv_dw(bank_a, d_ref, fwd_offsets, tt, D, emit_dw)
        db_ref[0:1, :] += jnp.sum(d_ref[...], axis=0, keepdims=True)

    prev = lambda i: jnp.maximum(i * hb - 1, 0)
    nxt = lambda i: jnp.minimum((i + 1) * hb, n_hb - 1)
    return pl.pallas_call(
        body, name="conf_bwd_conv", grid=(n_tiles,),
        in_specs=[pl.BlockSpec((tt, D), lambda i: (i, 0)), pl.BlockSpec((HALO_CONF, D), lambda i: (nxt(i), 0)),
                  pl.BlockSpec((tt, D), lambda i: (i, 0)), pl.BlockSpec((tt, D), lambda i: (i, 1)),
                  pl.BlockSpec((HALO_CONF, D), lambda i: (prev(i), 0)),
                  pl.BlockSpec((HALO_CONF, D), lambda i: (prev(i), 1)),
                  pl.BlockSpec((tt, D), lambda i: (i, 0)), pl.BlockSpec((32, D), lambda i: (0, 0))],
        out_specs=[pl.BlockSpec((tt, 3 * D), lambda i: (i, 0)), pl.BlockSpec((32, D), lambda i: (0, 0)),
                   pl.BlockSpec((8, D), lambda i: (0, 0))],
        out_shape=[jax.ShapeDtypeStruct((t, 3 * D), BF16), jax.ShapeDtypeStruct((32, D), F32),
                   jax.ShapeDtypeStruct((8, D), F32)],
        scratch_shapes=[pltpu.VMEM((8, tt + HALO_CONF, D), F32), pltpu.VMEM((8, tt + HALO_CONF, D), F32),
                        pltpu.VMEM((tt, D), F32)],
        compiler_params=_cparams(("arbitrary",)),
    )(dcpre, dcpre, proj_conf, proj_conf, proj_conf, proj_conf, dcz, dw_w)


def _gdn_conv_fwd(proj_qkv, conv_w, tt=256):
    t, width = proj_qkv.shape
    hb = tt // HALO_GDN
    offsets = [HALO_GDN - (K_GDN - 1) + k for k in range(K_GDN)]
    shifts = sorted({off % 8 for off in offsets})

    def body(x_ref, xh_ref, w_ref, o_ref, bank_ref):
        bank_ref[0, 0:HALO_GDN, :] = jnp.where(pl.program_id(1) == 0, 0.0, xh_ref[...])
        bank_ref[0, HALO_GDN:, :] = x_ref[...]
        _build_bank(bank_ref, shifts)

        def emit(r0, lanes, acc):
            o_ref[pl.ds(r0, 16), lanes] = acc

        _conv_taps(bank_ref, w_ref, offsets, tt, D, emit)

    return pl.pallas_call(
        body, name="gdn_conv_fwd", grid=(width // D, t // tt),
        in_specs=[pl.BlockSpec((tt, D), lambda j, i: (i, j)),
                  pl.BlockSpec((HALO_GDN, D), lambda j, i: (jnp.maximum(i * hb - 1, 0), j)),
                  pl.BlockSpec((8, D), lambda j, i: (0, j))],
        out_specs=pl.BlockSpec((tt, D), lambda j, i: (i, j)),
        out_shape=jax.ShapeDtypeStruct((t, width), F32),
        scratch_shapes=[pltpu.VMEM((8, tt + HALO_GDN, D), F32)],
        compiler_params=_cparams(("parallel", "parallel")),
    )(proj_qkv, proj_qkv, conv_w)


def _gdn_conv_bwd(dqkv_c, proj_qkv, conv_w, tt=256):
    t, width = proj_qkv.shape
    n_tiles = t // tt
    hb = tt // HALO_GDN
    n_hb = t // HALO_GDN
    fwd_offsets = [HALO_GDN - (K_GDN - 1) + k for k in range(K_GDN)]
    bwd_offsets = [K_GDN - 1 - k for k in range(K_GDN)]

    def body(d_ref, dn_ref, x_ref, xh_ref, w_ref, dx_ref, dw_ref, bank_x, bank_d):
        i = pl.program_id(1)

        @pl.when(i == 0)
        def _():
            dw_ref[...] = jnp.zeros_like(dw_ref)

        bank_x[0, 0:HALO_GDN, :] = jnp.where(i == 0, 0.0, xh_ref[...])
        bank_x[0, HALO_GDN:, :] = x_ref[...]
        _build_bank(bank_x, sorted({off % 8 for off in fwd_offsets}))
        bank_d[0, 0:tt, :] = d_ref[...]
        bank_d[0, tt:, :] = jnp.where(i == n_tiles - 1, 0.0, dn_ref[...])
        _build_bank(bank_d, sorted({off % 8 for off in bwd_offsets}))

        def emit_dx(r0, lanes, acc):
            dx_ref[pl.ds(r0, 16), lanes] = acc.astype(BF16)

        _conv_taps(bank_d, w_ref, bwd_offsets, tt, D, emit_dx)

        def emit_dw(k, lanes, row):
            dw_ref[k:k + 1, lanes] += row

        _conv_dw(bank_x, d_ref, fwd_offsets, tt, D, emit_dw)

    return pl.pallas_call(
        body, name="gdn_conv_bwd", grid=(width // D, n_tiles),
        in_specs=[pl.BlockSpec((tt, D), lambda j, i: (i, j)),
                  pl.BlockSpec((HALO_GDN, D), lambda j, i: (jnp.minimum((i + 1) * hb, n_hb - 1), j)),
                  pl.BlockSpec((tt, D), lambda j, i: (i, j)),
                  pl.BlockSpec((HALO_GDN, D), lambda j, i: (jnp.maximum(i * hb - 1, 0), j)),
                  pl.BlockSpec((8, D), lambda j, i: (0, j))],
        out_specs=[pl.BlockSpec((tt, D), lambda j, i: (i, j)), pl.BlockSpec((8, D), lambda j, i: (0, j))],
        out_shape=[jax.ShapeDtypeStruct((t, width), BF16), jax.ShapeDtypeStruct((8, width), F32)],
        scratch_shapes=[pltpu.VMEM((8, tt + HALO_GDN, D), F32), pltpu.VMEM((8, tt + HALO_GDN, D), F32)],
        compiler_params=_cparams(("parallel", "arbitrary")),
    )(dqkv_c, dqkv_c, proj_qkv, proj_qkv, conv_w)


def _head_slices(ref):
    return [ref[:, h * HEAD_DIM:(h + 1) * HEAD_DIM] for h in range(HEADS)]


def _gdn_chunk_fwd(qkv_c, proj_ba, proj_gz, a_row, dt_row, ng_row):
    t = qkv_c.shape[0]
    n_chunks = t // CHUNK

    def body(q_ref, k_ref, v_ref, ba_ref, gz_ref, a_ref, dt_ref, ng_ref, o_ref, ssave_ref, s_scr):
        @pl.when(pl.program_id(0) == 0)
        def _():
            s_scr[...] = jnp.zeros_like(s_scr)

        s_list = [s_scr[h] for h in range(HEADS)]
        for h in range(HEADS):
            ssave_ref[0, h] = s_list[h]
        o_list, s_new = _gdn_chunk(_head_slices(q_ref), _head_slices(k_ref), _head_slices(v_ref), ba_ref[...],
                                   _head_slices(gz_ref), s_list, a_ref[...], dt_ref[...], ng_ref[...])
        for h in range(HEADS):
            o_ref[:, h * HEAD_DIM:(h + 1) * HEAD_DIM] = o_list[h].astype(BF16)
            s_scr[h] = s_new[h]

    row = pl.BlockSpec((1, HEAD_DIM), lambda i: (0, 0))
    return pl.pallas_call(
        body, name="gdn_chunk_fwd", grid=(n_chunks,),
        in_specs=[pl.BlockSpec((CHUNK, D), lambda i: (i, 0)), pl.BlockSpec((CHUNK, D), lambda i: (i, 1)),
                  pl.BlockSpec((CHUNK, D), lambda i: (i, 2)), pl.BlockSpec((CHUNK, HEAD_DIM), lambda i: (i, 0)),
                  pl.BlockSpec((CHUNK, D), lambda i: (i, 0)), row, row, row],
        out_specs=[pl.BlockSpec((CHUNK, D), lambda i: (i, 0)),
                   pl.BlockSpec((1, HEADS, HEAD_DIM, HEAD_DIM), lambda i: (i, 0, 0, 0))],
        out_shape=[jax.ShapeDtypeStruct((t, D), BF16),
                   jax.ShapeDtypeStruct((n_chunks, HEADS, HEAD_DIM, HEAD_DIM), F32)],
        scratch_shapes=[pltpu.VMEM((HEADS, HEAD_DIM, HEAD_DIM), F32)],
        compiler_params=_cparams(("arbitrary",)),
    )(qkv_c, qkv_c, qkv_c, proj_ba, proj_gz, a_row, dt_row, ng_row)


def _gdn_chunk_bwd(qkv_c, proj_ba, proj_gz, s_saved, do_gated, a_row, dt_row, ng_row):
    t = qkv_c.shape[0]
    n_chunks = t // CHUNK

    def body(q_ref, k_ref, v_ref, ba_ref, gz_ref, s_ref, do_ref, a_ref, dt_ref, ng_ref,
             dqkv_ref, dba_ref, dgz_ref, da_ref, ddt_ref, dng_ref, ds_scr):
        @pl.when(pl.program_id(0) == 0)
        def _():
            ds_scr[...] = jnp.zeros_like(ds_scr)
            da_ref[...] = jnp.zeros_like(da_ref)
            ddt_ref[...] = jnp.zeros_like(ddt_ref)
            dng_ref[...] = jnp.zeros_like(dng_ref)

        s_list = [s_ref[0, h] for h in range(HEADS)]
        _, vjp = jax.vjp(_gdn_chunk, _head_slices(q_ref), _head_slices(k_ref), _head_slices(v_ref), ba_ref[...],
                         _head_slices(gz_ref), s_list, a_ref[...], dt_ref[...], ng_ref[...])
        do_list = [do_ref[:, h * HEAD_DIM:(h + 1) * HEAD_DIM] for h in range(HEADS)]
        ds_list = [ds_scr[h] for h in range(HEADS)]
        dq, dk, dv, dba, dgz, ds_in, da, ddt, dng = vjp((do_list, ds_list))
        for h in range(HEADS):
            lanes = slice(h * HEAD_DIM, (h + 1) * HEAD_DIM)
            dqkv_ref[:, h * HEAD_DIM:(h + 1) * HEAD_DIM] = dq[h]
            dqkv_ref[:, D + h * HEAD_DIM:D + (h + 1) * HEAD_DIM] = dk[h]
            dqkv_ref[:, 2 * D + h * HEAD_DIM:2 * D + (h + 1) * HEAD_DIM] = dv[h]
            dgz_ref[:, lanes] = dgz[h].astype(BF16)
            ds_scr[h] = ds_in[h]
        dba_ref[...] = dba.astype(BF16)
        da_ref[0:1, :] += da
        ddt_ref[0:1, :] += ddt
        dng_ref[0:1, :] += dng

    rev = lambda i: n_chunks - 1 - i
    row = pl.BlockSpec((1, HEAD_DIM), lambda i: (0, 0))
    acc = pl.BlockSpec((8, HEAD_DIM), lambda i: (0, 0))
    outs = pl.pallas_call(
        body, name="gdn_chunk_bwd", grid=(n_chunks,),
        in_specs=[pl.BlockSpec((CHUNK, D), lambda i: (rev(i), 0)), pl.BlockSpec((CHUNK, D), lambda i: (rev(i), 1)),
                  pl.BlockSpec((CHUNK, D), lambda i: (rev(i), 2)),
                  pl.BlockSpec((CHUNK, HEAD_DIM), lambda i: (rev(i), 0)),
                  pl.BlockSpec((CHUNK, D), lambda i: (rev(i), 0)),
                  pl.BlockSpec((1, HEADS, HEAD_DIM, HEAD_DIM), lambda i: (rev(i), 0, 0, 0)),
                  pl.BlockSpec((CHUNK, D), lambda i: (rev(i), 0)), row, row, row],
        out_specs=[pl.BlockSpec((CHUNK, 3 * D), lambda i: (rev(i), 0)),
                   pl.BlockSpec((CHUNK, HEAD_DIM), lambda i: (rev(i), 0)),
                   pl.BlockSpec((CHUNK, D), lambda i: (rev(i), 0)), acc, acc, acc],
        out_shape=[jax.ShapeDtypeStruct((t, 3 * D), F32)]
        + [jax.ShapeDtypeStruct((t, HEAD_DIM), BF16), jax.ShapeDtypeStruct((t, D), BF16)]
        + [jax.ShapeDtypeStruct((8, HEAD_DIM), F32)] * 3,
        scratch_shapes=[pltpu.VMEM((HEADS, HEAD_DIM, HEAD_DIM), F32)],
        compiler_params=_cparams(("arbitrary",)),
    )(qkv_c, qkv_c, qkv_c, proj_ba, proj_gz, s_saved, do_gated, a_row, dt_row, ng_row)
    return outs


def _merge(a_out, o_gated, proj_gate, x, target, w_conf, w_gdn, w_o, pg, pb, tt=256):
    t = x.shape[0]

    def body(a_ref, o_ref, gt_ref, x_ref, y_ref, wc_ref, wg_ref, wo_ref, pg_ref, pb_ref,
             loss_ref, dpg_ref, dpb_ref, dx_ref, dgt_ref, da_ref, do_ref, h_ref, ds_ref, dyc_ref, dyg_ref):
        @pl.when(pl.program_id(0) == 0)
        def _():
            loss_ref[...] = jnp.zeros_like(loss_ref)
            dpg_ref[...] = jnp.zeros_like(dpg_ref)
            dpb_ref[...] = jnp.zeros_like(dpb_ref)

        wc, wg, wo = wc_ref[...], wg_ref[...], wo_ref[...]
        y_conf = _dot(a_ref[...], wc, _NN)
        y_gdn = _dot(o_ref[...], wg, _NN)
        sc = _sigmoid(gt_ref[:, 0:D])
        sg = _sigmoid(gt_ref[:, D:2 * D])
        h = sc * y_conf + sg * y_gdn
        z = DN_ALPHA * x_ref[...] + _dot(h, wo, _NN)
        mu = jnp.mean(z, axis=-1, keepdims=True)
        zc = z - mu
        rstd = lax.rsqrt(jnp.mean(zc * zc, axis=-1, keepdims=True) + LN_EPS)
        xhat = zc * rstd
        gain = pg_ref[...]
        err = xhat * gain + pb_ref[...] - y_ref[...]
        tok = jnp.mean(err * err, axis=-1, keepdims=True)
        loss_ref[...] += 0.5 * jnp.sum(tok, axis=0, keepdims=True)

        dy = err * (1.0 / D)
        dpg_ref[0:1, :] += jnp.sum(dy * xhat, axis=0, keepdims=True)
        dpb_ref[0:1, :] += jnp.sum(dy, axis=0, keepdims=True)
        dxh = dy * gain
        dz = rstd * (dxh - jnp.mean(dxh, axis=-1, keepdims=True)
                     - xhat * jnp.mean(dxh * xhat, axis=-1, keepdims=True))
        dx_ref[...] = DN_ALPHA * dz
        dh = _dot(dz, wo, _NT)
        dyc = dh * sc
        dyg = dh * sg
        dgt_ref[:, 0:D] = (dh * y_conf * sc * (1.0 - sc)).astype(BF16)
        dgt_ref[:, D:2 * D] = (dh * y_gdn * sg * (1.0 - sg)).astype(BF16)
        da_ref[...] = _dot(dyc, wc, _NT)
        do_ref[...] = _dot(dyg, wg, _NT)
        h_ref[...] = h.astype(BF16)
        ds_ref[...] = dz.astype(BF16)
        dyc_ref[...] = dyc.astype(BF16)
        dyg_ref[...] = dyg.astype(BF16)

    tile = pl.BlockSpec((tt, D), lambda i: (i, 0))
    wide = pl.BlockSpec((tt, 2 * D), lambda i: (i, 0))
    mat = pl.BlockSpec((D, D), lambda i: (0, 0))
    row = pl.BlockSpec((1, D), lambda i: (0, 0))
    acc = pl.BlockSpec((8, D), lambda i: (0, 0))
    act = lambda dt: jax.ShapeDtypeStruct((t, D), dt)
    return pl.pallas_call(
        body, name="merge", grid=(t // tt,),
        in_specs=[tile, tile, wide, tile, tile, mat, mat, mat, row, row],
        out_specs=[pl.BlockSpec((8, 128), lambda i: (0, 0)), acc, acc, tile, wide, tile, tile, tile, tile, tile, tile],
        out_shape=[jax.ShapeDtypeStruct((8, 128), F32), jax.ShapeDtypeStruct((8, D), F32),
                   jax.ShapeDtypeStruct((8, D), F32), act(F32), jax.ShapeDtypeStruct((t, 2 * D), BF16),
                   act(F32), act(F32), act(BF16), act(BF16), act(BF16), act(BF16)],
        compiler_params=_cparams(("arbitrary",)),
    )(a_out, o_gated, proj_gate, x, target, w_conf, w_gdn, w_o, pg, pb)


def _mesh_place():
    x, y, c = lax.axis_index("x"), lax.axis_index("y"), lax.axis_index("c")
    return x, y, c


def _flat(px, py, pc):
    return 4 * px + 2 * py + pc


def _all_gather(shards):
    n = len(shards)

    def body(*refs):
        ins, outs = refs[:n], refs[n:2 * n]
        send_sems, recv_sems, local_sems = refs[2 * n:]
        x, y, c = _mesh_place()
        me, sibling = (x, y, c), (x, y, 1 - c)
        chips = [(1 - x, y), (x, 1 - y), (1 - x, 1 - y)]

        def copy(a, k, block, to, src=None):
            dst = outs[a].at[_flat(*block)]
            return pltpu.make_async_remote_copy(
                src_ref=dst if src is None else src, dst_ref=dst,
                send_sem=send_sems.at[a, k], recv_sem=recv_sems.at[a, k],
                device_id=to, device_id_type=pl.DeviceIdType.MESH)

        mine = [pltpu.make_async_copy(ins[a], outs[a].at[_flat(*me)], local_sems.at[a]) for a in range(n)]
        for cp in mine:
            cp.start()
        first = []
        for a in range(n):
            first.append(copy(a, 0, me, sibling, src=ins[a]))
            first += [copy(a, 1 + j, me, (*chip, c), src=ins[a]) for j, chip in enumerate(chips)]
        for cp in first:
            cp.start()
        passed = []
        for j, chip in enumerate(chips):
            for a in range(n):
                copy(a, 1 + j, (*chip, c), me).wait_recv()
                fwd = copy(a, 4 + j, (*chip, c), sibling)
                fwd.start()
                passed.append(fwd)
        for a in range(n):
            copy(a, 0, sibling, me).wait_recv()
            for j, chip in enumerate(chips):
                copy(a, 4 + j, (*chip, 1 - c), me).wait_recv()
        for cp in first + passed:
            cp.wait_send()
        for cp in mine:
            cp.wait()

    any_spec = pl.BlockSpec(memory_space=pl.ANY)
    return pl.pallas_call(
        body, name="all_gather_weights",
        in_specs=[any_spec] * n, out_specs=[any_spec] * n,
        out_shape=[jax.ShapeDtypeStruct((N_DEV,) + s.shape, s.dtype) for s in shards],
        scratch_shapes=[pltpu.SemaphoreType.DMA((n, 7)), pltpu.SemaphoreType.DMA((n, 7)),
                        pltpu.SemaphoreType.DMA((n,))],
    )(*shards)


def _exchange(block_arrays, small):
    nb = len(block_arrays)

    def body(*refs):
        g_refs, s_ref = refs[:nb], refs[nb]
        land_refs, sall_ref = refs[nb + 1:2 * nb + 1], refs[2 * nb + 1]
        send_sems, recv_sems, local_sems = refs[2 * nb + 2:]
        x, y, c = _mesh_place()
        me = _flat(x, y, c)
        mine = [pltpu.make_async_copy(g_refs[a].at[me], land_refs[a].at[me], local_sems.at[a]) for a in range(nb)]
        mine.append(pltpu.make_async_copy(s_ref, sall_ref.at[me], local_sems.at[nb]))
        for cp in mine:
            cp.start()
        sends, recvs = [], []
        for k in range(7):
            mask = k + 1
            px = 1 - x if mask & 4 else x
            py = 1 - y if mask & 2 else y
            pc = 1 - c if mask & 1 else c
            peer = _flat(px, py, pc)
            for a in range(nb + 1):
                kw = dict(send_sem=send_sems.at[a, k], recv_sem=recv_sems.at[a, k],
                          device_id=(px, py, pc), device_id_type=pl.DeviceIdType.MESH)
                src = g_refs[a].at[peer] if a < nb else s_ref
                land = land_refs[a] if a < nb else sall_ref
                sends.append(pltpu.make_async_remote_copy(src_ref=src, dst_ref=land.at[me], **kw))
                recvs.append(pltpu.make_async_remote_copy(src_ref=src, dst_ref=land.at[peer], **kw))
        for cp in sends:
            cp.start()
        for cp in recvs:
            cp.wait_recv()
        for cp in sends:
            cp.wait_send()
        for cp in mine:
            cp.wait()

    any_spec = pl.BlockSpec(memory_space=pl.ANY)
    outs = pl.pallas_call(
        body, name="exchange_grads",
        in_specs=[any_spec] * (nb + 1), out_specs=[any_spec] * (nb + 1),
        out_shape=[jax.ShapeDtypeStruct(b.shape, b.dtype) for b in block_arrays]
        + [jax.ShapeDtypeStruct((N_DEV,) + small.shape, small.dtype)],
        scratch_shapes=[pltpu.SemaphoreType.DMA((nb + 1, 7)), pltpu.SemaphoreType.DMA((nb + 1, 7)),
                        pltpu.SemaphoreType.DMA((nb + 1,))],
    )(*block_arrays, small)
    return outs[:nb], outs[nb]


def _adamw(parts, w, m, v, name, tile):
    rows, cols = w.shape

    def body(p_ref, w_ref, m_ref, v_ref, g_ref, d_ref, nm_ref, nv_ref):
        g = p_ref[0].astype(F32)
        for s in range(1, N_DEV):
            g = g + p_ref[s].astype(F32)
        nm = ADAM_B1 * m_ref[...] + (1.0 - ADAM_B1) * g
        nv = ADAM_B2 * v_ref[...] + (1.0 - ADAM_B2) * jnp.square(g)
        m_hat = nm / (1.0 - ADAM_B1 ** ADAM_STEP)
        v_hat = nv / (1.0 - ADAM_B2 ** ADAM_STEP)
        g_ref[...] = g
        d_ref[...] = -ADAM_LR * (m_hat / (jnp.sqrt(v_hat) + ADAM_EPS) + ADAM_WD * w_ref[...])
        nm_ref[...] = nm
        nv_ref[...] = nv

    blk = pl.BlockSpec((tile, cols), lambda i: (i, 0))
    out = jax.ShapeDtypeStruct((rows, cols), F32)
    return pl.pallas_call(
        body, name=name, grid=(rows // tile,),
        in_specs=[pl.BlockSpec((N_DEV, tile, cols), lambda i: (0, i, 0)), blk, blk, blk],
        out_specs=[blk, blk, blk, blk], out_shape=[out, out, out, out],
        compiler_params=_cparams(("parallel",)),
    )(parts, w, m, v)


def _rows_of(flat, n_rows):
    flat = flat.reshape(-1)
    return jnp.pad(flat, (0, n_rows * D - flat.shape[0])).reshape(n_rows, D)


def _pack_shards(conf_w_out, gdn_w_out, w_o, conf_dw_w, gdn_conv_w):
    return jnp.concatenate([conf_w_out, gdn_w_out, w_o, _rows_of(conf_dw_w, 16), _rows_of(gdn_conv_w, 16)], axis=0)


def _unpack_shards(p):
    dw = p[ROW_DW:ROW_DW + 4].reshape(-1)[:K_CONF * 128].reshape(K_CONF, 128)
    gc = p[ROW_GC:ROW_GC + 2].reshape(-1)[:K_GDN * 384].reshape(K_GDN, 384)
    return p[ROW_CWO:ROW_CWO + 128], p[ROW_GWO:ROW_GWO + 128], p[ROW_WO:ROW_WO + 128], dw, gc


def _pack_small(dw_b, ln_g, ln_b, pg, pb, ng, a_log, dt_bias, loss=None):
    s = jnp.zeros((SMALL_ROWS, D), F32)
    for r, val in enumerate((dw_b, ln_g, ln_b, pg, pb, ng, a_log, dt_bias)):
        s = s.at[r, :val.shape[0]].set(val)
    if loss is not None:
        s = s.at[8, 0].set(loss)
    return s


def _unpack_small(s):
    return (s[0], s[1], s[2], s[3], s[4], s[5, :HEAD_DIM], s[6, :HEADS], s[7, :HEADS])


def _scatter_blocks(g_cwo, g_gwo, g_wo, g_dw, g_gc):
    dw = g_dw.reshape(K_CONF, N_DEV, 128).transpose(1, 0, 2).reshape(N_DEV, K_CONF * 128)
    dw = jnp.pad(dw, ((0, 0), (0, 16 * D - K_CONF * 128))).reshape(N_DEV, 16, D)
    gc = g_gc.reshape(K_GDN, N_DEV, 384).transpose(1, 0, 2).reshape(N_DEV, K_GDN * 384)
    gc = jnp.pad(gc, ((0, 0), (0, 16 * D - K_GDN * 384))).reshape(N_DEV, 16, D)
    return jnp.concatenate([g_cwo.reshape(N_DEV, 128, D), g_gwo.reshape(N_DEV, 128, D),
                            g_wo.reshape(N_DEV, 128, D), dw, gc], axis=1)


def kernel(x, w_in, conf_dw_w, conf_dw_b, conf_ln_g, conf_ln_b, conf_w_out, gdn_conv_w, gdn_A_log, gdn_dt_bias, gdn_norm_g, gdn_w_out, w_o, post_ln_g, post_ln_b, loss_target, m_w_in, m_conf_dw_w, m_conf_dw_b, m_conf_ln_g, m_conf_ln_b, m_conf_w_out, m_gdn_conv_w, m_gdn_A_log, m_gdn_dt_bias, m_gdn_norm_g, m_gdn_w_out, m_w_o, m_post_ln_g, m_post_ln_b, v_w_in, v_conf_dw_w, v_conf_dw_b, v_conf_ln_g, v_conf_ln_b, v_conf_w_out, v_gdn_conv_w, v_gdn_A_log, v_gdn_dt_bias, v_gdn_norm_g, v_gdn_w_out, v_w_o, v_post_ln_g, v_post_ln_b):
    t = x.shape[1]
    x2 = x.reshape(t, D)
    target = loss_target.reshape(t, D)
    x_bf = x2.astype(BF16)

    w_pack = _pack_shards(conf_w_out, gdn_w_out, w_o, conf_dw_w, gdn_conv_w)
    convw = jnp.concatenate([_rows_of(conf_dw_w, 8), _rows_of(gdn_conv_w, 8)], axis=0)
    all_w_in, all_w, all_convw = _all_gather([w_in.astype(BF16), w_pack.astype(BF16), convw])
    w_full = all_w_in.transpose(1, 0, 2).reshape(D, W_IN_COLS)
    w_conf = w_full[:, 0:3 * D]
    w_qkv = w_full[:, 3 * D:6 * D]
    w_gz = w_full[:, 6 * D:7 * D]
    w_ba = jnp.pad(w_full[:, 7 * D:7 * D + 2 * HEADS], ((0, 0), (0, HEAD_DIM - 2 * HEADS)))
    w_gate = w_full[:, 7 * D + 2 * HEADS:]
    cwo_full = all_w[:, ROW_CWO:ROW_CWO + 128].reshape(D, D)
    gwo_full = all_w[:, ROW_GWO:ROW_GWO + 128].reshape(D, D)
    wo_full = all_w[:, ROW_WO:ROW_WO + 128].reshape(D, D)
    dw_full = all_convw[:, 0:4].reshape(N_DEV, 4 * D)[:, :K_CONF * 128].reshape(N_DEV, K_CONF, 128)
    dw_full = jnp.pad(dw_full.transpose(1, 0, 2).reshape(K_CONF, D), ((0, 32 - K_CONF), (0, 0)))
    gc_full = all_convw[:, 8:10].reshape(N_DEV, 2 * D)[:, :K_GDN * 384].reshape(N_DEV, K_GDN, 384)
    gc_full = jnp.pad(gc_full.transpose(1, 0, 2).reshape(K_GDN, 3 * D), ((0, 8 - K_GDN), (0, 0)))

    row = lambda vec: vec.reshape(1, -1)
    lane_row = lambda vec, at: jnp.zeros((1, HEAD_DIM), F32).at[0, at:at + vec.shape[0]].set(vec)
    a_row = lane_row(gdn_A_log, HEADS)
    dt_row = lane_row(gdn_dt_bias, HEADS)
    ng_row = row(gdn_norm_g)

    proj_conf = _matmul_nn(x_bf, w_conf, "proj_conf")
    proj_qkv = _matmul_nn(x_bf, w_qkv, "proj_qkv")
    proj_gz = _matmul_nn(x_bf, w_gz, "proj_gz")
    proj_gate = _matmul_nn(x_bf, w_gate, "proj_gate")
    proj_ba = _matmul_nn(x_bf, w_ba, "proj_ba")
    cpre, a_out = _conf_fwd(proj_conf, dw_full, row(conf_dw_b), row(conf_ln_g), row(conf_ln_b))
    qkv_c = _gdn_conv_fwd(proj_qkv, gc_full)
    o_gated, s_saved = _gdn_chunk_fwd(qkv_c, proj_ba, proj_gz, a_row, dt_row, ng_row)

    (loss_acc, d_pg, d_pb, dx, d_gate, da_out, do_gated, h_bf, dsub_bf, dyc_bf, dyg_bf) = _merge(
        a_out, o_gated, proj_gate, x2, target, cwo_full, gwo_full, wo_full, row(post_ln_g), row(post_ln_b))
    g_wo = _matmul_tn(h_bf, dsub_bf, "grad_w_o")
    g_cwo = _matmul_tn(a_out, dyc_bf, "grad_conf_w_out")
    g_gwo = _matmul_tn(o_gated, dyg_bf, "grad_gdn_w_out")

    dqkv_c, d_ba, d_gz, d_a_row, d_dt_row, d_ng_row = _gdn_chunk_bwd(
        qkv_c, proj_ba, proj_gz, s_saved, do_gated, a_row, dt_row, ng_row)
    d_qkv, g_gc = _gdn_conv_bwd(dqkv_c, proj_qkv, gc_full)

    dcpre, dcz, d_ln_g, d_ln_b = _conf_bwd_post(cpre, proj_conf, da_out, row(conf_ln_g), row(conf_ln_b))
    d_conf, g_dw, g_dwb = _conf_bwd_conv(dcpre, proj_conf, dcz, dw_full)

    segments = [(d_conf, w_conf, "conf"), (d_qkv, w_qkv, "qkv"), (d_gz, w_gz, "gz"),
                (d_ba, w_ba, "ba"), (d_gate, w_gate, "gate")]
    g_cols = {}
    for d_seg, w_seg, tag in segments:
        g_cols[tag] = _matmul_tn(x_bf, d_seg, "grad_w_in_" + tag)
        dx = _matmul_nt_acc(dx, d_seg, w_seg, "grad_x_" + tag)
    g_w_in = jnp.concatenate([g_cols["conf"], g_cols["qkv"], g_cols["gz"], g_cols["ba"][:, :2 * HEADS],
                              g_cols["gate"]], axis=1)

    w_in_blocks = g_w_in.reshape(D, N_DEV, W_IN_SHARD).transpose(1, 0, 2).astype(BF16)
    blocks = _scatter_blocks(g_cwo, g_gwo, g_wo, g_dw[:K_CONF], g_gc[:K_GDN]).astype(BF16)
    small = _pack_small(g_dwb[0], d_ln_g[0], d_ln_b[0], d_pg[0], d_pb[0], d_ng_row[0],
                        d_a_row[0, HEADS:2 * HEADS], d_dt_row[0, HEADS:2 * HEADS], loss_acc[0, 0])
    (landed_w_in, landed), small_all = _exchange([w_in_blocks, blocks], small)

    m_pack = _pack_shards(m_conf_w_out, m_gdn_w_out, m_w_o, m_conf_dw_w, m_gdn_conv_w)
    v_pack = _pack_shards(v_conf_w_out, v_gdn_w_out, v_w_o, v_conf_dw_w, v_gdn_conv_w)
    big_w_in = _adamw(landed_w_in, w_in, m_w_in, v_w_in, "adamw_w_in", W_IN_TILE)
    big = _adamw(landed, w_pack, m_pack, v_pack, "adamw_shards", PACK_TILE)
    ws = _pack_small(conf_dw_b, conf_ln_g, conf_ln_b, post_ln_g, post_ln_b, gdn_norm_g, gdn_A_log, gdn_dt_bias)
    ms = _pack_small(m_conf_dw_b, m_conf_ln_g, m_conf_ln_b, m_post_ln_g, m_post_ln_b, m_gdn_norm_g, m_gdn_A_log,
                     m_gdn_dt_bias)
    vs = _pack_small(v_conf_dw_b, v_conf_ln_g, v_conf_ln_b, v_post_ln_g, v_post_ln_b, v_gdn_norm_g, v_gdn_A_log,
                     v_gdn_dt_bias)
    sml = _adamw(small_all, ws, ms, vs, "adamw_replicated", SMALL_ROWS)

    loss = sml[0][8, 0]
    outs = []
    for b_w_in, big_k, sml_k in zip(big_w_in, big, sml):
        b_cwo, b_gwo, b_wo, b_dw, b_gc = _unpack_shards(big_k)
        s_dwb, s_lng, s_lnb, s_pg, s_pb, s_ng, s_a, s_dt = _unpack_small(sml_k)
        outs.append([b_w_in, b_dw, s_dwb, s_lng, s_lnb, b_cwo, b_gc, s_a, s_dt, s_ng, b_gwo, b_wo, s_pg, s_pb])
    return (loss, dx.reshape(1, t, D), *outs[0], *outs[1], *outs[2], *outs[3])
```

```python
import functools

import jax
import jax.numpy as jnp
from jax import lax
from jax.experimental import pallas as pl
from jax.experimental.pallas import tpu as pltpu

F32 = jnp.float32
BF16 = jnp.bfloat16

N_DEV = 8
D = 1024
HEADS = 8
HEAD_DIM = 128
CHUNK = 64
CHUNKS_PER_STEP = 2
K_CONF = 31
K_GDN = 4
HALO_CONF = 32
HALO_GDN = 8
LN_EPS = 1e-5
RMS_EPS = 1e-6
L2_EPS = 1e-6
DN_ALPHA = 2.0 ** 0.25
ADAM_LR = 0.001
ADAM_B1 = 0.9
ADAM_B2 = 0.999
ADAM_EPS = 1e-08
ADAM_WD = 0.01
ADAM_STEP = 10

W_IN_COLS = 9232
W_IN_SHARD = W_IN_COLS // N_DEV
ROW_CWO = 0
ROW_GWO = ROW_CWO + 128
ROW_WO = ROW_GWO + 128
ROW_DW = ROW_WO + 128
ROW_GC = ROW_DW + 16
PACK_ROWS = ROW_GC + 16
PACK_TILE = PACK_ROWS // 2
W_IN_TILE = 128
SMALL_ROWS = 16
CONVW_ROWS = 16

VMEM_LIMIT = 56 * 1024 * 1024

_NN = ((1,), (0,))
_NT = ((1,), (1,))
_TN = ((0,), (0,))


def _cparams(sem=None):
    return pltpu.CompilerParams(dimension_semantics=sem, vmem_limit_bytes=VMEM_LIMIT)


def _dot(a, b, dims, hi=False):
    dn = (dims, ((), ()))
    a_hi = a.astype(BF16)
    b_hi = b.astype(BF16)
    if not hi:
        return lax.dot_general(a_hi, b_hi, dn, preferred_element_type=F32)
    a_lo = (a - a_hi.astype(F32)).astype(BF16)
    b_lo = (b - b_hi.astype(F32)).astype(BF16)
    d = lambda p, q: lax.dot_general(p, q, dn, preferred_element_type=F32)
    return d(a_hi, b_hi) + (d(a_hi, b_lo) + d(a_lo, b_hi))


def _make_mm(kind, hi):
    dims = {"nn": _NN, "nt": _NT, "tn": _TN}[kind]

    @jax.custom_vjp
    def mm(a, b):
        return _dot(a, b, dims, hi)

    def fwd(a, b):
        return _dot(a, b, dims, hi), (a, b)

    def bwd(res, g):
        a, b = res
        if kind == "nn":
            return _dot(g, b, _NT, hi), _dot(a, g, _TN, hi)
        if kind == "nt":
            return _dot(g, b, _NN, hi), _dot(g, a, _TN, hi)
        return _dot(b, g, _NT, hi), _dot(a, g, _NN, hi)

    mm.defvjp(fwd, bwd)
    return mm


_mm_nn = _make_mm("nn", False)
_mm_nt = _make_mm("nt", False)
_mm_tn = _make_mm("tn", False)
_mm_nn_hi = _make_mm("nn", True)
_mm_tn_hi = _make_mm("tn", True)


def _tri_inv_impl(lows):
    c = lows[0].shape[0]
    eye = (lax.broadcasted_iota(jnp.int32, (c, c), 0) == lax.broadcasted_iota(jnp.int32, (c, c), 1)).astype(F32)
    ms = [-low for low in lows]
    ps = [eye + m for m in ms]
    steps = max(c.bit_length() - 2, 0)
    for _ in range(steps):
        ms = [_dot(m, m, _NN, True) for m in ms]
        ps = [p + _dot(p, m, _NN, True) for p, m in zip(ps, ms)]
    return ps


@jax.custom_vjp
def _tri_inv(lows):
    return _tri_inv_impl(lows)


def _tri_inv_fwd(lows):
    xs = _tri_inv_impl(lows)
    return xs, xs


def _tri_inv_bwd(xs, dxs):
    ts = [_dot(x, dx, _TN, True) for x, dx in zip(xs, dxs)]
    return ([-_dot(t, x, _NT, True) for t, x in zip(ts, xs)],)


_tri_inv.defvjp(_tri_inv_fwd, _tri_inv_bwd)


def _sigmoid(x):
    return jax.nn.sigmoid(x)


def _silu(x):
    return x * jax.nn.sigmoid(x)


def _softplus(x):
    u = jnp.exp(-jnp.abs(x))
    log1p_u = jnp.where(u < 1e-3, u * (1.0 - u * (0.5 - u * (1.0 / 3.0))), jnp.log(1.0 + u))
    return jnp.maximum(x, 0.0) + log1p_u


def _layernorm(x, g, b):
    mu = jnp.mean(x, axis=-1, keepdims=True)
    xc = x - mu
    var = jnp.mean(xc * xc, axis=-1, keepdims=True)
    return xc * lax.rsqrt(var + LN_EPS) * g + b


def _pick_lane(x, lane):
    idx = lax.broadcasted_iota(jnp.int32, x.shape, 1)
    return jnp.sum(jnp.where(idx == lane, x, 0.0), axis=1, keepdims=True)


def _gdn_chunk(q_list, k_list, v_list, ba_list, gz_list, s_list, a_row, dt_row, ng_row):
    c = ba_list[0].shape[0]
    n_chunks = len(ba_list)
    pairs = [(ci, h) for ci in range(n_chunks) for h in range(HEADS)]
    every = range(len(pairs))
    rows = lax.broadcasted_iota(jnp.int32, (c, c), 0)
    cols = lax.broadcasted_iota(jnp.int32, (c, c), 1)
    causal = rows >= cols
    strict = rows > cols
    tril = causal.astype(F32)
    triu = (rows <= cols).astype(F32)
    last_row = lax.broadcasted_iota(jnp.int32, (c, 1), 0) == c - 1
    sub8 = lax.broadcasted_iota(jnp.int32, (HEADS, c), 0)

    beta_all = [_sigmoid(ba) for ba in ba_list]
    g_all = [-jnp.exp(a_row) * _softplus(ba + dt_row) for ba in ba_list]
    gc_all = [_mm_nn_hi(tril, g) for g in g_all]
    gc_t = [_mm_tn_hi(g, triu)[HEADS:2 * HEADS, :] for g in g_all]

    q = [_silu(a) for a in q_list]
    k = [_silu(a) for a in k_list]
    v = [_silu(a) for a in v_list]
    q = [a * lax.rsqrt(jnp.sum(a * a, axis=-1, keepdims=True) + L2_EPS) * (HEAD_DIM ** -0.5) for a in q]
    k = [a * lax.rsqrt(jnp.sum(a * a, axis=-1, keepdims=True) + L2_EPS) for a in k]
    beta = [_pick_lane(beta_all[ci], h) for ci, h in pairs]
    gc = [_pick_lane(gc_all[ci], HEADS + h) for ci, h in pairs]
    gc_cols = [jnp.sum(jnp.where(sub8 == h, gc_t[ci], 0.0), axis=0, keepdims=True) for ci, h in pairs]
    decay = [jnp.where(causal, jnp.exp(jnp.where(causal, gc[p] - gc_cols[p], 0.0)), 0.0) for p in every]
    kb = [k[p] * beta[p] for p in every]
    low = [jnp.where(strict, _mm_nt(kb[p], k[p]) * decay[p], 0.0) for p in every]
    x = _tri_inv(low)
    eg = [jnp.exp(gc[p]) for p in every]
    u = [_mm_nn_hi(x[p], v[p] * beta[p]) for p in every]
    w = [_mm_nn_hi(x[p], kb[p] * eg[p]) for p in every]
    intra = [_mm_nt(q[p], k[p]) * decay[p] for p in every]
    q_dec = [q[p] * eg[p] for p in every]
    g_last = [jnp.sum(jnp.where(last_row, gc[p], 0.0), axis=0, keepdims=True) for p in every]
    k_dec = [k[p] * jnp.exp(g_last[p] - gc[p]) for p in every]
    s_dec = [jnp.exp(g_last[p]) for p in every]

    o = []
    state = list(s_list)
    for ci in range(n_chunks):
        at = [ci * HEADS + h for h in range(HEADS)]
        v_new = [u[p] - _mm_nn(w[p], state[h]) for h, p in enumerate(at)]
        o += [_mm_nn(q_dec[p], state[h]) + _mm_nn(intra[p], v_new[h]) for h, p in enumerate(at)]
        state = [state[h] * s_dec[p] + _mm_tn(k_dec[p], v_new[h]) for h, p in enumerate(at)]
    o = [a * lax.rsqrt(jnp.mean(a * a, axis=-1, keepdims=True) + RMS_EPS) * ng_row for a in o]
    o = [o[p] * _silu(gz_list[p]) for p in every]
    return o, state


def _conf_post(cpre, cz, g, b):
    return _silu(_layernorm(cpre, g, b)) * _silu(cz)


def _matmul_nn(a, b, name, tm=512, tn=1024):
    m, k = a.shape
    n = b.shape[1]
    tn = min(tn, n)

    def body(a_ref, b_ref, o_ref):
        o_ref[...] = jnp.dot(a_ref[...], b_ref[...], preferred_element_type=F32)

    return pl.pallas_call(
        body, name=name, grid=(n // tn, m // tm),
        in_specs=[pl.BlockSpec((tm, k), lambda j, i: (i, 0)), pl.BlockSpec((k, tn), lambda j, i: (0, j))],
        out_specs=pl.BlockSpec((tm, tn), lambda j, i: (i, j)),
        out_shape=jax.ShapeDtypeStruct((m, n), F32),
        compiler_params=_cparams(("parallel", "parallel")),
    )(a, b)


def _matmul_tn(a, b, name, tt=512, tn=1024):
    t, k1 = a.shape
    n = b.shape[1]
    tn = min(tn, n)

    def body(a_ref, b_ref, o_ref):
        @pl.when(pl.program_id(1) == 0)
        def _():
            o_ref[...] = jnp.zeros_like(o_ref)

        o_ref[...] += lax.dot_general(a_ref[...], b_ref[...], (_TN, ((), ())), preferred_element_type=F32)

    return pl.pallas_call(
        body, name=name, grid=(n // tn, t // tt),
        in_specs=[pl.BlockSpec((tt, k1), lambda j, i: (i, 0)), pl.BlockSpec((tt, tn), lambda j, i: (i, j))],
        out_specs=pl.BlockSpec((k1, tn), lambda j, i: (0, j)),
        out_shape=jax.ShapeDtypeStruct((k1, n), F32),
        compiler_params=_cparams(("parallel", "arbitrary")),
    )(a, b)


def _build_bank(bank_ref, shifts):
    ext = bank_ref[0]
    rows = ext.shape[0]
    for s in shifts:
        if s:
            bank_ref[s] = pltpu.roll(ext, rows - s, axis=0)


def _conv_taps(bank_ref, w_ref, offsets, n_rows, width, emit):
    def piece(rc, carry):
        r0 = pl.multiple_of(rc * 16, 16)
        for cb in range(width // 128):
            lanes = slice(cb * 128, (cb + 1) * 128)
            acc = jnp.zeros((16, 128), F32)
            for k, off in enumerate(offsets):
                m, s = divmod(off, 8)
                acc = acc + bank_ref[s, pl.ds(r0 + 8 * m, 16), lanes] * w_ref[k:k + 1, lanes]
            emit(r0, lanes, acc)
        return carry

    lax.fori_loop(0, n_rows // 16, piece, 0)


def _conv_dw(bank_ref, d_ref, offsets, n_rows, width, emit):
    ms = [divmod(off, 8) for off in offsets]
    n_taps = len(offsets)
    group = max(1, 32 // n_taps)
    blocks = [slice(cb * 128, (cb + 1) * 128) for cb in range(width // 128)]
    for g0 in range(0, len(blocks), group):
        lane_group = blocks[g0:g0 + group]

        def piece(rc, accs, lane_group=lane_group):
            r0 = pl.multiple_of(rc * 8, 8)
            out = []
            for b, lanes in enumerate(lane_group):
                d = d_ref[pl.ds(r0, 8), lanes]
                out += [accs[b * n_taps + k] + d * bank_ref[s, pl.ds(r0 + 8 * m, 8), lanes]
                        for k, (m, s) in enumerate(ms)]
            return tuple(out)

        init = tuple(jnp.zeros((8, 128), F32) for _ in range(len(lane_group) * n_taps))
        accs = lax.fori_loop(0, n_rows // 8, piece, init)
        for b, lanes in enumerate(lane_group):
            for k in range(n_taps):
                emit(k, lanes, jnp.sum(accs[b * n_taps + k], axis=0, keepdims=True))


def _conf_fwd(proj_conf, dw_w, dw_b, ln_g, ln_b, tt=256):
    t = proj_conf.shape[0]
    hb = tt // HALO_CONF
    offsets = [HALO_CONF - (K_CONF - 1) + k for k in range(K_CONF)]

    def body(cv_ref, cg_ref, cz_ref, cvh_ref, cgh_ref, w_ref, b_ref, g_ref, bb_ref, cpre_ref, aout_ref, bank_ref):
        first = pl.program_id(0) == 0
        halo = cvh_ref[...] * _sigmoid(cgh_ref[...])
        bank_ref[0, 0:HALO_CONF, :] = jnp.where(first, 0.0, halo)
        bank_ref[0, HALO_CONF:, :] = cv_ref[...] * _sigmoid(cg_ref[...])
        _build_bank(bank_ref, range(8))

        def emit(r0, lanes, acc):
            cpre_ref[pl.ds(r0, 16), lanes] = acc + b_ref[0:1, lanes]

        _conv_taps(bank_ref, w_ref, offsets, tt, D, emit)
        aout_ref[...] = _conf_post(cpre_ref[...], cz_ref[...], g_ref[...], bb_ref[...]).astype(BF16)

    row = pl.BlockSpec((1, D), lambda i: (0, 0))
    return pl.pallas_call(
        body, name="conf_fwd", grid=(t // tt,),
        in_specs=[pl.BlockSpec((tt, D), lambda i: (i, 0)), pl.BlockSpec((tt, D), lambda i: (i, 1)),
                  pl.BlockSpec((tt, D), lambda i: (i, 2)),
                  pl.BlockSpec((HALO_CONF, D), lambda i: (jnp.maximum(i * hb - 1, 0), 0)),
                  pl.BlockSpec((HALO_CONF, D), lambda i: (jnp.maximum(i * hb - 1, 0), 1)),
                  pl.BlockSpec((32, D), lambda i: (0, 0)), row, row, row],
        out_specs=[pl.BlockSpec((tt, D), lambda i: (i, 0)), pl.BlockSpec((tt, D), lambda i: (i, 0))],
        out_shape=[jax.ShapeDtypeStruct((t, D), F32), jax.ShapeDtypeStruct((t, D), BF16)],
        scratch_shapes=[pltpu.VMEM((8, tt + HALO_CONF, D), F32)],
        compiler_params=_cparams(("parallel",)),
    )(proj_conf, proj_conf, proj_conf, proj_conf, proj_conf, dw_w, dw_b, ln_g, ln_b)


def _conf_bwd_post(cpre, proj_conf, da_out, ln_g, ln_b, tt=256):
    t = cpre.shape[0]

    def body(c_ref, z_ref, da_ref, g_ref, b_ref, dc_ref, dz_ref, dg_ref, db_ref):
        @pl.when(pl.program_id(0) == 0)
        def _():
            dg_ref[...] = jnp.zeros_like(dg_ref)
            db_ref[...] = jnp.zeros_like(db_ref)

        _, vjp = jax.vjp(_conf_post, c_ref[...], z_ref[...], g_ref[...], b_ref[...])
        dc, dz, dg, db = vjp(da_ref[...])
        dc_ref[...] = dc
        dz_ref[...] = dz.astype(BF16)
        dg_ref[0:1, :] += dg
        db_ref[0:1, :] += db

    row = pl.BlockSpec((1, D), lambda i: (0, 0))
    acc = pl.BlockSpec((8, D), lambda i: (0, 0))
    return pl.pallas_call(
        body, name="conf_bwd_post", grid=(t // tt,),
        in_specs=[pl.BlockSpec((tt, D), lambda i: (i, 0)), pl.BlockSpec((tt, D), lambda i: (i, 2)),
                  pl.BlockSpec((tt, D), lambda i: (i, 0)), row, row],
        out_specs=[pl.BlockSpec((tt, D), lambda i: (i, 0)), pl.BlockSpec((tt, D), lambda i: (i, 0)), acc, acc],
        out_shape=[jax.ShapeDtypeStruct((t, D), F32), jax.ShapeDtypeStruct((t, D), BF16),
                   jax.ShapeDtypeStruct((8, D), F32), jax.ShapeDtypeStruct((8, D), F32)],
        compiler_params=_cparams(("arbitrary",)),
    )(cpre, proj_conf, da_out, ln_g, ln_b)


def _conf_bwd_conv(dcpre, proj_conf, dcz, dw_w, tt=256):
    t = dcpre.shape[0]
    n_tiles = t // tt
    hb = tt // HALO_CONF
    n_hb = t // HALO_CONF
    offsets = [K_CONF - 1 - k for k in range(K_CONF)]

    def body(d_ref, dn_ref, cv_ref, cg_ref, dz_ref, w_ref, dp_ref, dw_ref, db_ref, bank_d, a_scr, da_scr):
        i = pl.program_id(0)

        @pl.when(i == 0)
        def _():
            dw_ref[...] = jnp.zeros_like(dw_ref)
            db_ref[...] = jnp.zeros_like(db_ref)

        cv = cv_ref[...]
        sg = _sigmoid(cg_ref[...])
        a_scr[...] = cv * sg
        bank_d[0, 0:tt, :] = d_ref[...]
        bank_d[0, tt:, :] = jnp.where(i == n_tiles - 1, 0.0, dn_ref[...])
        _build_bank(bank_d, range(8))

        def emit_da(r0, lanes, acc):
            da_scr[pl.ds(r0, 16), lanes] = acc

        _conv_taps(bank_d, w_ref, offsets, tt, D, emit_da)
        da = da_scr[...]
        dp_ref[:, 0:D] = (da * sg).astype(BF16)
        dp_ref[:, D:2 * D] = (da * cv * sg * (1.0 - sg)).astype(BF16)
        dp_ref[:, 2 * D:3 * D] = dz_ref[...]

        def emit_dw(k, lanes, row):
            dw_ref[k:k + 1, lanes] += row

        _conv_dw(bank_d, a_scr, offsets, tt, D, emit_dw)
        db_ref[0:1, :] += jnp.sum(d_ref[...], axis=0, keepdims=True)

    nxt = lambda i: jnp.minimum((i + 1) * hb, n_hb - 1)
    return pl.pallas_call(
        body, name="conf_bwd_conv", grid=(n_tiles,),
        in_specs=[pl.BlockSpec((tt, D), lambda i: (i, 0)), pl.BlockSpec((HALO_CONF, D), lambda i: (nxt(i), 0)),
                  pl.BlockSpec((tt, D), lambda i: (i, 0)), pl.BlockSpec((tt, D), lambda i: (i, 1)),
                  pl.BlockSpec((tt, D), lambda i: (i, 0)), pl.BlockSpec((32, D), lambda i: (0, 0))],
        out_specs=[pl.BlockSpec((tt, 3 * D), lambda i: (i, 0)), pl.BlockSpec((32, D), lambda i: (0, 0)),
                   pl.BlockSpec((8, D), lambda i: (0, 0))],
        out_shape=[jax.ShapeDtypeStruct((t, 3 * D), BF16), jax.ShapeDtypeStruct((32, D), F32),
                   jax.ShapeDtypeStruct((8, D), F32)],
        scratch_shapes=[pltpu.VMEM((8, tt + HALO_CONF, D), F32), pltpu.VMEM((tt, D), F32), pltpu.VMEM((tt, D), F32)],
        compiler_params=_cparams(("arbitrary",)),
    )(dcpre, dcpre, proj_conf, proj_conf, dcz, dw_w)


def _gdn_conv_fwd(proj_qkv, conv_w, tt=256):
    t, width = proj_qkv.shape
    hb = tt // HALO_GDN
    offsets = [HALO_GDN - (K_GDN - 1) + k for k in range(K_GDN)]
    shifts = sorted({off % 8 for off in offsets})

    def body(x_ref, xh_ref, w_ref, o_ref, bank_ref):
        bank_ref[0, 0:HALO_GDN, :] = jnp.where(pl.program_id(1) == 0, 0.0, xh_ref[...])
        bank_ref[0, HALO_GDN:, :] = x_ref[...]
        _build_bank(bank_ref, shifts)

        def emit(r0, lanes, acc):
            o_ref[pl.ds(r0, 16), lanes] = acc

        _conv_taps(bank_ref, w_ref, offsets, tt, D, emit)

    return pl.pallas_call(
        body, name="gdn_conv_fwd", grid=(width // D, t // tt),
        in_specs=[pl.BlockSpec((tt, D), lambda j, i: (i, j)),
                  pl.BlockSpec((HALO_GDN, D), lambda j, i: (jnp.maximum(i * hb - 1, 0), j)),
                  pl.BlockSpec((8, D), lambda j, i: (0, j))],
        out_specs=pl.BlockSpec((tt, D), lambda j, i: (i, j)),
        out_shape=jax.ShapeDtypeStruct((t, width), F32),
        scratch_shapes=[pltpu.VMEM((8, tt + HALO_GDN, D), F32)],
        compiler_params=_cparams(("parallel", "parallel")),
    )(proj_qkv, proj_qkv, conv_w)


def _gdn_conv_bwd(dqkv_c, proj_qkv, conv_w, tt=256):
    t, width = proj_qkv.shape
    n_tiles = t // tt
    hb = tt // HALO_GDN
    n_hb = t // HALO_GDN
    offsets = [K_GDN - 1 - k for k in range(K_GDN)]

    def body(d_ref, dn_ref, x_ref, w_ref, dx_ref, dw_ref, bank_d):
        i = pl.program_id(1)

        @pl.when(i == 0)
        def _():
            dw_ref[...] = jnp.zeros_like(dw_ref)

        bank_d[0, 0:tt, :] = d_ref[...]
        bank_d[0, tt:, :] = jnp.where(i == n_tiles - 1, 0.0, dn_ref[...])
        _build_bank(bank_d, sorted({off % 8 for off in offsets}))

        def emit_dx(r0, lanes, acc):
            dx_ref[pl.ds(r0, 16), lanes] = acc.astype(BF16)

        _conv_taps(bank_d, w_ref, offsets, tt, D, emit_dx)

        def emit_dw(k, lanes, row):
            dw_ref[k:k + 1, lanes] += row

        _conv_dw(bank_d, x_ref, offsets, tt, D, emit_dw)

    return pl.pallas_call(
        body, name="gdn_conv_bwd", grid=(width // D, n_tiles),
        in_specs=[pl.BlockSpec((tt, D), lambda j, i: (i, j)),
                  pl.BlockSpec((HALO_GDN, D), lambda j, i: (jnp.minimum((i + 1) * hb, n_hb - 1), j)),
                  pl.BlockSpec((tt, D), lambda j, i: (i, j)),
                  pl.BlockSpec((8, D), lambda j, i: (0, j))],
        out_specs=[pl.BlockSpec((tt, D), lambda j, i: (i, j)), pl.BlockSpec((8, D), lambda j, i: (0, j))],
        out_shape=[jax.ShapeDtypeStruct((t, width), BF16), jax.ShapeDtypeStruct((8, width), F32)],
        scratch_shapes=[pltpu.VMEM((8, tt + HALO_GDN, D), F32)],
        compiler_params=_cparams(("parallel", "arbitrary")),
    )(dqkv_c, dqkv_c, proj_qkv, conv_w)


def _pair_rows(ci):
    return slice(ci * CHUNK, (ci + 1) * CHUNK)


def _pair_lanes(h, base=0):
    return slice(base + h * HEAD_DIM, base + (h + 1) * HEAD_DIM)


def _pair_slices(ref):
    return [ref[_pair_rows(ci), _pair_lanes(h)] for ci in range(CHUNKS_PER_STEP) for h in range(HEADS)]


def _gdn_chunk_fwd(qkv_c, proj_ba, proj_gz, a_row, dt_row, ng_row):
    t = qkv_c.shape[0]
    rows = CHUNKS_PER_STEP * CHUNK
    n_steps = t // rows

    def body(q_ref, k_ref, v_ref, ba_ref, gz_ref, a_ref, dt_ref, ng_ref, o_ref, ssave_ref, s_scr):
        @pl.when(pl.program_id(0) == 0)
        def _():
            s_scr[...] = jnp.zeros_like(s_scr)

        s_list = [s_scr[h] for h in range(HEADS)]
        for h in range(HEADS):
            ssave_ref[0, h] = s_list[h]
        ba_list = [ba_ref[_pair_rows(ci), :] for ci in range(CHUNKS_PER_STEP)]
        o_list, s_new = _gdn_chunk(_pair_slices(q_ref), _pair_slices(k_ref), _pair_slices(v_ref), ba_list,
                                   _pair_slices(gz_ref), s_list, a_ref[...], dt_ref[...], ng_ref[...])
        for ci in range(CHUNKS_PER_STEP):
            for h in range(HEADS):
                o_ref[_pair_rows(ci), _pair_lanes(h)] = o_list[ci * HEADS + h].astype(BF16)
        for h in range(HEADS):
            s_scr[h] = s_new[h]

    row = pl.BlockSpec((1, HEAD_DIM), lambda i: (0, 0))
    return pl.pallas_call(
        body, name="gdn_chunk_fwd", grid=(n_steps,),
        in_specs=[pl.BlockSpec((rows, D), lambda i: (i, 0)), pl.BlockSpec((rows, D), lambda i: (i, 1)),
                  pl.BlockSpec((rows, D), lambda i: (i, 2)), pl.BlockSpec((rows, HEAD_DIM), lambda i: (i, 0)),
                  pl.BlockSpec((rows, D), lambda i: (i, 0)), row, row, row],
        out_specs=[pl.BlockSpec((rows, D), lambda i: (i, 0)),
                   pl.BlockSpec((1, HEADS, HEAD_DIM, HEAD_DIM), lambda i: (i, 0, 0, 0))],
        out_shape=[jax.ShapeDtypeStruct((t, D), BF16),
                   jax.ShapeDtypeStruct((n_steps, HEADS, HEAD_DIM, HEAD_DIM), F32)],
        scratch_shapes=[pltpu.VMEM((HEADS, HEAD_DIM, HEAD_DIM), F32)],
        compiler_params=_cparams(("arbitrary",)),
    )(qkv_c, qkv_c, qkv_c, proj_ba, proj_gz, a_row, dt_row, ng_row)


def _gdn_chunk_bwd(qkv_c, proj_ba, proj_gz, s_saved, do_gated, a_row, dt_row, ng_row):
    t = qkv_c.shape[0]
    rows = CHUNKS_PER_STEP * CHUNK
    n_steps = t // rows

    def body(q_ref, k_ref, v_ref, ba_ref, gz_ref, s_ref, do_ref, a_ref, dt_ref, ng_ref,
             dqkv_ref, dba_ref, dgz_ref, da_ref, ddt_ref, dng_ref, ds_scr):
        @pl.when(pl.program_id(0) == 0)
        def _():
            ds_scr[...] = jnp.zeros_like(ds_scr)
            da_ref[...] = jnp.zeros_like(da_ref)
            ddt_ref[...] = jnp.zeros_like(ddt_ref)
            dng_ref[...] = jnp.zeros_like(dng_ref)

        s_list = [s_ref[0, h] for h in range(HEADS)]
        ba_list = [ba_ref[_pair_rows(ci), :] for ci in range(CHUNKS_PER_STEP)]
        _, vjp = jax.vjp(_gdn_chunk, _pair_slices(q_ref), _pair_slices(k_ref), _pair_slices(v_ref), ba_list,
                         _pair_slices(gz_ref), s_list, a_ref[...], dt_ref[...], ng_ref[...])
        ds_list = [ds_scr[h] for h in range(HEADS)]
        dq, dk, dv, dba, dgz, ds_in, da, ddt, dng = vjp((_pair_slices(do_ref), ds_list))
        for ci in range(CHUNKS_PER_STEP):
            for h in range(HEADS):
                p = ci * HEADS + h
                dqkv_ref[_pair_rows(ci), _pair_lanes(h)] = dq[p]
                dqkv_ref[_pair_rows(ci), _pair_lanes(h, D)] = dk[p]
                dqkv_ref[_pair_rows(ci), _pair_lanes(h, 2 * D)] = dv[p]
                dgz_ref[_pair_rows(ci), _pair_lanes(h)] = dgz[p].astype(BF16)
            dba_ref[_pair_rows(ci), :] = dba[ci].astype(BF16)
        for h in range(HEADS):
            ds_scr[h] = ds_in[h]
        da_ref[0:1, :] += da
        ddt_ref[0:1, :] += ddt
        dng_ref[0:1, :] += dng

    rev = lambda i: n_steps - 1 - i
    row = pl.BlockSpec((1, HEAD_DIM), lambda i: (0, 0))
    acc = pl.BlockSpec((8, HEAD_DIM), lambda i: (0, 0))
    outs = pl.pallas_call(
        body, name="gdn_chunk_bwd", grid=(n_steps,),
        in_specs=[pl.BlockSpec((rows, D), lambda i: (rev(i), 0)), pl.BlockSpec((rows, D), lambda i: (rev(i), 1)),
                  pl.BlockSpec((rows, D), lambda i: (rev(i), 2)),
                  pl.BlockSpec((rows, HEAD_DIM), lambda i: (rev(i), 0)),
                  pl.BlockSpec((rows, D), lambda i: (rev(i), 0)),
                  pl.BlockSpec((1, HEADS, HEAD_DIM, HEAD_DIM), lambda i: (rev(i), 0, 0, 0)),
                  pl.BlockSpec((rows, D), lambda i: (rev(i), 0)), row, row, row],
        out_specs=[pl.BlockSpec((rows, 3 * D), lambda i: (rev(i), 0)),
                   pl.BlockSpec((rows, HEAD_DIM), lambda i: (rev(i), 0)),
                   pl.BlockSpec((rows, D), lambda i: (rev(i), 0)), acc, acc, acc],
        out_shape=[jax.ShapeDtypeStruct((t, 3 * D), F32)]
        + [jax.ShapeDtypeStruct((t, HEAD_DIM), BF16), jax.ShapeDtypeStruct((t, D), BF16)]
        + [jax.ShapeDtypeStruct((8, HEAD_DIM), F32)] * 3,
        scratch_shapes=[pltpu.VMEM((HEADS, HEAD_DIM, HEAD_DIM), F32)],
        compiler_params=_cparams(("arbitrary",)),
    )(qkv_c, qkv_c, qkv_c, proj_ba, proj_gz, s_saved, do_gated, a_row, dt_row, ng_row)
    return outs


def _merge(a_out, o_gated, proj_gate, x, target, w_conf, w_gdn, w_o, pg, pb, tt=256):
    t = x.shape[0]

    def body(a_ref, o_ref, gt_ref, x_ref, y_ref, wc_ref, wg_ref, wo_ref, pg_ref, pb_ref,
             loss_ref, dpg_ref, dpb_ref, dx_ref, dgt_ref, da_ref, do_ref, h_ref, ds_ref, dyc_ref, dyg_ref):
        @pl.when(pl.program_id(0) == 0)
        def _():
            loss_ref[...] = jnp.zeros_like(loss_ref)
            dpg_ref[...] = jnp.zeros_like(dpg_ref)
            dpb_ref[...] = jnp.zeros_like(dpb_ref)

        wc, wg, wo = wc_ref[...], wg_ref[...], wo_ref[...]
        y_conf = _dot(a_ref[...], wc, _NN)
        y_gdn = _dot(o_ref[...], wg, _NN)
        sc = _sigmoid(gt_ref[:, 0:D])
        sg = _sigmoid(gt_ref[:, D:2 * D])
        h = sc * y_conf + sg * y_gdn
        z = DN_ALPHA * x_ref[...] + _dot(h, wo, _NN)
        mu = jnp.mean(z, axis=-1, keepdims=True)
        zc = z - mu
        rstd = lax.rsqrt(jnp.mean(zc * zc, axis=-1, keepdims=True) + LN_EPS)
        xhat = zc * rstd
        gain = pg_ref[...]
        err = xhat * gain + pb_ref[...] - y_ref[...]
        tok = jnp.mean(err * err, axis=-1, keepdims=True)
        loss_ref[...] += 0.5 * jnp.sum(tok, axis=0, keepdims=True)

        dy = err * (1.0 / D)
        dpg_ref[0:1, :] += jnp.sum(dy * xhat, axis=0, keepdims=True)
        dpb_ref[0:1, :] += jnp.sum(dy, axis=0, keepdims=True)
        dxh = dy * gain
        dz = rstd * (dxh - jnp.mean(dxh, axis=-1, keepdims=True)
                     - xhat * jnp.mean(dxh * xhat, axis=-1, keepdims=True))
        dx_ref[...] = DN_ALPHA * dz
        dh = _dot(dz, wo, _NT)
        dyc = dh * sc
        dyg = dh * sg
        dgt_ref[:, 0:D] = (dh * y_conf * sc * (1.0 - sc)).astype(BF16)
        dgt_ref[:, D:2 * D] = (dh * y_gdn * sg * (1.0 - sg)).astype(BF16)
        da_ref[...] = _dot(dyc, wc, _NT)
        do_ref[...] = _dot(dyg, wg, _NT)
        h_ref[...] = h.astype(BF16)
        ds_ref[...] = dz.astype(BF16)
        dyc_ref[...] = dyc.astype(BF16)
        dyg_ref[...] = dyg.astype(BF16)

    tile = pl.BlockSpec((tt, D), lambda i: (i, 0))
    wide = pl.BlockSpec((tt, 2 * D), lambda i: (i, 0))
    mat = pl.BlockSpec((D, D), lambda i: (0, 0))
    row = pl.BlockSpec((1, D), lambda i: (0, 0))
    acc = pl.BlockSpec((8, D), lambda i: (0, 0))
    act = lambda dt: jax.ShapeDtypeStruct((t, D), dt)
    return pl.pallas_call(
        body, name="merge", grid=(t // tt,),
        in_specs=[tile, tile, wide, tile, tile, mat, mat, mat, row, row],
        out_specs=[pl.BlockSpec((8, 128), lambda i: (0, 0)), acc, acc, tile, wide, tile, tile, tile, tile, tile, tile],
        out_shape=[jax.ShapeDtypeStruct((8, 128), F32), jax.ShapeDtypeStruct((8, D), F32),
                   jax.ShapeDtypeStruct((8, D), F32), act(F32), jax.ShapeDtypeStruct((t, 2 * D), BF16),
                   act(F32), act(F32), act(BF16), act(BF16), act(BF16), act(BF16)],
        compiler_params=_cparams(("arbitrary",)),
    )(a_out, o_gated, proj_gate, x, target, w_conf, w_gdn, w_o, pg, pb)


def _mesh_place():
    x, y, c = lax.axis_index("x"), lax.axis_index("y"), lax.axis_index("c")
    return x, y, c


def _flat(px, py, pc):
    return 4 * px + 2 * py + pc


def _all_gather(shards):
    n = len(shards)

    def body(*refs):
        ins, outs = refs[:n], refs[n:2 * n]
        send_sems, recv_sems, local_sems = refs[2 * n:]
        x, y, c = _mesh_place()
        me, sibling = (x, y, c), (x, y, 1 - c)
        chips = [(1 - x, y), (x, 1 - y), (1 - x, 1 - y)]

        def copy(a, k, block, to, src=None):
            dst = outs[a].at[_flat(*block)]
            return pltpu.make_async_remote_copy(
                src_ref=dst if src is None else src, dst_ref=dst,
                send_sem=send_sems.at[a, k], recv_sem=recv_sems.at[a, k],
                device_id=to, device_id_type=pl.DeviceIdType.MESH)

        mine = [pltpu.make_async_copy(ins[a], outs[a].at[_flat(*me)], local_sems.at[a]) for a in range(n)]
        for cp in mine:
            cp.start()
        first = []
        for a in range(n):
            first.append(copy(a, 0, me, sibling, src=ins[a]))
            first += [copy(a, 1 + j, me, (*chip, c), src=ins[a]) for j, chip in enumerate(chips)]
        for cp in first:
            cp.start()
        passed = []
        for j, chip in enumerate(chips):
            for a in range(n):
                copy(a, 1 + j, (*chip, c), me).wait_recv()
                fwd = copy(a, 4 + j, (*chip, c), sibling)
                fwd.start()
                passed.append(fwd)
        for a in range(n):
            copy(a, 0, sibling, me).wait_recv()
            for j, chip in enumerate(chips):
                copy(a, 4 + j, (*chip, 1 - c), me).wait_recv()
        for cp in first + passed:
            cp.wait_send()
        for cp in mine:
            cp.wait()

    any_spec = pl.BlockSpec(memory_space=pl.ANY)
    return pl.pallas_call(
        body, name="all_gather_weights",
        in_specs=[any_spec] * n, out_specs=[any_spec] * n,
        out_shape=[jax.ShapeDtypeStruct((N_DEV,) + s.shape, s.dtype) for s in shards],
        scratch_shapes=[pltpu.SemaphoreType.DMA((n, 7)), pltpu.SemaphoreType.DMA((n, 7)),
                        pltpu.SemaphoreType.DMA((n,))],
    )(*shards)


def _exchange_and_grad_x(block_arrays, small, init, segments, tm=512, tk=1024):
    nb = len(block_arrays)
    ns = len(segments)
    m, k1 = init.shape
    widths = [min(tk, a.shape[1]) for a, _ in segments]
    counts = [a.shape[1] // wd for (a, _), wd in zip(segments, widths)]
    starts = [sum(counts[:s]) for s in range(ns)]
    n_j = sum(counts)
    n_i = m // tm

    def body(*refs):
        g_refs, s_ref, i_ref = refs[:nb], refs[nb], refs[nb + 1]
        seg_refs = refs[nb + 2:nb + 2 + 2 * ns]
        outs = refs[nb + 2 + 2 * ns:]
        land_refs, sall_ref, o_ref = outs[:nb], outs[nb], outs[nb + 1]
        send_sems, recv_sems, local_sems = outs[nb + 2:]
        i, j = pl.program_id(0), pl.program_id(1)

        def copies(with_arrivals):
            x, y, c = _mesh_place()
            me = _flat(x, y, c)
            mine = [pltpu.make_async_copy(g_refs[a].at[me], land_refs[a].at[me], local_sems.at[a]) for a in range(nb)]
            mine.append(pltpu.make_async_copy(s_ref, sall_ref.at[me], local_sems.at[nb]))
            sends, recvs = [], []
            for k in range(7):
                mask = k + 1
                px = 1 - x if mask & 4 else x
                py = 1 - y if mask & 2 else y
                pc = 1 - c if mask & 1 else c
                peer = _flat(px, py, pc)
                for a in range(nb + 1):
                    kw = dict(send_sem=send_sems.at[a, k], recv_sem=recv_sems.at[a, k],
                              device_id=(px, py, pc), device_id_type=pl.DeviceIdType.MESH)
                    src = g_refs[a].at[peer] if a < nb else s_ref
                    land = land_refs[a] if a < nb else sall_ref
                    sends.append(pltpu.make_async_remote_copy(src_ref=src, dst_ref=land.at[me], **kw))
                    if with_arrivals:
                        recvs.append(pltpu.make_async_remote_copy(src_ref=src, dst_ref=land.at[peer], **kw))
            return mine, sends, recvs

        @pl.when((i == 0) & (j == 0))
        def _():
            mine, sends, _ = copies(False)
            for cp in mine + sends:
                cp.start()

        @pl.when(j == 0)
        def _():
            o_ref[...] = i_ref[...]

        for s in range(ns):
            @pl.when((j >= starts[s]) & (j < starts[s] + counts[s]))
            def _(s=s):
                o_ref[...] += lax.dot_general(seg_refs[2 * s][...], seg_refs[2 * s + 1][...], (_NT, ((), ())),
                                              preferred_element_type=F32)

        @pl.when((i == n_i - 1) & (j == n_j - 1))
        def _():
            mine, sends, recvs = copies(True)
            for cp in recvs:
                cp.wait_recv()
            for cp in sends:
                cp.wait_send()
            for cp in mine:
                cp.wait()

    any_spec = pl.BlockSpec(memory_space=pl.ANY)
    seg_specs = []
    for s in range(ns):
        col = lambda i, j, s=s: jnp.clip(j - starts[s], 0, counts[s] - 1)
        seg_specs.append(pl.BlockSpec((tm, widths[s]), lambda i, j, col=col: (i, col(i, j))))
        seg_specs.append(pl.BlockSpec((k1, widths[s]), lambda i, j, col=col: (0, col(i, j))))
    tile = pl.BlockSpec((tm, k1), lambda i, j: (i, 0))
    outs = pl.pallas_call(
        body, name="exchange_grads_and_grad_x", grid=(n_i, n_j),
        in_specs=[any_spec] * (nb + 1) + [tile] + seg_specs,
        out_specs=[any_spec] * (nb + 1) + [tile],
        out_shape=[jax.ShapeDtypeStruct(b.shape, b.dtype) for b in block_arrays]
        + [jax.ShapeDtypeStruct((N_DEV,) + small.shape, small.dtype), jax.ShapeDtypeStruct((m, k1), F32)],
        scratch_shapes=[pltpu.SemaphoreType.DMA((nb + 1, 7)), pltpu.SemaphoreType.DMA((nb + 1, 7)),
                        pltpu.SemaphoreType.DMA((nb + 1,))],
        compiler_params=_cparams(("arbitrary", "arbitrary")),
    )(*block_arrays, small, init, *[r for seg in segments for r in seg])
    return outs[:nb], outs[nb], outs[nb + 1]


def _adamw(parts, w, m, v, name, tile):
    rows, cols = w.shape

    def body(p_ref, w_ref, m_ref, v_ref, g_ref, d_ref, nm_ref, nv_ref):
        g = p_ref[0].astype(F32)
        for s in range(1, N_DEV):
            g = g + p_ref[s].astype(F32)
        nm = ADAM_B1 * m_ref[...] + (1.0 - ADAM_B1) * g
        nv = ADAM_B2 * v_ref[...] + (1.0 - ADAM_B2) * jnp.square(g)
        m_hat = nm / (1.0 - ADAM_B1 ** ADAM_STEP)
        v_hat = nv / (1.0 - ADAM_B2 ** ADAM_STEP)
        g_ref[...] = g
        d_ref[...] = -ADAM_LR * (m_hat / (jnp.sqrt(v_hat) + ADAM_EPS) + ADAM_WD * w_ref[...])
        nm_ref[...] = nm
        nv_ref[...] = nv

    blk = pl.BlockSpec((tile, cols), lambda i: (i, 0))
    out = jax.ShapeDtypeStruct((rows, cols), F32)
    return pl.pallas_call(
        body, name=name, grid=(rows // tile,),
        in_specs=[pl.BlockSpec((N_DEV, tile, cols), lambda i: (0, i, 0)), blk, blk, blk],
        out_specs=[blk, blk, blk, blk], out_shape=[out, out, out, out],
        compiler_params=_cparams(("parallel",)),
    )(parts, w, m, v)


def _rows_of(flat, n_rows):
    flat = flat.reshape(-1)
    return jnp.pad(flat, (0, n_rows * D - flat.shape[0])).reshape(n_rows, D)


def _pack_shards(conf_w_out, gdn_w_out, w_o, conf_dw_w, gdn_conv_w):
    return jnp.concatenate([conf_w_out, gdn_w_out, w_o, _rows_of(conf_dw_w, 16), _rows_of(gdn_conv_w, 16)], axis=0)


def _unpack_shards(p):
    dw = p[ROW_DW:ROW_DW + 4].reshape(-1)[:K_CONF * 128].reshape(K_CONF, 128)
    gc = p[ROW_GC:ROW_GC + 2].reshape(-1)[:K_GDN * 384].reshape(K_GDN, 384)
    return p[ROW_CWO:ROW_CWO + 128], p[ROW_GWO:ROW_GWO + 128], p[ROW_WO:ROW_WO + 128], dw, gc


def _pack_small(dw_b, ln_g, ln_b, pg, pb, ng, a_log, dt_bias, loss=None):
    s = jnp.zeros((SMALL_ROWS, D), F32)
    for r, val in enumerate((dw_b, ln_g, ln_b, pg, pb, ng, a_log, dt_bias)):
        s = s.at[r, :val.shape[0]].set(val)
    if loss is not None:
        s = s.at[8, 0].set(loss)
    return s


def _unpack_small(s):
    return (s[0], s[1], s[2], s[3], s[4], s[5, :HEAD_DIM], s[6, :HEADS], s[7, :HEADS])


def _scatter_blocks(g_cwo, g_gwo, g_wo, g_dw, g_gc):
    dw = g_dw.reshape(K_CONF, N_DEV, 128).transpose(1, 0, 2).reshape(N_DEV, K_CONF * 128)
    dw = jnp.pad(dw, ((0, 0), (0, 16 * D - K_CONF * 128))).reshape(N_DEV, 16, D)
    gc = g_gc.reshape(K_GDN, N_DEV, 384).transpose(1, 0, 2).reshape(N_DEV, K_GDN * 384)
    gc = jnp.pad(gc, ((0, 0), (0, 16 * D - K_GDN * 384))).reshape(N_DEV, 16, D)
    return jnp.concatenate([g_cwo.reshape(N_DEV, 128, D), g_gwo.reshape(N_DEV, 128, D),
                            g_wo.reshape(N_DEV, 128, D), dw, gc], axis=1)


def kernel(x, w_in, conf_dw_w, conf_dw_b, conf_ln_g, conf_ln_b, conf_w_out, gdn_conv_w, gdn_A_log, gdn_dt_bias, gdn_norm_g, gdn_w_out, w_o, post_ln_g, post_ln_b, loss_target, m_w_in, m_conf_dw_w, m_conf_dw_b, m_conf_ln_g, m_conf_ln_b, m_conf_w_out, m_gdn_conv_w, m_gdn_A_log, m_gdn_dt_bias, m_gdn_norm_g, m_gdn_w_out, m_w_o, m_post_ln_g, m_post_ln_b, v_w_in, v_conf_dw_w, v_conf_dw_b, v_conf_ln_g, v_conf_ln_b, v_conf_w_out, v_gdn_conv_w, v_gdn_A_log, v_gdn_dt_bias, v_gdn_norm_g, v_gdn_w_out, v_w_o, v_post_ln_g, v_post_ln_b):
    t = x.shape[1]
    x2 = x.reshape(t, D)
    target = loss_target.reshape(t, D)
    x_bf = x2.astype(BF16)

    w_pack = _pack_shards(conf_w_out, gdn_w_out, w_o, conf_dw_w, gdn_conv_w)
    convw = jnp.concatenate([_rows_of(conf_dw_w, 8), _rows_of(gdn_conv_w, 8)], axis=0)
    all_w_in, all_w, all_convw = _all_gather([w_in.astype(BF16), w_pack.astype(BF16), convw])
    w_full = all_w_in.transpose(1, 0, 2).reshape(D, W_IN_COLS)
    w_conf = w_full[:, 0:3 * D]
    w_qkv = w_full[:, 3 * D:6 * D]
    w_gz = w_full[:, 6 * D:7 * D]
    w_ba = jnp.pad(w_full[:, 7 * D:7 * D + 2 * HEADS], ((0, 0), (0, HEAD_DIM - 2 * HEADS)))
    w_gate = w_full[:, 7 * D + 2 * HEADS:]
    cwo_full = all_w[:, ROW_CWO:ROW_CWO + 128].reshape(D, D)
    gwo_full = all_w[:, ROW_GWO:ROW_GWO + 128].reshape(D, D)
    wo_full = all_w[:, ROW_WO:ROW_WO + 128].reshape(D, D)
    dw_full = all_convw[:, 0:4].reshape(N_DEV, 4 * D)[:, :K_CONF * 128].reshape(N_DEV, K_CONF, 128)
    dw_full = jnp.pad(dw_full.transpose(1, 0, 2).reshape(K_CONF, D), ((0, 32 - K_CONF), (0, 0)))
    gc_full = all_convw[:, 8:10].reshape(N_DEV, 2 * D)[:, :K_GDN * 384].reshape(N_DEV, K_GDN, 384)
    gc_full = jnp.pad(gc_full.transpose(1, 0, 2).reshape(K_GDN, 3 * D), ((0, 8 - K_GDN), (0, 0)))

    row = lambda vec: vec.reshape(1, -1)
    lane_row = lambda vec, at: jnp.zeros((1, HEAD_DIM), F32).at[0, at:at + vec.shape[0]].set(vec)
    a_row = lane_row(gdn_A_log, HEADS)
    dt_row = lane_row(gdn_dt_bias, HEADS)
    ng_row = row(gdn_norm_g)

    proj_conf = _matmul_nn(x_bf, w_conf, "proj_conf")
    proj_qkv = _matmul_nn(x_bf, w_qkv, "proj_qkv")
    proj_gz = _matmul_nn(x_bf, w_gz, "proj_gz")
    proj_gate = _matmul_nn(x_bf, w_gate, "proj_gate")
    proj_ba = _matmul_nn(x_bf, w_ba, "proj_ba")
    cpre, a_out = _conf_fwd(proj_conf, dw_full, row(conf_dw_b), row(conf_ln_g), row(conf_ln_b))
    qkv_c = _gdn_conv_fwd(proj_qkv, gc_full)
    o_gated, s_saved = _gdn_chunk_fwd(qkv_c, proj_ba, proj_gz, a_row, dt_row, ng_row)

    (loss_acc, d_pg, d_pb, dx, d_gate, da_out, do_gated, h_bf, dsub_bf, dyc_bf, dyg_bf) = _merge(
        a_out, o_gated, proj_gate, x2, target, cwo_full, gwo_full, wo_full, row(post_ln_g), row(post_ln_b))
    g_wo = _matmul_tn(h_bf, dsub_bf, "grad_w_o")
    g_cwo = _matmul_tn(a_out, dyc_bf, "grad_conf_w_out")
    g_gwo = _matmul_tn(o_gated, dyg_bf, "grad_gdn_w_out")

    dqkv_c, d_ba, d_gz, d_a_row, d_dt_row, d_ng_row = _gdn_chunk_bwd(
        qkv_c, proj_ba, proj_gz, s_saved, do_gated, a_row, dt_row, ng_row)
    d_qkv, g_gc = _gdn_conv_bwd(dqkv_c, proj_qkv, gc_full)

    dcpre, dcz, d_ln_g, d_ln_b = _conf_bwd_post(cpre, proj_conf, da_out, row(conf_ln_g), row(conf_ln_b))
    d_conf, g_dw, g_dwb = _conf_bwd_conv(dcpre, proj_conf, dcz, dw_full)

    segments = [(d_conf, w_conf, "conf"), (d_qkv, w_qkv, "qkv"), (d_gz, w_gz, "gz"),
                (d_ba, w_ba, "ba"), (d_gate, w_gate, "gate")]
    g_cols = {tag: _matmul_tn(x_bf, d_seg, "grad_w_in_" + tag) for d_seg, _, tag in segments}
    g_w_in = jnp.concatenate([g_cols["conf"], g_cols["qkv"], g_cols["gz"], g_cols["ba"][:, :2 * HEADS],
                              g_cols["gate"]], axis=1)

    w_in_blocks = g_w_in.reshape(D, N_DEV, W_IN_SHARD).transpose(1, 0, 2).astype(BF16)
    blocks = _scatter_blocks(g_cwo, g_gwo, g_wo, g_dw[:K_CONF], g_gc[:K_GDN]).astype(BF16)
    small = _pack_small(g_dwb[0], d_ln_g[0], d_ln_b[0], d_pg[0], d_pb[0], d_ng_row[0],
                        d_a_row[0, HEADS:2 * HEADS], d_dt_row[0, HEADS:2 * HEADS], loss_acc[0, 0])
    (landed_w_in, landed), small_all, dx = _exchange_and_grad_x(
        [w_in_blocks, blocks], small, dx, [(d_seg, w_seg) for d_seg, w_seg, _ in segments])

    m_pack = _pack_shards(m_conf_w_out, m_gdn_w_out, m_w_o, m_conf_dw_w, m_gdn_conv_w)
    v_pack = _pack_shards(v_conf_w_out, v_gdn_w_out, v_w_o, v_conf_dw_w, v_gdn_conv_w)
    big_w_in = _adamw(landed_w_in, w_in, m_w_in, v_w_in, "adamw_w_in", W_IN_TILE)
    big = _adamw(landed, w_pack, m_pack, v_pack, "adamw_shards", PACK_TILE)
    ws = _pack_small(conf_dw_b, conf_ln_g, conf_ln_b, post_ln_g, post_ln_b, gdn_norm_g, gdn_A_log, gdn_dt_bias)
    ms = _pack_small(m_conf_dw_b, m_conf_ln_g, m_conf_ln_b, m_post_ln_g, m_post_ln_b, m_gdn_norm_g, m_gdn_A_log,
                     m_gdn_dt_bias)
    vs = _pack_small(v_conf_dw_b, v_conf_ln_g, v_conf_ln_b, v_post_ln_g, v_post_ln_b, v_gdn_norm_g, v_gdn_A_log,
                     v_gdn_dt_bias)
    sml = _adamw(small_all, ws, ms, vs, "adamw_replicated", SMALL_ROWS)

    loss = sml[0][8, 0]
    outs = []
    for b_w_in, big_k, sml_k in zip(big_w_in, big, sml):
        b_cwo, b_gwo, b_wo, b_dw, b_gc = _unpack_shards(big_k)
        s_dwb, s_lng, s_lnb, s_pg, s_pb, s_ng, s_a, s_dt = _unpack_small(sml_k)
        outs.append([b_w_in, b_dw, s_dwb, s_lng, s_lnb, b_cwo, b_gc, s_a, s_dt, s_ng, b_gwo, b_wo, s_pg, s_pb])
    return (loss, dx.reshape(1, t, D), *outs[0], *outs[1], *outs[2], *outs[3])
```

```python
import functools

import jax
import jax.numpy as jnp
from jax import lax
from jax.experimental import pallas as pl
from jax.experimental.pallas import tpu as pltpu

F32 = jnp.float32
BF16 = jnp.bfloat16

N_DEV = 8
D = 1024
HEADS = 8
HEAD_DIM = 128
CHUNK = 64
CHUNKS_PER_STEP = 2
K_CONF = 31
K_GDN = 4
HALO_CONF = 32
HALO_GDN = 8
LN_EPS = 1e-5
RMS_EPS = 1e-6
L2_EPS = 1e-6
DN_ALPHA = 2.0 ** 0.25
ADAM_LR = 0.001
ADAM_B1 = 0.9
ADAM_B2 = 0.999
ADAM_EPS = 1e-08
ADAM_WD = 0.01
ADAM_STEP = 10

W_IN_COLS = 9232
W_IN_SHARD = W_IN_COLS // N_DEV
ROW_CWO = 0
ROW_GWO = ROW_CWO + 128
ROW_WO = ROW_GWO + 128
ROW_DW = ROW_WO + 128
ROW_GC = ROW_DW + 16
PACK_ROWS = ROW_GC + 16
PACK_TILE = PACK_ROWS // 2
W_IN_TILE = 128
SMALL_ROWS = 16
CONVW_ROWS = 16

VMEM_LIMIT = 56 * 1024 * 1024

_NN = ((1,), (0,))
_NT = ((1,), (1,))
_TN = ((0,), (0,))


def _cparams(sem=None):
    return pltpu.CompilerParams(dimension_semantics=sem, vmem_limit_bytes=VMEM_LIMIT)


def _dot(a, b, dims, hi=False):
    dn = (dims, ((), ()))
    a_hi = a.astype(BF16)
    b_hi = b.astype(BF16)
    if not hi:
        return lax.dot_general(a_hi, b_hi, dn, preferred_element_type=F32)
    a_lo = (a - a_hi.astype(F32)).astype(BF16)
    b_lo = (b - b_hi.astype(F32)).astype(BF16)
    d = lambda p, q: lax.dot_general(p, q, dn, preferred_element_type=F32)
    return d(a_hi, b_hi) + (d(a_hi, b_lo) + d(a_lo, b_hi))


def _make_mm(kind, hi):
    dims = {"nn": _NN, "nt": _NT, "tn": _TN}[kind]

    @jax.custom_vjp
    def mm(a, b):
        return _dot(a, b, dims, hi)

    def fwd(a, b):
        return _dot(a, b, dims, hi), (a, b)

    def bwd(res, g):
        a, b = res
        if kind == "nn":
            return _dot(g, b, _NT, hi), _dot(a, g, _TN, hi)
        if kind == "nt":
            return _dot(g, b, _NN, hi), _dot(g, a, _TN, hi)
        return _dot(b, g, _NT, hi), _dot(a, g, _NN, hi)

    mm.defvjp(fwd, bwd)
    return mm


_mm_nn = _make_mm("nn", False)
_mm_nt = _make_mm("nt", False)
_mm_tn = _make_mm("tn", False)
_mm_nn_hi = _make_mm("nn", True)
_mm_tn_hi = _make_mm("tn", True)


def _tri_inv_impl(lows):
    c = lows[0].shape[0]
    eye = (lax.broadcasted_iota(jnp.int32, (c, c), 0) == lax.broadcasted_iota(jnp.int32, (c, c), 1)).astype(F32)
    ms = [-low for low in lows]
    ps = [eye + m for m in ms]
    steps = max(c.bit_length() - 2, 0)
    for _ in range(steps):
        ms = [_dot(m, m, _NN) for m in ms]
        ps = [p + _dot(p, m, _NN) for p, m in zip(ps, ms)]
    rs = [eye - p - _dot(low, p, _NN, True) for low, p in zip(lows, ps)]
    return [p + _dot(p, r, _NN, True) for p, r in zip(ps, rs)]


@jax.custom_vjp
def _tri_inv(lows):
    return _tri_inv_impl(lows)


def _tri_inv_fwd(lows):
    xs = _tri_inv_impl(lows)
    return xs, xs


def _tri_inv_bwd(xs, dxs):
    ts = [_dot(x, dx, _TN) for x, dx in zip(xs, dxs)]
    return ([-_dot(t, x, _NT) for t, x in zip(ts, xs)],)


_tri_inv.defvjp(_tri_inv_fwd, _tri_inv_bwd)


def _sigmoid(x):
    return jax.nn.sigmoid(x)


def _silu(x):
    return x * jax.nn.sigmoid(x)


def _softplus(x):
    u = jnp.exp(-jnp.abs(x))
    log1p_u = jnp.where(u < 1e-3, u * (1.0 - u * (0.5 - u * (1.0 / 3.0))), jnp.log(1.0 + u))
    return jnp.maximum(x, 0.0) + log1p_u


def _layernorm(x, g, b):
    mu = jnp.mean(x, axis=-1, keepdims=True)
    xc = x - mu
    var = jnp.mean(xc * xc, axis=-1, keepdims=True)
    return xc * lax.rsqrt(var + LN_EPS) * g + b


def _pick_lane(x, lane):
    idx = lax.broadcasted_iota(jnp.int32, x.shape, 1)
    return jnp.sum(jnp.where(idx == lane, x, 0.0), axis=1, keepdims=True)


def _gdn_chunk(q_list, k_list, v_list, ba_list, gz_list, s_list, a_row, dt_row, ng_row):
    c = ba_list[0].shape[0]
    n_chunks = len(ba_list)
    pairs = [(ci, h) for ci in range(n_chunks) for h in range(HEADS)]
    every = range(len(pairs))
    rows = lax.broadcasted_iota(jnp.int32, (c, c), 0)
    cols = lax.broadcasted_iota(jnp.int32, (c, c), 1)
    causal = rows >= cols
    strict = rows > cols
    tril = causal.astype(F32)
    triu = (rows <= cols).astype(F32)
    last_row = lax.broadcasted_iota(jnp.int32, (c, 1), 0) == c - 1
    sub8 = lax.broadcasted_iota(jnp.int32, (HEADS, c), 0)

    beta_all = [_sigmoid(ba) for ba in ba_list]
    g_all = [-jnp.exp(a_row) * _softplus(ba + dt_row) for ba in ba_list]
    gc_all = [_mm_nn_hi(tril, g) for g in g_all]
    gc_t = [_mm_tn_hi(g, triu)[HEADS:2 * HEADS, :] for g in g_all]

    q = [_silu(a) for a in q_list]
    k = [_silu(a) for a in k_list]
    v = [_silu(a) for a in v_list]
    q = [a * lax.rsqrt(jnp.sum(a * a, axis=-1, keepdims=True) + L2_EPS) * (HEAD_DIM ** -0.5) for a in q]
    k = [a * lax.rsqrt(jnp.sum(a * a, axis=-1, keepdims=True) + L2_EPS) for a in k]
    beta = [_pick_lane(beta_all[ci], h) for ci, h in pairs]
    gc = [_pick_lane(gc_all[ci], HEADS + h) for ci, h in pairs]
    gc_cols = [jnp.sum(jnp.where(sub8 == h, gc_t[ci], 0.0), axis=0, keepdims=True) for ci, h in pairs]
    decay = [jnp.where(causal, jnp.exp(jnp.where(causal, gc[p] - gc_cols[p], 0.0)), 0.0) for p in every]
    kb = [k[p] * beta[p] for p in every]
    low = [jnp.where(strict, _mm_nt(kb[p], k[p]) * decay[p], 0.0) for p in every]
    x = _tri_inv(low)
    eg = [jnp.exp(gc[p]) for p in every]
    u = [_mm_nn(x[p], v[p] * beta[p]) for p in every]
    w = [_mm_nn(x[p], kb[p] * eg[p]) for p in every]
    intra = [_mm_nt(q[p], k[p]) * decay[p] for p in every]
    q_dec = [q[p] * eg[p] for p in every]
    g_last = [jnp.sum(jnp.where(last_row, gc[p], 0.0), axis=0, keepdims=True) for p in every]
    k_dec = [k[p] * jnp.exp(g_last[p] - gc[p]) for p in every]
    s_dec = [jnp.exp(g_last[p]) for p in every]

    o = []
    state = list(s_list)
    for ci in range(n_chunks):
        at = [ci * HEADS + h for h in range(HEADS)]
        v_new = [u[p] - _mm_nn(w[p], state[h]) for h, p in enumerate(at)]
        o += [_mm_nn(q_dec[p], state[h]) + _mm_nn(intra[p], v_new[h]) for h, p in enumerate(at)]
        state = [state[h] * s_dec[p] + _mm_tn(k_dec[p], v_new[h]) for h, p in enumerate(at)]
    o = [a * lax.rsqrt(jnp.mean(a * a, axis=-1, keepdims=True) + RMS_EPS) * ng_row for a in o]
    o = [o[p] * _silu(gz_list[p]) for p in every]
    return o, state


def _conf_post(cpre, cz, g, b):
    return _silu(_layernorm(cpre, g, b)) * _silu(cz)


def _matmul_nn(a, b, name, tm=512, tn=1024):
    m, k = a.shape
    n = b.shape[1]
    tn = min(tn, n)

    def body(a_ref, b_ref, o_ref):
        o_ref[...] = jnp.dot(a_ref[...], b_ref[...], preferred_element_type=F32)

    return pl.pallas_call(
        body, name=name, grid=(n // tn, m // tm),
        in_specs=[pl.BlockSpec((tm, k), lambda j, i: (i, 0)), pl.BlockSpec((k, tn), lambda j, i: (0, j))],
        out_specs=pl.BlockSpec((tm, tn), lambda j, i: (i, j)),
        out_shape=jax.ShapeDtypeStruct((m, n), F32),
        compiler_params=_cparams(("parallel", "parallel")),
    )(a, b)


def _matmul_tn(a, b, name, tt=512, tn=1024):
    t, k1 = a.shape
    n = b.shape[1]
    tn = min(tn, n)

    def body(a_ref, b_ref, o_ref):
        @pl.when(pl.program_id(1) == 0)
        def _():
            o_ref[...] = jnp.zeros_like(o_ref)

        o_ref[...] += lax.dot_general(a_ref[...], b_ref[...], (_TN, ((), ())), preferred_element_type=F32)

    return pl.pallas_call(
        body, name=name, grid=(n // tn, t // tt),
        in_specs=[pl.BlockSpec((tt, k1), lambda j, i: (i, 0)), pl.BlockSpec((tt, tn), lambda j, i: (i, j))],
        out_specs=pl.BlockSpec((k1, tn), lambda j, i: (0, j)),
        out_shape=jax.ShapeDtypeStruct((k1, n), F32),
        compiler_params=_cparams(("parallel", "arbitrary")),
    )(a, b)


def _build_bank(bank_ref, shifts):
    ext = bank_ref[0]
    rows = ext.shape[0]
    for s in shifts:
        if s:
            bank_ref[s] = pltpu.roll(ext, rows - s, axis=0)


def _conv_taps(bank_ref, w_ref, offsets, n_rows, width, emit):
    def piece(rc, carry):
        r0 = pl.multiple_of(rc * 16, 16)
        for cb in range(width // 128):
            lanes = slice(cb * 128, (cb + 1) * 128)
            acc = jnp.zeros((16, 128), F32)
            for k, off in enumerate(offsets):
                m, s = divmod(off, 8)
                acc = acc + bank_ref[s, pl.ds(r0 + 8 * m, 16), lanes] * w_ref[k:k + 1, lanes]
            emit(r0, lanes, acc)
        return carry

    lax.fori_loop(0, n_rows // 16, piece, 0)


def _conv_dw(bank_ref, d_ref, offsets, n_rows, width, emit):
    ms = [divmod(off, 8) for off in offsets]
    n_taps = len(offsets)
    group = max(1, 32 // n_taps)
    blocks = [slice(cb * 128, (cb + 1) * 128) for cb in range(width // 128)]
    for g0 in range(0, len(blocks), group):
        lane_group = blocks[g0:g0 + group]

        def piece(rc, accs, lane_group=lane_group):
            r0 = pl.multiple_of(rc * 8, 8)
            out = []
            for b, lanes in enumerate(lane_group):
                d = d_ref[pl.ds(r0, 8), lanes]
                out += [accs[b * n_taps + k] + d * bank_ref[s, pl.ds(r0 + 8 * m, 8), lanes]
                        for k, (m, s) in enumerate(ms)]
            return tuple(out)

        init = tuple(jnp.zeros((8, 128), F32) for _ in range(len(lane_group) * n_taps))
        accs = lax.fori_loop(0, n_rows // 8, piece, init)
        for b, lanes in enumerate(lane_group):
            for k in range(n_taps):
                emit(k, lanes, jnp.sum(accs[b * n_taps + k], axis=0, keepdims=True))


def _conf_fwd(proj_conf, dw_w, dw_b, ln_g, ln_b, tt=256):
    t = proj_conf.shape[0]
    hb = tt // HALO_CONF
    offsets = [HALO_CONF - (K_CONF - 1) + k for k in range(K_CONF)]

    def body(cv_ref, cg_ref, cz_ref, cvh_ref, cgh_ref, w_ref, b_ref, g_ref, bb_ref, cpre_ref, aout_ref, bank_ref):
        first = pl.program_id(0) == 0
        halo = cvh_ref[...] * _sigmoid(cgh_ref[...])
        bank_ref[0, 0:HALO_CONF, :] = jnp.where(first, 0.0, halo)
        bank_ref[0, HALO_CONF:, :] = cv_ref[...] * _sigmoid(cg_ref[...])
        _build_bank(bank_ref, range(8))

        def emit(r0, lanes, acc):
            cpre_ref[pl.ds(r0, 16), lanes] = acc + b_ref[0:1, lanes]

        _conv_taps(bank_ref, w_ref, offsets, tt, D, emit)
        aout_ref[...] = _conf_post(cpre_ref[...], cz_ref[...], g_ref[...], bb_ref[...]).astype(BF16)

    row = pl.BlockSpec((1, D), lambda i: (0, 0))
    return pl.pallas_call(
        body, name="conf_fwd", grid=(t // tt,),
        in_specs=[pl.BlockSpec((tt, D), lambda i: (i, 0)), pl.BlockSpec((tt, D), lambda i: (i, 1)),
                  pl.BlockSpec((tt, D), lambda i: (i, 2)),
                  pl.BlockSpec((HALO_CONF, D), lambda i: (jnp.maximum(i * hb - 1, 0), 0)),
                  pl.BlockSpec((HALO_CONF, D), lambda i: (jnp.maximum(i * hb - 1, 0), 1)),
                  pl.BlockSpec((32, D), lambda i: (0, 0)), row, row, row],
        out_specs=[pl.BlockSpec((tt, D), lambda i: (i, 0)), pl.BlockSpec((tt, D), lambda i: (i, 0))],
        out_shape=[jax.ShapeDtypeStruct((t, D), F32), jax.ShapeDtypeStruct((t, D), BF16)],
        scratch_shapes=[pltpu.VMEM((8, tt + HALO_CONF, D), F32)],
        compiler_params=_cparams(("parallel",)),
    )(proj_conf, proj_conf, proj_conf, proj_conf, proj_conf, dw_w, dw_b, ln_g, ln_b)


def _conf_bwd_post(cpre, proj_conf, da_out, ln_g, ln_b, tt=256):
    t = cpre.shape[0]

    def body(c_ref, z_ref, da_ref, g_ref, b_ref, dc_ref, dz_ref, dg_ref, db_ref):
        @pl.when(pl.program_id(0) == 0)
        def _():
            dg_ref[...] = jnp.zeros_like(dg_ref)
            db_ref[...] = jnp.zeros_like(db_ref)

        _, vjp = jax.vjp(_conf_post, c_ref[...], z_ref[...], g_ref[...], b_ref[...])
        dc, dz, dg, db = vjp(da_ref[...])
        dc_ref[...] = dc
        dz_ref[...] = dz.astype(BF16)
        dg_ref[0:1, :] += dg
        db_ref[0:1, :] += db

    row = pl.BlockSpec((1, D), lambda i: (0, 0))
    acc = pl.BlockSpec((8, D), lambda i: (0, 0))
    return pl.pallas_call(
        body, name="conf_bwd_post", grid=(t // tt,),
        in_specs=[pl.BlockSpec((tt, D), lambda i: (i, 0)), pl.BlockSpec((tt, D), lambda i: (i, 2)),
                  pl.BlockSpec((tt, D), lambda i: (i, 0)), row, row],
        out_specs=[pl.BlockSpec((tt, D), lambda i: (i, 0)), pl.BlockSpec((tt, D), lambda i: (i, 0)), acc, acc],
        out_shape=[jax.ShapeDtypeStruct((t, D), F32), jax.ShapeDtypeStruct((t, D), BF16),
                   jax.ShapeDtypeStruct((8, D), F32), jax.ShapeDtypeStruct((8, D), F32)],
        compiler_params=_cparams(("arbitrary",)),
    )(cpre, proj_conf, da_out, ln_g, ln_b)


def _conf_bwd_conv(dcpre, proj_conf, dcz, dw_w, tt=256):
    t = dcpre.shape[0]
    n_tiles = t // tt
    hb = tt // HALO_CONF
    n_hb = t // HALO_CONF
    offsets = [K_CONF - 1 - k for k in range(K_CONF)]

    def body(d_ref, dn_ref, cv_ref, cg_ref, dz_ref, w_ref, dp_ref, dw_ref, db_ref, bank_d, a_scr, da_scr):
        i = pl.program_id(0)

        @pl.when(i == 0)
        def _():
            dw_ref[...] = jnp.zeros_like(dw_ref)
            db_ref[...] = jnp.zeros_like(db_ref)

        cv = cv_ref[...]
        sg = _sigmoid(cg_ref[...])
        a_scr[...] = cv * sg
        bank_d[0, 0:tt, :] = d_ref[...]
        bank_d[0, tt:, :] = jnp.where(i == n_tiles - 1, 0.0, dn_ref[...])
        _build_bank(bank_d, range(8))

        def emit_da(r0, lanes, acc):
            da_scr[pl.ds(r0, 16), lanes] = acc

        _conv_taps(bank_d, w_ref, offsets, tt, D, emit_da)
        da = da_scr[...]
        dp_ref[:, 0:D] = (da * sg).astype(BF16)
        dp_ref[:, D:2 * D] = (da * cv * sg * (1.0 - sg)).astype(BF16)
        dp_ref[:, 2 * D:3 * D] = dz_ref[...]

        def emit_dw(k, lanes, row):
            dw_ref[k:k + 1, lanes] += row

        _conv_dw(bank_d, a_scr, offsets, tt, D, emit_dw)
        db_ref[0:1, :] += jnp.sum(d_ref[...], axis=0, keepdims=True)

    nxt = lambda i: jnp.minimum((i + 1) * hb, n_hb - 1)
    return pl.pallas_call(
        body, name="conf_bwd_conv", grid=(n_tiles,),
        in_specs=[pl.BlockSpec((tt, D), lambda i: (i, 0)), pl.BlockSpec((HALO_CONF, D), lambda i: (nxt(i), 0)),
                  pl.BlockSpec((tt, D), lambda i: (i, 0)), pl.BlockSpec((tt, D), lambda i: (i, 1)),
                  pl.BlockSpec((tt, D), lambda i: (i, 0)), pl.BlockSpec((32, D), lambda i: (0, 0))],
        out_specs=[pl.BlockSpec((tt, 3 * D), lambda i: (i, 0)), pl.BlockSpec((32, D), lambda i: (0, 0)),
                   pl.BlockSpec((8, D), lambda i: (0, 0))],
        out_shape=[jax.ShapeDtypeStruct((t, 3 * D), BF16), jax.ShapeDtypeStruct((32, D), F32),
                   jax.ShapeDtypeStruct((8, D), F32)],
        scratch_shapes=[pltpu.VMEM((8, tt + HALO_CONF, D), F32), pltpu.VMEM((tt, D), F32), pltpu.VMEM((tt, D), F32)],
        compiler_params=_cparams(("arbitrary",)),
    )(dcpre, dcpre, proj_conf, proj_conf, dcz, dw_w)


def _gdn_conv_fwd(proj_qkv, conv_w, tt=256):
    t, width = proj_qkv.shape
    hb = tt // HALO_GDN
    offsets = [HALO_GDN - (K_GDN - 1) + k for k in range(K_GDN)]
    shifts = sorted({off % 8 for off in offsets})

    def body(x_ref, xh_ref, w_ref, o_ref, bank_ref):
        bank_ref[0, 0:HALO_GDN, :] = jnp.where(pl.program_id(1) == 0, 0.0, xh_ref[...])
        bank_ref[0, HALO_GDN:, :] = x_ref[...]
        _build_bank(bank_ref, shifts)

        def emit(r0, lanes, acc):
            o_ref[pl.ds(r0, 16), lanes] = acc

        _conv_taps(bank_ref, w_ref, offsets, tt, D, emit)

    return pl.pallas_call(
        body, name="gdn_conv_fwd", grid=(width // D, t // tt),
        in_specs=[pl.BlockSpec((tt, D), lambda j, i: (i, j)),
                  pl.BlockSpec((HALO_GDN, D), lambda j, i: (jnp.maximum(i * hb - 1, 0), j)),
                  pl.BlockSpec((8, D), lambda j, i: (0, j))],
        out_specs=pl.BlockSpec((tt, D), lambda j, i: (i, j)),
        out_shape=jax.ShapeDtypeStruct((t, width), F32),
        scratch_shapes=[pltpu.VMEM((8, tt + HALO_GDN, D), F32)],
        compiler_params=_cparams(("parallel", "parallel")),
    )(proj_qkv, proj_qkv, conv_w)


def _gdn_conv_bwd(dqkv_c, proj_qkv, conv_w, tt=256):
    t, width = proj_qkv.shape
    n_tiles = t // tt
    hb = tt // HALO_GDN
    n_hb = t // HALO_GDN
    offsets = [K_GDN - 1 - k for k in range(K_GDN)]

    def body(d_ref, dn_ref, x_ref, w_ref, dx_ref, dw_ref, bank_d):
        i = pl.program_id(1)

        @pl.when(i == 0)
        def _():
            dw_ref[...] = jnp.zeros_like(dw_ref)

        bank_d[0, 0:tt, :] = d_ref[...]
        bank_d[0, tt:, :] = jnp.where(i == n_tiles - 1, 0.0, dn_ref[...])
        _build_bank(bank_d, sorted({off % 8 for off in offsets}))

        def emit_dx(r0, lanes, acc):
            dx_ref[pl.ds(r0, 16), lanes] = acc.astype(BF16)

        _conv_taps(bank_d, w_ref, offsets, tt, D, emit_dx)

        def emit_dw(k, lanes, row):
            dw_ref[k:k + 1, lanes] += row

        _conv_dw(bank_d, x_ref, offsets, tt, D, emit_dw)

    return pl.pallas_call(
        body, name="gdn_conv_bwd", grid=(width // D, n_tiles),
        in_specs=[pl.BlockSpec((tt, D), lambda j, i: (i, j)),
                  pl.BlockSpec((HALO_GDN, D), lambda j, i: (jnp.minimum((i + 1) * hb, n_hb - 1), j)),
                  pl.BlockSpec((tt, D), lambda j, i: (i, j)),
                  pl.BlockSpec((8, D), lambda j, i: (0, j))],
        out_specs=[pl.BlockSpec((tt, D), lambda j, i: (i, j)), pl.BlockSpec((8, D), lambda j, i: (0, j))],
        out_shape=[jax.ShapeDtypeStruct((t, width), BF16), jax.ShapeDtypeStruct((8, width), F32)],
        scratch_shapes=[pltpu.VMEM((8, tt + HALO_GDN, D), F32)],
        compiler_params=_cparams(("parallel", "arbitrary")),
    )(dqkv_c, dqkv_c, proj_qkv, conv_w)


def _pair_rows(ci):
    return slice(ci * CHUNK, (ci + 1) * CHUNK)


def _pair_lanes(h, base=0):
    return slice(base + h * HEAD_DIM, base + (h + 1) * HEAD_DIM)


def _pair_slices(ref):
    return [ref[_pair_rows(ci), _pair_lanes(h)] for ci in range(CHUNKS_PER_STEP) for h in range(HEADS)]


def _gdn_chunk_fwd(qkv_c, proj_ba, proj_gz, a_row, dt_row, ng_row):
    t = qkv_c.shape[0]
    rows = CHUNKS_PER_STEP * CHUNK
    n_steps = t // rows

    def body(q_ref, k_ref, v_ref, ba_ref, gz_ref, a_ref, dt_ref, ng_ref, o_ref, ssave_ref, s_scr):
        @pl.when(pl.program_id(0) == 0)
        def _():
            s_scr[...] = jnp.zeros_like(s_scr)

        s_list = [s_scr[h] for h in range(HEADS)]
        for h in range(HEADS):
            ssave_ref[0, h] = s_list[h]
        ba_list = [ba_ref[_pair_rows(ci), :] for ci in range(CHUNKS_PER_STEP)]
        o_list, s_new = _gdn_chunk(_pair_slices(q_ref), _pair_slices(k_ref), _pair_slices(v_ref), ba_list,
                                   _pair_slices(gz_ref), s_list, a_ref[...], dt_ref[...], ng_ref[...])
        for ci in range(CHUNKS_PER_STEP):
            for h in range(HEADS):
                o_ref[_pair_rows(ci), _pair_lanes(h)] = o_list[ci * HEADS + h].astype(BF16)
        for h in range(HEADS):
            s_scr[h] = s_new[h]

    row = pl.BlockSpec((1, HEAD_DIM), lambda i: (0, 0))
    return pl.pallas_call(
        body, name="gdn_chunk_fwd", grid=(n_steps,),
        in_specs=[pl.BlockSpec((rows, D), lambda i: (i, 0)), pl.BlockSpec((rows, D), lambda i: (i, 1)),
                  pl.BlockSpec((rows, D), lambda i: (i, 2)), pl.BlockSpec((rows, HEAD_DIM), lambda i: (i, 0)),
                  pl.BlockSpec((rows, D), lambda i: (i, 0)), row, row, row],
        out_specs=[pl.BlockSpec((rows, D), lambda i: (i, 0)),
                   pl.BlockSpec((1, HEADS, HEAD_DIM, HEAD_DIM), lambda i: (i, 0, 0, 0))],
        out_shape=[jax.ShapeDtypeStruct((t, D), BF16),
                   jax.ShapeDtypeStruct((n_steps, HEADS, HEAD_DIM, HEAD_DIM), F32)],
        scratch_shapes=[pltpu.VMEM((HEADS, HEAD_DIM, HEAD_DIM), F32)],
        compiler_params=_cparams(("arbitrary",)),
    )(qkv_c, qkv_c, qkv_c, proj_ba, proj_gz, a_row, dt_row, ng_row)


def _gdn_chunk_bwd(qkv_c, proj_ba, proj_gz, s_saved, do_gated, a_row, dt_row, ng_row):
    t = qkv_c.shape[0]
    rows = CHUNKS_PER_STEP * CHUNK
    n_steps = t // rows

    def body(q_ref, k_ref, v_ref, ba_ref, gz_ref, s_ref, do_ref, a_ref, dt_ref, ng_ref,
             dqkv_ref, dba_ref, dgz_ref, da_ref, ddt_ref, dng_ref, ds_scr):
        @pl.when(pl.program_id(0) == 0)
        def _():
            ds_scr[...] = jnp.zeros_like(ds_scr)
            da_ref[...] = jnp.zeros_like(da_ref)
            ddt_ref[...] = jnp.zeros_like(ddt_ref)
            dng_ref[...] = jnp.zeros_like(dng_ref)

        s_list = [s_ref[0, h] for h in range(HEADS)]
        ba_list = [ba_ref[_pair_rows(ci), :] for ci in range(CHUNKS_PER_STEP)]
        _, vjp = jax.vjp(_gdn_chunk, _pair_slices(q_ref), _pair_slices(k_ref), _pair_slices(v_ref), ba_list,
                         _pair_slices(gz_ref), s_list, a_ref[...], dt_ref[...], ng_ref[...])
        ds_list = [ds_scr[h] for h in range(HEADS)]
        dq, dk, dv, dba, dgz, ds_in, da, ddt, dng = vjp((_pair_slices(do_ref), ds_list))
        for ci in range(CHUNKS_PER_STEP):
            for h in range(HEADS):
                p = ci * HEADS + h
                dqkv_ref[_pair_rows(ci), _pair_lanes(h)] = dq[p]
                dqkv_ref[_pair_rows(ci), _pair_lanes(h, D)] = dk[p]
                dqkv_ref[_pair_rows(ci), _pair_lanes(h, 2 * D)] = dv[p]
                dgz_ref[_pair_rows(ci), _pair_lanes(h)] = dgz[p].astype(BF16)
            dba_ref[_pair_rows(ci), :] = dba[ci].astype(BF16)
        for h in range(HEADS):
            ds_scr[h] = ds_in[h]
        da_ref[0:1, :] += da
        ddt_ref[0:1, :] += ddt
        dng_ref[0:1, :] += dng

    rev = lambda i: n_steps - 1 - i
    row = pl.BlockSpec((1, HEAD_DIM), lambda i: (0, 0))
    acc = pl.BlockSpec((8, HEAD_DIM), lambda i: (0, 0))
    outs = pl.pallas_call(
        body, name="gdn_chunk_bwd", grid=(n_steps,),
        in_specs=[pl.BlockSpec((rows, D), lambda i: (rev(i), 0)), pl.BlockSpec((rows, D), lambda i: (rev(i), 1)),
                  pl.BlockSpec((rows, D), lambda i: (rev(i), 2)),
                  pl.BlockSpec((rows, HEAD_DIM), lambda i: (rev(i), 0)),
                  pl.BlockSpec((rows, D), lambda i: (rev(i), 0)),
                  pl.BlockSpec((1, HEADS, HEAD_DIM, HEAD_DIM), lambda i: (rev(i), 0, 0, 0)),
                  pl.BlockSpec((rows, D), lambda i: (rev(i), 0)), row, row, row],
        out_specs=[pl.BlockSpec((rows, 3 * D), lambda i: (rev(i), 0)),
                   pl.BlockSpec((rows, HEAD_DIM), lambda i: (rev(i), 0)),
                   pl.BlockSpec((rows, D), lambda i: (rev(i), 0)), acc, acc, acc],
        out_shape=[jax.ShapeDtypeStruct((t, 3 * D), F32)]
        + [jax.ShapeDtypeStruct((t, HEAD_DIM), BF16), jax.ShapeDtypeStruct((t, D), BF16)]
        + [jax.ShapeDtypeStruct((8, HEAD_DIM), F32)] * 3,
        scratch_shapes=[pltpu.VMEM((HEADS, HEAD_DIM, HEAD_DIM), F32)],
        compiler_params=_cparams(("arbitrary",)),
    )(qkv_c, qkv_c, qkv_c, proj_ba, proj_gz, s_saved, do_gated, a_row, dt_row, ng_row)
    return outs


def _merge(a_out, o_gated, proj_gate, x, target, w_conf, w_gdn, w_o, pg, pb, tt=256):
    t = x.shape[0]

    def body(a_ref, o_ref, gt_ref, x_ref, y_ref, wc_ref, wg_ref, wo_ref, pg_ref, pb_ref,
             loss_ref, dpg_ref, dpb_ref, dx_ref, dgt_ref, da_ref, do_ref, h_ref, ds_ref, dyc_ref, dyg_ref):
        @pl.when(pl.program_id(0) == 0)
        def _():
            loss_ref[...] = jnp.zeros_like(loss_ref)
            dpg_ref[...] = jnp.zeros_like(dpg_ref)
            dpb_ref[...] = jnp.zeros_like(dpb_ref)

        wc, wg, wo = wc_ref[...], wg_ref[...], wo_ref[...]
        y_conf = _dot(a_ref[...], wc, _NN)
        y_gdn = _dot(o_ref[...], wg, _NN)
        sc = _sigmoid(gt_ref[:, 0:D])
        sg = _sigmoid(gt_ref[:, D:2 * D])
        h = sc * y_conf + sg * y_gdn
        z = DN_ALPHA * x_ref[...] + _dot(h, wo, _NN)
        mu = jnp.mean(z, axis=-1, keepdims=True)
        zc = z - mu
        rstd = lax.rsqrt(jnp.mean(zc * zc, axis=-1, keepdims=True) + LN_EPS)
        xhat = zc * rstd
        gain = pg_ref[...]
        err = xhat * gain + pb_ref[...] - y_ref[...]
        tok = jnp.mean(err * err, axis=-1, keepdims=True)
        loss_ref[...] += 0.5 * jnp.sum(tok, axis=0, keepdims=True)

        dy = err * (1.0 / D)
        dpg_ref[0:1, :] += jnp.sum(dy * xhat, axis=0, keepdims=True)
        dpb_ref[0:1, :] += jnp.sum(dy, axis=0, keepdims=True)
        dxh = dy * gain
        dz = rstd * (dxh - jnp.mean(dxh, axis=-1, keepdims=True)
                     - xhat * jnp.mean(dxh * xhat, axis=-1, keepdims=True))
        dx_ref[...] = DN_ALPHA * dz
        dh = _dot(dz, wo, _NT)
        dyc = dh * sc
        dyg = dh * sg
        dgt_ref[:, 0:D] = (dh * y_conf * sc * (1.0 - sc)).astype(BF16)
        dgt_ref[:, D:2 * D] = (dh * y_gdn * sg * (1.0 - sg)).astype(BF16)
        da_ref[...] = _dot(dyc, wc, _NT)
        do_ref[...] = _dot(dyg, wg, _NT)
        h_ref[...] = h.astype(BF16)
        ds_ref[...] = dz.astype(BF16)
        dyc_ref[...] = dyc.astype(BF16)
        dyg_ref[...] = dyg.astype(BF16)

    tile = pl.BlockSpec((tt, D), lambda i: (i, 0))
    wide = pl.BlockSpec((tt, 2 * D), lambda i: (i, 0))
    mat = pl.BlockSpec((D, D), lambda i: (0, 0))
    row = pl.BlockSpec((1, D), lambda i: (0, 0))
    acc = pl.BlockSpec((8, D), lambda i: (0, 0))
    act = lambda dt: jax.ShapeDtypeStruct((t, D), dt)
    return pl.pallas_call(
        body, name="merge", grid=(t // tt,),
        in_specs=[tile, tile, wide, tile, tile, mat, mat, mat, row, row],
        out_specs=[pl.BlockSpec((8, 128), lambda i: (0, 0)), acc, acc, tile, wide, tile, tile, tile, tile, tile, tile],
        out_shape=[jax.ShapeDtypeStruct((8, 128), F32), jax.ShapeDtypeStruct((8, D), F32),
                   jax.ShapeDtypeStruct((8, D), F32), act(F32), jax.ShapeDtypeStruct((t, 2 * D), BF16),
                   act(F32), act(F32), act(BF16), act(BF16), act(BF16), act(BF16)],
        compiler_params=_cparams(("arbitrary",)),
    )(a_out, o_gated, proj_gate, x, target, w_conf, w_gdn, w_o, pg, pb)


def _mesh_place():
    x, y, c = lax.axis_index("x"), lax.axis_index("y"), lax.axis_index("c")
    return x, y, c


def _flat(px, py, pc):
    return 4 * px + 2 * py + pc


def _all_gather(shards):
    n = len(shards)

    def body(*refs):
        ins, outs = refs[:n], refs[n:2 * n]
        send_sems, recv_sems, local_sems = refs[2 * n:]
        x, y, c = _mesh_place()
        me, sibling = (x, y, c), (x, y, 1 - c)
        chips = [(1 - x, y), (x, 1 - y), (1 - x, 1 - y)]

        def copy(a, k, block, to, src=None):
            dst = outs[a].at[_flat(*block)]
            return pltpu.make_async_remote_copy(
                src_ref=dst if src is None else src, dst_ref=dst,
                send_sem=send_sems.at[a, k], recv_sem=recv_sems.at[a, k],
                device_id=to, device_id_type=pl.DeviceIdType.MESH)

        mine = [pltpu.make_async_copy(ins[a], outs[a].at[_flat(*me)], local_sems.at[a]) for a in range(n)]
        for cp in mine:
            cp.start()
        first = []
        for a in range(n):
            first.append(copy(a, 0, me, sibling, src=ins[a]))
            first += [copy(a, 1 + j, me, (*chip, c), src=ins[a]) for j, chip in enumerate(chips)]
        for cp in first:
            cp.start()
        passed = []
        for j, chip in enumerate(chips):
            for a in range(n):
                copy(a, 1 + j, (*chip, c), me).wait_recv()
                fwd = copy(a, 4 + j, (*chip, c), sibling)
                fwd.start()
                passed.append(fwd)
        for a in range(n):
            copy(a, 0, sibling, me).wait_recv()
            for j, chip in enumerate(chips):
                copy(a, 4 + j, (*chip, 1 - c), me).wait_recv()
        for cp in first + passed:
            cp.wait_send()
        for cp in mine:
            cp.wait()

    any_spec = pl.BlockSpec(memory_space=pl.ANY)
    return pl.pallas_call(
        body, name="all_gather_weights",
        in_specs=[any_spec] * n, out_specs=[any_spec] * n,
        out_shape=[jax.ShapeDtypeStruct((N_DEV,) + s.shape, s.dtype) for s in shards],
        scratch_shapes=[pltpu.SemaphoreType.DMA((n, 7)), pltpu.SemaphoreType.DMA((n, 7)),
                        pltpu.SemaphoreType.DMA((n,))],
    )(*shards)


def _exchange_and_grad_x(block_arrays, small, init, segments, tm=512, tk=1024):
    nb = len(block_arrays)
    ns = len(segments)
    m, k1 = init.shape
    widths = [min(tk, a.shape[1]) for a, _ in segments]
    counts = [a.shape[1] // wd for (a, _), wd in zip(segments, widths)]
    starts = [sum(counts[:s]) for s in range(ns)]
    n_j = sum(counts)
    n_i = m // tm

    def body(*refs):
        g_refs, s_ref, i_ref = refs[:nb], refs[nb], refs[nb + 1]
        seg_refs = refs[nb + 2:nb + 2 + 2 * ns]
        outs = refs[nb + 2 + 2 * ns:]
        land_refs, sall_ref, o_ref = outs[:nb], outs[nb], outs[nb + 1]
        send_sems, recv_sems, local_sems = outs[nb + 2:]
        i, j = pl.program_id(0), pl.program_id(1)

        def copies(with_arrivals):
            x, y, c = _mesh_place()
            me = _flat(x, y, c)
            mine = [pltpu.make_async_copy(g_refs[a].at[me], land_refs[a].at[me], local_sems.at[a]) for a in range(nb)]
            mine.append(pltpu.make_async_copy(s_ref, sall_ref.at[me], local_sems.at[nb]))
            sends, recvs = [], []
            for k in range(7):
                mask = k + 1
                px = 1 - x if mask & 4 else x
                py = 1 - y if mask & 2 else y
                pc = 1 - c if mask & 1 else c
                peer = _flat(px, py, pc)
                for a in range(nb + 1):
                    kw = dict(send_sem=send_sems.at[a, k], recv_sem=recv_sems.at[a, k],
                              device_id=(px, py, pc), device_id_type=pl.DeviceIdType.MESH)
                    src = g_refs[a].at[peer] if a < nb else s_ref
                    land = land_refs[a] if a < nb else sall_ref
                    sends.append(pltpu.make_async_remote_copy(src_ref=src, dst_ref=land.at[me], **kw))
                    if with_arrivals:
                        recvs.append(pltpu.make_async_remote_copy(src_ref=src, dst_ref=land.at[peer], **kw))
            return mine, sends, recvs

        @pl.when((i == 0) & (j == 0))
        def _():
            mine, sends, _ = copies(False)
            for cp in mine + sends:
                cp.start()

        @pl.when(j == 0)
        def _():
            o_ref[...] = i_ref[...]

        for s in range(ns):
            @pl.when((j >= starts[s]) & (j < starts[s] + counts[s]))
            def _(s=s):
                o_ref[...] += lax.dot_general(seg_refs[2 * s][...], seg_refs[2 * s + 1][...], (_NT, ((), ())),
                                              preferred_element_type=F32)

        @pl.when((i == n_i - 1) & (j == n_j - 1))
        def _():
            mine, sends, recvs = copies(True)
            for cp in recvs:
                cp.wait_recv()
            for cp in sends:
                cp.wait_send()
            for cp in mine:
                cp.wait()

    any_spec = pl.BlockSpec(memory_space=pl.ANY)
    seg_specs = []
    for s in range(ns):
        col = lambda i, j, s=s: jnp.clip(j - starts[s], 0, counts[s] - 1)
        seg_specs.append(pl.BlockSpec((tm, widths[s]), lambda i, j, col=col: (i, col(i, j))))
        seg_specs.append(pl.BlockSpec((k1, widths[s]), lambda i, j, col=col: (0, col(i, j))))
    tile = pl.BlockSpec((tm, k1), lambda i, j: (i, 0))
    outs = pl.pallas_call(
        body, name="exchange_grads_and_grad_x", grid=(n_i, n_j),
        in_specs=[any_spec] * (nb + 1) + [tile] + seg_specs,
        out_specs=[any_spec] * (nb + 1) + [tile],
        out_shape=[jax.ShapeDtypeStruct(b.shape, b.dtype) for b in block_arrays]
        + [jax.ShapeDtypeStruct((N_DEV,) + small.shape, small.dtype), jax.ShapeDtypeStruct((m, k1), F32)],
        scratch_shapes=[pltpu.SemaphoreType.DMA((nb + 1, 7)), pltpu.SemaphoreType.DMA((nb + 1, 7)),
                        pltpu.SemaphoreType.DMA((nb + 1,))],
        compiler_params=_cparams(("arbitrary", "arbitrary")),
    )(*block_arrays, small, init, *[r for seg in segments for r in seg])
    return outs[:nb], outs[nb], outs[nb + 1]


def _adamw(parts, w, m, v, name, tile):
    rows, cols = w.shape

    def body(p_ref, w_ref, m_ref, v_ref, g_ref, d_ref, nm_ref, nv_ref):
        g = p_ref[0].astype(F32)
        for s in range(1, N_DEV):
            g = g + p_ref[s].astype(F32)
        nm = ADAM_B1 * m_ref[...] + (1.0 - ADAM_B1) * g
        nv = ADAM_B2 * v_ref[...] + (1.0 - ADAM_B2) * jnp.square(g)
        m_hat = nm / (1.0 - ADAM_B1 ** ADAM_STEP)
        v_hat = nv / (1.0 - ADAM_B2 ** ADAM_STEP)
        g_ref[...] = g
        d_ref[...] = -ADAM_LR * (m_hat / (jnp.sqrt(v_hat) + ADAM_EPS) + ADAM_WD * w_ref[...])
        nm_ref[...] = nm
        nv_ref[...] = nv

    blk = pl.BlockSpec((tile, cols), lambda i: (i, 0))
    out = jax.ShapeDtypeStruct((rows, cols), F32)
    return pl.pallas_call(
        body, name=name, grid=(rows // tile,),
        in_specs=[pl.BlockSpec((N_DEV, tile, cols), lambda i: (0, i, 0)), blk, blk, blk],
        out_specs=[blk, blk, blk, blk], out_shape=[out, out, out, out],
        compiler_params=_cparams(("parallel",)),
    )(parts, w, m, v)


def _rows_of(flat, n_rows):
    flat = flat.reshape(-1)
    return jnp.pad(flat, (0, n_rows * D - flat.shape[0])).reshape(n_rows, D)


def _pack_shards(conf_w_out, gdn_w_out, w_o, conf_dw_w, gdn_conv_w):
    return jnp.concatenate([conf_w_out, gdn_w_out, w_o, _rows_of(conf_dw_w, 16), _rows_of(gdn_conv_w, 16)], axis=0)


def _unpack_shards(p):
    dw = p[ROW_DW:ROW_DW + 4].reshape(-1)[:K_CONF * 128].reshape(K_CONF, 128)
    gc = p[ROW_GC:ROW_GC + 2].reshape(-1)[:K_GDN * 384].reshape(K_GDN, 384)
    return p[ROW_CWO:ROW_CWO + 128], p[ROW_GWO:ROW_GWO + 128], p[ROW_WO:ROW_WO + 128], dw, gc


def _pack_small(dw_b, ln_g, ln_b, pg, pb, ng, a_log, dt_bias, loss=None):
    s = jnp.zeros((SMALL_ROWS, D), F32)
    for r, val in enumerate((dw_b, ln_g, ln_b, pg, pb, ng, a_log, dt_bias)):
        s = s.at[r, :val.shape[0]].set(val)
    if loss is not None:
        s = s.at[8, 0].set(loss)
    return s


def _unpack_small(s):
    return (s[0], s[1], s[2], s[3], s[4], s[5, :HEAD_DIM], s[6, :HEADS], s[7, :HEADS])


def _scatter_blocks(g_cwo, g_gwo, g_wo, g_dw, g_gc):
    dw = g_dw.reshape(K_CONF, N_DEV, 128).transpose(1, 0, 2).reshape(N_DEV, K_CONF * 128)
    dw = jnp.pad(dw, ((0, 0), (0, 16 * D - K_CONF * 128))).reshape(N_DEV, 16, D)
    gc = g_gc.reshape(K_GDN, N_DEV, 384).transpose(1, 0, 2).reshape(N_DEV, K_GDN * 384)
    gc = jnp.pad(gc, ((0, 0), (0, 16 * D - K_GDN * 384))).reshape(N_DEV, 16, D)
    return jnp.concatenate([g_cwo.reshape(N_DEV, 128, D), g_gwo.reshape(N_DEV, 128, D),
                            g_wo.reshape(N_DEV, 128, D), dw, gc], axis=1)


def kernel(x, w_in, conf_dw_w, conf_dw_b, conf_ln_g, conf_ln_b, conf_w_out, gdn_conv_w, gdn_A_log, gdn_dt_bias, gdn_norm_g, gdn_w_out, w_o, post_ln_g, post_ln_b, loss_target, m_w_in, m_conf_dw_w, m_conf_dw_b, m_conf_ln_g, m_conf_ln_b, m_conf_w_out, m_gdn_conv_w, m_gdn_A_log, m_gdn_dt_bias, m_gdn_norm_g, m_gdn_w_out, m_w_o, m_post_ln_g, m_post_ln_b, v_w_in, v_conf_dw_w, v_conf_dw_b, v_conf_ln_g, v_conf_ln_b, v_conf_w_out, v_gdn_conv_w, v_gdn_A_log, v_gdn_dt_bias, v_gdn_norm_g, v_gdn_w_out, v_w_o, v_post_ln_g, v_post_ln_b):
    t = x.shape[1]
    x2 = x.reshape(t, D)
    target = loss_target.reshape(t, D)
    x_bf = x2.astype(BF16)

    w_pack = _pack_shards(conf_w_out, gdn_w_out, w_o, conf_dw_w, gdn_conv_w)
    convw = jnp.concatenate([_rows_of(conf_dw_w, 8), _rows_of(gdn_conv_w, 8)], axis=0)
    all_w_in, all_w, all_convw = _all_gather([w_in.astype(BF16), w_pack.astype(BF16), convw])
    w_full = all_w_in.transpose(1, 0, 2).reshape(D, W_IN_COLS)
    w_conf = w_full[:, 0:3 * D]
    w_qkv = w_full[:, 3 * D:6 * D]
    w_gz = w_full[:, 6 * D:7 * D]
    w_ba = jnp.pad(w_full[:, 7 * D:7 * D + 2 * HEADS], ((0, 0), (0, HEAD_DIM - 2 * HEADS)))
    w_gate = w_full[:, 7 * D + 2 * HEADS:]
    cwo_full = all_w[:, ROW_CWO:ROW_CWO + 128].reshape(D, D)
    gwo_full = all_w[:, ROW_GWO:ROW_GWO + 128].reshape(D, D)
    wo_full = all_w[:, ROW_WO:ROW_WO + 128].reshape(D, D)
    dw_full = all_convw[:, 0:4].reshape(N_DEV, 4 * D)[:, :K_CONF * 128].reshape(N_DEV, K_CONF, 128)
    dw_full = jnp.pad(dw_full.transpose(1, 0, 2).reshape(K_CONF, D), ((0, 32 - K_CONF), (0, 0)))
    gc_full = all_convw[:, 8:10].reshape(N_DEV, 2 * D)[:, :K_GDN * 384].reshape(N_DEV, K_GDN, 384)
    gc_full = jnp.pad(gc_full.transpose(1, 0, 2).reshape(K_GDN, 3 * D), ((0, 8 - K_GDN), (0, 0)))

    row = lambda vec: vec.reshape(1, -1)
    lane_row = lambda vec, at: jnp.zeros((1, HEAD_DIM), F32).at[0, at:at + vec.shape[0]].set(vec)
    a_row = lane_row(gdn_A_log, HEADS)
    dt_row = lane_row(gdn_dt_bias, HEADS)
    ng_row = row(gdn_norm_g)

    proj_conf = _matmul_nn(x_bf, w_conf, "proj_conf")
    proj_qkv = _matmul_nn(x_bf, w_qkv, "proj_qkv")
    proj_gz = _matmul_nn(x_bf, w_gz, "proj_gz")
    proj_gate = _matmul_nn(x_bf, w_gate, "proj_gate")
    proj_ba = _matmul_nn(x_bf, w_ba, "proj_ba")
    cpre, a_out = _conf_fwd(proj_conf, dw_full, row(conf_dw_b), row(conf_ln_g), row(conf_ln_b))
    qkv_c = _gdn_conv_fwd(proj_qkv, gc_full)
    o_gated, s_saved = _gdn_chunk_fwd(qkv_c, proj_ba, proj_gz, a_row, dt_row, ng_row)

    (loss_acc, d_pg, d_pb, dx, d_gate, da_out, do_gated, h_bf, dsub_bf, dyc_bf, dyg_bf) = _merge(
        a_out, o_gated, proj_gate, x2, target, cwo_full, gwo_full, wo_full, row(post_ln_g), row(post_ln_b))
    g_wo = _matmul_tn(h_bf, dsub_bf, "grad_w_o")
    g_cwo = _matmul_tn(a_out, dyc_bf, "grad_conf_w_out")
    g_gwo = _matmul_tn(o_gated, dyg_bf, "grad_gdn_w_out")

    dqkv_c, d_ba, d_gz, d_a_row, d_dt_row, d_ng_row = _gdn_chunk_bwd(
        qkv_c, proj_ba, proj_gz, s_saved, do_gated, a_row, dt_row, ng_row)
    d_qkv, g_gc = _gdn_conv_bwd(dqkv_c, proj_qkv, gc_full)

    dcpre, dcz, d_ln_g, d_ln_b = _conf_bwd_post(cpre, proj_conf, da_out, row(conf_ln_g), row(conf_ln_b))
    d_conf, g_dw, g_dwb = _conf_bwd_conv(dcpre, proj_conf, dcz, dw_full)

    segments = [(d_conf, w_conf, "conf"), (d_qkv, w_qkv, "qkv"), (d_gz, w_gz, "gz"),
                (d_ba, w_ba, "ba"), (d_gate, w_gate, "gate")]
    g_cols = {tag: _matmul_tn(x_bf, d_seg, "grad_w_in_" + tag) for d_seg, _, tag in segments}
    g_w_in = jnp.concatenate([g_cols["conf"], g_cols["qkv"], g_cols["gz"], g_cols["ba"][:, :2 * HEADS],
                              g_cols["gate"]], axis=1)

    w_in_blocks = g_w_in.reshape(D, N_DEV, W_IN_SHARD).transpose(1, 0, 2).astype(BF16)
    blocks = _scatter_blocks(g_cwo, g_gwo, g_wo, g_dw[:K_CONF], g_gc[:K_GDN]).astype(BF16)
    small = _pack_small(g_dwb[0], d_ln_g[0], d_ln_b[0], d_pg[0], d_pb[0], d_ng_row[0],
                        d_a_row[0, HEADS:2 * HEADS], d_dt_row[0, HEADS:2 * HEADS], loss_acc[0, 0])
    (landed_w_in, landed), small_all, dx = _exchange_and_grad_x(
        [w_in_blocks, blocks], small, dx, [(d_seg, w_seg) for d_seg, w_seg, _ in segments])

    m_pack = _pack_shards(m_conf_w_out, m_gdn_w_out, m_w_o, m_conf_dw_w, m_gdn_conv_w)
    v_pack = _pack_shards(v_conf_w_out, v_gdn_w_out, v_w_o, v_conf_dw_w, v_gdn_conv_w)
    big_w_in = _adamw(landed_w_in, w_in, m_w_in, v_w_in, "adamw_w_in", W_IN_TILE)
    big = _adamw(landed, w_pack, m_pack, v_pack, "adamw_shards", PACK_TILE)
    ws = _pack_small(conf_dw_b, conf_ln_g, conf_ln_b, post_ln_g, post_ln_b, gdn_norm_g, gdn_A_log, gdn_dt_bias)
    ms = _pack_small(m_conf_dw_b, m_conf_ln_g, m_conf_ln_b, m_post_ln_g, m_post_ln_b, m_gdn_norm_g, m_gdn_A_log,
                     m_gdn_dt_bias)
    vs = _pack_small(v_conf_dw_b, v_conf_ln_g, v_conf_ln_b, v_post_ln_g, v_post_ln_b, v_gdn_norm_g, v_gdn_A_log,
                     v_gdn_dt_bias)
    sml = _adamw(small_all, ws, ms, vs, "adamw_replicated", SMALL_ROWS)

    loss = sml[0][8, 0]
    outs = []
    for b_w_in, big_k, sml_k in zip(big_w_in, big, sml):
        b_cwo, b_gwo, b_wo, b_dw, b_gc = _unpack_shards(big_k)
        s_dwb, s_lng, s_lnb, s_pg, s_pb, s_ng, s_a, s_dt = _unpack_small(sml_k)
        outs.append([b_w_in, b_dw, s_dwb, s_lng, s_lnb, b_cwo, b_gc, s_a, s_dt, s_ng, b_gwo, b_wo, s_pg, s_pb])
    return (loss, dx.reshape(1, t, D), *outs[0], *outs[1], *outs[2], *outs[3])
```

```python
import functools

import jax
import jax.numpy as jnp
from jax import lax
from jax.experimental import pallas as pl
from jax.experimental.pallas import tpu as pltpu

F32 = jnp.float32
BF16 = jnp.bfloat16

N_DEV = 8
D = 1024
HEADS = 8
HEAD_DIM = 128
CHUNK = 64
CHUNKS_PER_STEP = 4
K_CONF = 31
K_GDN = 4
HALO_CONF = 32
HALO_GDN = 8
LN_EPS = 1e-5
RMS_EPS = 1e-6
L2_EPS = 1e-6
DN_ALPHA = 2.0 ** 0.25
ADAM_LR = 0.001
ADAM_B1 = 0.9
ADAM_B2 = 0.999
ADAM_EPS = 1e-08
ADAM_WD = 0.01
ADAM_STEP = 10

W_IN_COLS = 9232
W_IN_SHARD = W_IN_COLS // N_DEV
ROW_CWO = 0
ROW_GWO = ROW_CWO + 128
ROW_WO = ROW_GWO + 128
ROW_DW = ROW_WO + 128
ROW_GC = ROW_DW + 16
PACK_ROWS = ROW_GC + 16
PACK_TILE = PACK_ROWS // 2
W_IN_TILE = 128
SMALL_ROWS = 16
CONVW_ROWS = 16

VMEM_LIMIT = 56 * 1024 * 1024

_NN = ((1,), (0,))
_NT = ((1,), (1,))
_TN = ((0,), (0,))


def _cparams(sem=None):
    return pltpu.CompilerParams(dimension_semantics=sem, vmem_limit_bytes=VMEM_LIMIT)


def _dot(a, b, dims, hi=False):
    dn = (dims, ((), ()))
    a_hi = a.astype(BF16)
    b_hi = b.astype(BF16)
    if not hi:
        return lax.dot_general(a_hi, b_hi, dn, preferred_element_type=F32)
    a_lo = (a - a_hi.astype(F32)).astype(BF16)
    b_lo = (b - b_hi.astype(F32)).astype(BF16)
    d = lambda p, q: lax.dot_general(p, q, dn, preferred_element_type=F32)
    return d(a_hi, b_hi) + (d(a_hi, b_lo) + d(a_lo, b_hi))


def _make_mm(kind, hi):
    dims = {"nn": _NN, "nt": _NT, "tn": _TN}[kind]

    @jax.custom_vjp
    def mm(a, b):
        return _dot(a, b, dims, hi)

    def fwd(a, b):
        return _dot(a, b, dims, hi), (a, b)

    def bwd(res, g):
        a, b = res
        if kind == "nn":
            return _dot(g, b, _NT, hi), _dot(a, g, _TN, hi)
        if kind == "nt":
            return _dot(g, b, _NN, hi), _dot(g, a, _TN, hi)
        return _dot(b, g, _NT, hi), _dot(a, g, _NN, hi)

    mm.defvjp(fwd, bwd)
    return mm


_mm_nn = _make_mm("nn", False)
_mm_nt = _make_mm("nt", False)
_mm_tn = _make_mm("tn", False)
_mm_nn_hi = _make_mm("nn", True)
_mm_tn_hi = _make_mm("tn", True)


def _tri_inv_impl(lows):
    c = lows[0].shape[0]
    eye = (lax.broadcasted_iota(jnp.int32, (c, c), 0) == lax.broadcasted_iota(jnp.int32, (c, c), 1)).astype(F32)
    ms = [-low for low in lows]
    ps = [eye + m for m in ms]
    steps = max(c.bit_length() - 2, 0)
    for _ in range(steps):
        ms = [_dot(m, m, _NN) for m in ms]
        ps = [p + _dot(p, m, _NN) for p, m in zip(ps, ms)]
    rs = [eye - p - _dot(low, p, _NN, True) for low, p in zip(lows, ps)]
    return [p + _dot(p, r, _NN, True) for p, r in zip(ps, rs)]


@jax.custom_vjp
def _tri_inv(lows):
    return _tri_inv_impl(lows)


def _tri_inv_fwd(lows):
    xs = _tri_inv_impl(lows)
    return xs, xs


def _tri_inv_bwd(xs, dxs):
    ts = [_dot(x, dx, _TN) for x, dx in zip(xs, dxs)]
    return ([-_dot(t, x, _NT) for t, x in zip(ts, xs)],)


_tri_inv.defvjp(_tri_inv_fwd, _tri_inv_bwd)


def _sigmoid(x):
    return jax.nn.sigmoid(x)


def _silu(x):
    return x * jax.nn.sigmoid(x)


def _softplus(x):
    u = jnp.exp(-jnp.abs(x))
    log1p_u = jnp.where(u < 1e-3, u * (1.0 - u * (0.5 - u * (1.0 / 3.0))), jnp.log(1.0 + u))
    return jnp.maximum(x, 0.0) + log1p_u


def _layernorm(x, g, b):
    mu = jnp.mean(x, axis=-1, keepdims=True)
    xc = x - mu
    var = jnp.mean(xc * xc, axis=-1, keepdims=True)
    return xc * lax.rsqrt(var + LN_EPS) * g + b


def _pick_lane(x, lane):
    idx = lax.broadcasted_iota(jnp.int32, x.shape, 1)
    return jnp.sum(jnp.where(idx == lane, x, 0.0), axis=1, keepdims=True)


def _gdn_chunk(q_list, k_list, v_list, ba_list, gz_list, s_list, a_row, dt_row, ng_row):
    c = ba_list[0].shape[0]
    n_chunks = len(ba_list)
    pairs = [(ci, h) for ci in range(n_chunks) for h in range(HEADS)]
    every = range(len(pairs))
    rows = lax.broadcasted_iota(jnp.int32, (c, c), 0)
    cols = lax.broadcasted_iota(jnp.int32, (c, c), 1)
    causal = rows >= cols
    strict = rows > cols
    tril = causal.astype(F32)
    triu = (rows <= cols).astype(F32)
    last_row = lax.broadcasted_iota(jnp.int32, (c, 1), 0) == c - 1
    sub8 = lax.broadcasted_iota(jnp.int32, (HEADS, c), 0)

    beta_all = [_sigmoid(ba) for ba in ba_list]
    g_all = [-jnp.exp(a_row) * _softplus(ba + dt_row) for ba in ba_list]
    gc_all = [_mm_nn_hi(tril, g) for g in g_all]
    gc_t = [_mm_tn_hi(g, triu)[HEADS:2 * HEADS, :] for g in g_all]

    q = [_silu(a) for a in q_list]
    k = [_silu(a) for a in k_list]
    v = [_silu(a) for a in v_list]
    q = [a * lax.rsqrt(jnp.sum(a * a, axis=-1, keepdims=True) + L2_EPS) * (HEAD_DIM ** -0.5) for a in q]
    k = [a * lax.rsqrt(jnp.sum(a * a, axis=-1, keepdims=True) + L2_EPS) for a in k]
    beta = [_pick_lane(beta_all[ci], h) for ci, h in pairs]
    gc = [_pick_lane(gc_all[ci], HEADS + h) for ci, h in pairs]
    gc_cols = [jnp.sum(jnp.where(sub8 == h, gc_t[ci], 0.0), axis=0, keepdims=True) for ci, h in pairs]
    decay = [jnp.where(causal, jnp.exp(jnp.where(causal, gc[p] - gc_cols[p], 0.0)), 0.0) for p in every]
    kb = [k[p] * beta[p] for p in every]
    low = [jnp.where(strict, _mm_nt(kb[p], k[p]) * decay[p], 0.0) for p in every]
    x = _tri_inv(low)
    eg = [jnp.exp(gc[p]) for p in every]
    u = [_mm_nn(x[p], v[p] * beta[p]) for p in every]
    w = [_mm_nn(x[p], kb[p] * eg[p]) for p in every]
    intra = [_mm_nt(q[p], k[p]) * decay[p] for p in every]
    q_dec = [q[p] * eg[p] for p in every]
    g_last = [jnp.sum(jnp.where(last_row, gc[p], 0.0), axis=0, keepdims=True) for p in every]
    k_dec = [k[p] * jnp.exp(g_last[p] - gc[p]) for p in every]
    s_dec = [jnp.exp(g_last[p]) for p in every]

    o = []
    state = list(s_list)
    for ci in range(n_chunks):
        at = [ci * HEADS + h for h in range(HEADS)]
        v_new = [u[p] - _mm_nn(w[p], state[h]) for h, p in enumerate(at)]
        o += [_mm_nn(q_dec[p], state[h]) + _mm_nn(intra[p], v_new[h]) for h, p in enumerate(at)]
        state = [state[h] * s_dec[p] + _mm_tn(k_dec[p], v_new[h]) for h, p in enumerate(at)]
    o = [a * lax.rsqrt(jnp.mean(a * a, axis=-1, keepdims=True) + RMS_EPS) * ng_row for a in o]
    o = [o[p] * _silu(gz_list[p]) for p in every]
    return o, state


def _conf_post(cpre, cz, g, b):
    return _silu(_layernorm(cpre, g, b)) * _silu(cz)


def _matmul_nn(a, b, name, tm=512, tn=1024):
    m, k = a.shape
    n = b.shape[1]
    tn = min(tn, n)

    def body(a_ref, b_ref, o_ref):
        o_ref[...] = jnp.dot(a_ref[...], b_ref[...], preferred_element_type=F32)

    return pl.pallas_call(
        body, name=name, grid=(n // tn, m // tm),
        in_specs=[pl.BlockSpec((tm, k), lambda j, i: (i, 0)), pl.BlockSpec((k, tn), lambda j, i: (0, j))],
        out_specs=pl.BlockSpec((tm, tn), lambda j, i: (i, j)),
        out_shape=jax.ShapeDtypeStruct((m, n), F32),
        compiler_params=_cparams(("parallel", "parallel")),
    )(a, b)


def _matmul_tn(a, b, name, tt=512, tn=1024):
    t, k1 = a.shape
    n = b.shape[1]
    tn = min(tn, n)
    n_t = t // tt

    def body(a_ref, b_ref, o_ref, acc_ref):
        @pl.when(pl.program_id(1) == 0)
        def _():
            acc_ref[...] = jnp.zeros_like(acc_ref)

        acc_ref[...] += lax.dot_general(a_ref[...], b_ref[...], (_TN, ((), ())), preferred_element_type=F32)

        @pl.when(pl.program_id(1) == n_t - 1)
        def _():
            o_ref[...] = acc_ref[...].astype(BF16)

    return pl.pallas_call(
        body, name=name, grid=(n // tn, n_t),
        in_specs=[pl.BlockSpec((tt, k1), lambda j, i: (i, 0)), pl.BlockSpec((tt, tn), lambda j, i: (i, j))],
        out_specs=pl.BlockSpec((k1, tn), lambda j, i: (0, j)),
        out_shape=jax.ShapeDtypeStruct((k1, n), BF16),
        scratch_shapes=[pltpu.VMEM((k1, tn), F32)],
        compiler_params=_cparams(("parallel", "arbitrary")),
    )(a, b)


def _build_bank(bank_ref, shifts):
    ext = bank_ref[0]
    rows = ext.shape[0]
    for s in shifts:
        if s:
            bank_ref[s] = pltpu.roll(ext, rows - s, axis=0)


def _conv_taps(bank_ref, w_ref, offsets, n_rows, width, emit):
    def piece(rc, carry):
        r0 = pl.multiple_of(rc * 16, 16)
        for cb in range(width // 128):
            lanes = slice(cb * 128, (cb + 1) * 128)
            acc = jnp.zeros((16, 128), F32)
            for k, off in enumerate(offsets):
                m, s = divmod(off, 8)
                acc = acc + bank_ref[s, pl.ds(r0 + 8 * m, 16), lanes] * w_ref[k:k + 1, lanes]
            emit(r0, lanes, acc)
        return carry

    lax.fori_loop(0, n_rows // 16, piece, 0)


def _conv_dw(bank_ref, d_ref, offsets, n_rows, width, emit):
    ms = [divmod(off, 8) for off in offsets]
    n_taps = len(offsets)
    group = max(1, 32 // n_taps)
    blocks = [slice(cb * 128, (cb + 1) * 128) for cb in range(width // 128)]
    for g0 in range(0, len(blocks), group):
        lane_group = blocks[g0:g0 + group]

        def piece(rc, accs, lane_group=lane_group):
            r0 = pl.multiple_of(rc * 8, 8)
            out = []
            for b, lanes in enumerate(lane_group):
                d = d_ref[pl.ds(r0, 8), lanes]
                out += [accs[b * n_taps + k] + d * bank_ref[s, pl.ds(r0 + 8 * m, 8), lanes]
                        for k, (m, s) in enumerate(ms)]
            return tuple(out)

        init = tuple(jnp.zeros((8, 128), F32) for _ in range(len(lane_group) * n_taps))
        accs = lax.fori_loop(0, n_rows // 8, piece, init)
        for b, lanes in enumerate(lane_group):
            for k in range(n_taps):
                emit(k, lanes, jnp.sum(accs[b * n_taps + k], axis=0, keepdims=True))


def _conf_fwd(proj_conf, dw_w, dw_b, ln_g, ln_b, tt=256):
    t = proj_conf.shape[0]
    hb = tt // HALO_CONF
    offsets = [HALO_CONF - (K_CONF - 1) + k for k in range(K_CONF)]

    def body(cv_ref, cg_ref, cz_ref, cvh_ref, cgh_ref, w_ref, b_ref, g_ref, bb_ref, cpre_ref, aout_ref, bank_ref):
        first = pl.program_id(0) == 0
        halo = cvh_ref[...] * _sigmoid(cgh_ref[...])
        bank_ref[0, 0:HALO_CONF, :] = jnp.where(first, 0.0, halo)
        bank_ref[0, HALO_CONF:, :] = cv_ref[...] * _sigmoid(cg_ref[...])
        _build_bank(bank_ref, range(8))

        def emit(r0, lanes, acc):
            cpre_ref[pl.ds(r0, 16), lanes] = acc + b_ref[0:1, lanes]

        _conv_taps(bank_ref, w_ref, offsets, tt, D, emit)
        aout_ref[...] = _conf_post(cpre_ref[...], cz_ref[...], g_ref[...], bb_ref[...]).astype(BF16)

    row = pl.BlockSpec((1, D), lambda i: (0, 0))
    return pl.pallas_call(
        body, name="conf_fwd", grid=(t // tt,),
        in_specs=[pl.BlockSpec((tt, D), lambda i: (i, 0)), pl.BlockSpec((tt, D), lambda i: (i, 1)),
                  pl.BlockSpec((tt, D), lambda i: (i, 2)),
                  pl.BlockSpec((HALO_CONF, D), lambda i: (jnp.maximum(i * hb - 1, 0), 0)),
                  pl.BlockSpec((HALO_CONF, D), lambda i: (jnp.maximum(i * hb - 1, 0), 1)),
                  pl.BlockSpec((32, D), lambda i: (0, 0)), row, row, row],
        out_specs=[pl.BlockSpec((tt, D), lambda i: (i, 0)), pl.BlockSpec((tt, D), lambda i: (i, 0))],
        out_shape=[jax.ShapeDtypeStruct((t, D), F32), jax.ShapeDtypeStruct((t, D), BF16)],
        scratch_shapes=[pltpu.VMEM((8, tt + HALO_CONF, D), F32)],
        compiler_params=_cparams(("parallel",)),
    )(proj_conf, proj_conf, proj_conf, proj_conf, proj_conf, dw_w, dw_b, ln_g, ln_b)


def _conf_bwd_post(cpre, proj_conf, da_out, ln_g, ln_b, tt=256):
    t = cpre.shape[0]

    def body(c_ref, z_ref, da_ref, g_ref, b_ref, dc_ref, dz_ref, dg_ref, db_ref):
        @pl.when(pl.program_id(0) == 0)
        def _():
            dg_ref[...] = jnp.zeros_like(dg_ref)
            db_ref[...] = jnp.zeros_like(db_ref)

        _, vjp = jax.vjp(_conf_post, c_ref[...], z_ref[...], g_ref[...], b_ref[...])
        dc, dz, dg, db = vjp(da_ref[...])
        dc_ref[...] = dc
        dz_ref[...] = dz.astype(BF16)
        dg_ref[0:1, :] += dg
        db_ref[0:1, :] += db

    row = pl.BlockSpec((1, D), lambda i: (0, 0))
    acc = pl.BlockSpec((8, D), lambda i: (0, 0))
    return pl.pallas_call(
        body, name="conf_bwd_post", grid=(t // tt,),
        in_specs=[pl.BlockSpec((tt, D), lambda i: (i, 0)), pl.BlockSpec((tt, D), lambda i: (i, 2)),
                  pl.BlockSpec((tt, D), lambda i: (i, 0)), row, row],
        out_specs=[pl.BlockSpec((tt, D), lambda i: (i, 0)), pl.BlockSpec((tt, D), lambda i: (i, 0)), acc, acc],
        out_shape=[jax.ShapeDtypeStruct((t, D), F32), jax.ShapeDtypeStruct((t, D), BF16),
                   jax.ShapeDtypeStruct((8, D), F32), jax.ShapeDtypeStruct((8, D), F32)],
        compiler_params=_cparams(("arbitrary",)),
    )(cpre, proj_conf, da_out, ln_g, ln_b)


def _conf_bwd_conv(dcpre, proj_conf, dcz, dw_w, tt=256):
    t = dcpre.shape[0]
    n_tiles = t // tt
    hb = tt // HALO_CONF
    n_hb = t // HALO_CONF
    offsets = [K_CONF - 1 - k for k in range(K_CONF)]

    def body(d_ref, dn_ref, cv_ref, cg_ref, dz_ref, w_ref, dp_ref, dw_ref, db_ref, bank_d, a_scr, da_scr):
        i = pl.program_id(0)

        @pl.when(i == 0)
        def _():
            dw_ref[...] = jnp.zeros_like(dw_ref)
            db_ref[...] = jnp.zeros_like(db_ref)

        cv = cv_ref[...]
        sg = _sigmoid(cg_ref[...])
        a_scr[...] = cv * sg
        bank_d[0, 0:tt, :] = d_ref[...]
        bank_d[0, tt:, :] = jnp.where(i == n_tiles - 1, 0.0, dn_ref[...])
        _build_bank(bank_d, range(8))

        def emit_da(r0, lanes, acc):
            da_scr[pl.ds(r0, 16), lanes] = acc

        _conv_taps(bank_d, w_ref, offsets, tt, D, emit_da)
        da = da_scr[...]
        dp_ref[:, 0:D] = (da * sg).astype(BF16)
        dp_ref[:, D:2 * D] = (da * cv * sg * (1.0 - sg)).astype(BF16)
        dp_ref[:, 2 * D:3 * D] = dz_ref[...]

        def emit_dw(k, lanes, row):
            dw_ref[k:k + 1, lanes] += row

        _conv_dw(bank_d, a_scr, offsets, tt, D, emit_dw)
        db_ref[0:1, :] += jnp.sum(d_ref[...], axis=0, keepdims=True)

    nxt = lambda i: jnp.minimum((i + 1) * hb, n_hb - 1)
    return pl.pallas_call(
        body, name="conf_bwd_conv", grid=(n_tiles,),
        in_specs=[pl.BlockSpec((tt, D), lambda i: (i, 0)), pl.BlockSpec((HALO_CONF, D), lambda i: (nxt(i), 0)),
                  pl.BlockSpec((tt, D), lambda i: (i, 0)), pl.BlockSpec((tt, D), lambda i: (i, 1)),
                  pl.BlockSpec((tt, D), lambda i: (i, 0)), pl.BlockSpec((32, D), lambda i: (0, 0))],
        out_specs=[pl.BlockSpec((tt, 3 * D), lambda i: (i, 0)), pl.BlockSpec((32, D), lambda i: (0, 0)),
                   pl.BlockSpec((8, D), lambda i: (0, 0))],
        out_shape=[jax.ShapeDtypeStruct((t, 3 * D), BF16), jax.ShapeDtypeStruct((32, D), F32),
                   jax.ShapeDtypeStruct((8, D), F32)],
        scratch_shapes=[pltpu.VMEM((8, tt + HALO_CONF, D), F32), pltpu.VMEM((tt, D), F32), pltpu.VMEM((tt, D), F32)],
        compiler_params=_cparams(("arbitrary",)),
    )(dcpre, dcpre, proj_conf, proj_conf, dcz, dw_w)


def _gdn_conv_fwd(proj_qkv, conv_w, tt=256):
    t, width = proj_qkv.shape
    hb = tt // HALO_GDN
    offsets = [HALO_GDN - (K_GDN - 1) + k for k in range(K_GDN)]
    shifts = sorted({off % 8 for off in offsets})

    def body(x_ref, xh_ref, w_ref, o_ref, bank_ref):
        bank_ref[0, 0:HALO_GDN, :] = jnp.where(pl.program_id(1) == 0, 0.0, xh_ref[...])
        bank_ref[0, HALO_GDN:, :] = x_ref[...]
        _build_bank(bank_ref, shifts)

        def emit(r0, lanes, acc):
            o_ref[pl.ds(r0, 16), lanes] = acc

        _conv_taps(bank_ref, w_ref, offsets, tt, D, emit)

    return pl.pallas_call(
        body, name="gdn_conv_fwd", grid=(width // D, t // tt),
        in_specs=[pl.BlockSpec((tt, D), lambda j, i: (i, j)),
                  pl.BlockSpec((HALO_GDN, D), lambda j, i: (jnp.maximum(i * hb - 1, 0), j)),
                  pl.BlockSpec((8, D), lambda j, i: (0, j))],
        out_specs=pl.BlockSpec((tt, D), lambda j, i: (i, j)),
        out_shape=jax.ShapeDtypeStruct((t, width), F32),
        scratch_shapes=[pltpu.VMEM((8, tt + HALO_GDN, D), F32)],
        compiler_params=_cparams(("parallel", "parallel")),
    )(proj_qkv, proj_qkv, conv_w)


def _gdn_conv_bwd(dqkv_c, proj_qkv, conv_w, tt=256):
    t, width = proj_qkv.shape
    n_tiles = t // tt
    hb = tt // HALO_GDN
    n_hb = t // HALO_GDN
    offsets = [K_GDN - 1 - k for k in range(K_GDN)]

    def body(d_ref, dn_ref, x_ref, w_ref, dx_ref, dw_ref, bank_d):
        i = pl.program_id(1)

        @pl.when(i == 0)
        def _():
            dw_ref[...] = jnp.zeros_like(dw_ref)

        bank_d[0, 0:tt, :] = d_ref[...]
        bank_d[0, tt:, :] = jnp.where(i == n_tiles - 1, 0.0, dn_ref[...])
        _build_bank(bank_d, sorted({off % 8 for off in offsets}))

        def emit_dx(r0, lanes, acc):
            dx_ref[pl.ds(r0, 16), lanes] = acc.astype(BF16)

        _conv_taps(bank_d, w_ref, offsets, tt, D, emit_dx)

        def emit_dw(k, lanes, row):
            dw_ref[k:k + 1, lanes] += row

        _conv_dw(bank_d, x_ref, offsets, tt, D, emit_dw)

    return pl.pallas_call(
        body, name="gdn_conv_bwd", grid=(width // D, n_tiles),
        in_specs=[pl.BlockSpec((tt, D), lambda j, i: (i, j)),
                  pl.BlockSpec((HALO_GDN, D), lambda j, i: (jnp.minimum((i + 1) * hb, n_hb - 1), j)),
                  pl.BlockSpec((tt, D), lambda j, i: (i, j)),
                  pl.BlockSpec((8, D), lambda j, i: (0, j))],
        out_specs=[pl.BlockSpec((tt, D), lambda j, i: (i, j)), pl.BlockSpec((8, D), lambda j, i: (0, j))],
        out_shape=[jax.ShapeDtypeStruct((t, width), BF16), jax.ShapeDtypeStruct((8, width), F32)],
        scratch_shapes=[pltpu.VMEM((8, tt + HALO_GDN, D), F32)],
        compiler_params=_cparams(("parallel", "arbitrary")),
    )(dqkv_c, dqkv_c, proj_qkv, conv_w)


def _pair_rows(ci):
    return slice(ci * CHUNK, (ci + 1) * CHUNK)


def _pair_lanes(h, base=0):
    return slice(base + h * HEAD_DIM, base + (h + 1) * HEAD_DIM)


def _pair_slices(ref):
    return [ref[_pair_rows(ci), _pair_lanes(h)] for ci in range(CHUNKS_PER_STEP) for h in range(HEADS)]


def _gdn_chunk_fwd(qkv_c, proj_ba, proj_gz, a_row, dt_row, ng_row):
    t = qkv_c.shape[0]
    rows = CHUNKS_PER_STEP * CHUNK
    n_steps = t // rows

    def body(q_ref, k_ref, v_ref, ba_ref, gz_ref, a_ref, dt_ref, ng_ref, o_ref, ssave_ref, s_scr):
        @pl.when(pl.program_id(0) == 0)
        def _():
            s_scr[...] = jnp.zeros_like(s_scr)

        s_list = [s_scr[h] for h in range(HEADS)]
        for h in range(HEADS):
            ssave_ref[0, h] = s_list[h]
        ba_list = [ba_ref[_pair_rows(ci), :] for ci in range(CHUNKS_PER_STEP)]
        o_list, s_new = _gdn_chunk(_pair_slices(q_ref), _pair_slices(k_ref), _pair_slices(v_ref), ba_list,
                                   _pair_slices(gz_ref), s_list, a_ref[...], dt_ref[...], ng_ref[...])
        for ci in range(CHUNKS_PER_STEP):
            for h in range(HEADS):
                o_ref[_pair_rows(ci), _pair_lanes(h)] = o_list[ci * HEADS + h].astype(BF16)
        for h in range(HEADS):
            s_scr[h] = s_new[h]

    row = pl.BlockSpec((1, HEAD_DIM), lambda i: (0, 0))
    return pl.pallas_call(
        body, name="gdn_chunk_fwd", grid=(n_steps,),
        in_specs=[pl.BlockSpec((rows, D), lambda i: (i, 0)), pl.BlockSpec((rows, D), lambda i: (i, 1)),
                  pl.BlockSpec((rows, D), lambda i: (i, 2)), pl.BlockSpec((rows, HEAD_DIM), lambda i: (i, 0)),
                  pl.BlockSpec((rows, D), lambda i: (i, 0)), row, row, row],
        out_specs=[pl.BlockSpec((rows, D), lambda i: (i, 0)),
                   pl.BlockSpec((1, HEADS, HEAD_DIM, HEAD_DIM), lambda i: (i, 0, 0, 0))],
        out_shape=[jax.ShapeDtypeStruct((t, D), BF16),
                   jax.ShapeDtypeStruct((n_steps, HEADS, HEAD_DIM, HEAD_DIM), F32)],
        scratch_shapes=[pltpu.VMEM((HEADS, HEAD_DIM, HEAD_DIM), F32)],
        compiler_params=_cparams(("arbitrary",)),
    )(qkv_c, qkv_c, qkv_c, proj_ba, proj_gz, a_row, dt_row, ng_row)


def _gdn_chunk_bwd(qkv_c, proj_ba, proj_gz, s_saved, do_gated, a_row, dt_row, ng_row):
    t = qkv_c.shape[0]
    rows = CHUNKS_PER_STEP * CHUNK
    n_steps = t // rows

    def body(q_ref, k_ref, v_ref, ba_ref, gz_ref, s_ref, do_ref, a_ref, dt_ref, ng_ref,
             dqkv_ref, dba_ref, dgz_ref, da_ref, ddt_ref, dng_ref, ds_scr):
        @pl.when(pl.program_id(0) == 0)
        def _():
            ds_scr[...] = jnp.zeros_like(ds_scr)
            da_ref[...] = jnp.zeros_like(da_ref)
            ddt_ref[...] = jnp.zeros_like(ddt_ref)
            dng_ref[...] = jnp.zeros_like(dng_ref)

        s_list = [s_ref[0, h] for h in range(HEADS)]
        ba_list = [ba_ref[_pair_rows(ci), :] for ci in range(CHUNKS_PER_STEP)]
        _, vjp = jax.vjp(_gdn_chunk, _pair_slices(q_ref), _pair_slices(k_ref), _pair_slices(v_ref), ba_list,
                         _pair_slices(gz_ref), s_list, a_ref[...], dt_ref[...], ng_ref[...])
        ds_list = [ds_scr[h] for h in range(HEADS)]
        dq, dk, dv, dba, dgz, ds_in, da, ddt, dng = vjp((_pair_slices(do_ref), ds_list))
        for ci in range(CHUNKS_PER_STEP):
            for h in range(HEADS):
                p = ci * HEADS + h
                dqkv_ref[_pair_rows(ci), _pair_lanes(h)] = dq[p]
                dqkv_ref[_pair_rows(ci), _pair_lanes(h, D)] = dk[p]
                dqkv_ref[_pair_rows(ci), _pair_lanes(h, 2 * D)] = dv[p]
                dgz_ref[_pair_rows(ci), _pair_lanes(h)] = dgz[p].astype(BF16)
            dba_ref[_pair_rows(ci), :] = dba[ci].astype(BF16)
        for h in range(HEADS):
            ds_scr[h] = ds_in[h]
        da_ref[0:1, :] += da
        ddt_ref[0:1, :] += ddt
        dng_ref[0:1, :] += dng

    rev = lambda i: n_steps - 1 - i
    row = pl.BlockSpec((1, HEAD_DIM), lambda i: (0, 0))
    acc = pl.BlockSpec((8, HEAD_DIM), lambda i: (0, 0))
    outs = pl.pallas_call(
        body, name="gdn_chunk_bwd", grid=(n_steps,),
        in_specs=[pl.BlockSpec((rows, D), lambda i: (rev(i), 0)), pl.BlockSpec((rows, D), lambda i: (rev(i), 1)),
                  pl.BlockSpec((rows, D), lambda i: (rev(i), 2)),
                  pl.BlockSpec((rows, HEAD_DIM), lambda i: (rev(i), 0)),
                  pl.BlockSpec((rows, D), lambda i: (rev(i), 0)),
                  pl.BlockSpec((1, HEADS, HEAD_DIM, HEAD_DIM), lambda i: (rev(i), 0, 0, 0)),
                  pl.BlockSpec((rows, D), lambda i: (rev(i), 0)), row, row, row],
        out_specs=[pl.BlockSpec((rows, 3 * D), lambda i: (rev(i), 0)),
                   pl.BlockSpec((rows, HEAD_DIM), lambda i: (rev(i), 0)),
                   pl.BlockSpec((rows, D), lambda i: (rev(i), 0)), acc, acc, acc],
        out_shape=[jax.ShapeDtypeStruct((t, 3 * D), F32)]
        + [jax.ShapeDtypeStruct((t, HEAD_DIM), BF16), jax.ShapeDtypeStruct((t, D), BF16)]
        + [jax.ShapeDtypeStruct((8, HEAD_DIM), F32)] * 3,
        scratch_shapes=[pltpu.VMEM((HEADS, HEAD_DIM, HEAD_DIM), F32)],
        compiler_params=_cparams(("arbitrary",)),
    )(qkv_c, qkv_c, qkv_c, proj_ba, proj_gz, s_saved, do_gated, a_row, dt_row, ng_row)
    return outs


def _merge(a_out, o_gated, proj_gate, x, target, w_conf, w_gdn, w_o, pg, pb, tt=256):
    t = x.shape[0]

    def body(a_ref, o_ref, gt_ref, x_ref, y_ref, wc_ref, wg_ref, wo_ref, pg_ref, pb_ref,
             loss_ref, dpg_ref, dpb_ref, dx_ref, dgt_ref, da_ref, do_ref, h_ref, ds_ref, dyc_ref, dyg_ref):
        @pl.when(pl.program_id(0) == 0)
        def _():
            loss_ref[...] = jnp.zeros_like(loss_ref)
            dpg_ref[...] = jnp.zeros_like(dpg_ref)
            dpb_ref[...] = jnp.zeros_like(dpb_ref)

        wc, wg, wo = wc_ref[...], wg_ref[...], wo_ref[...]
        y_conf = _dot(a_ref[...], wc, _NN)
        y_gdn = _dot(o_ref[...], wg, _NN)
        sc = _sigmoid(gt_ref[:, 0:D])
        sg = _sigmoid(gt_ref[:, D:2 * D])
        h = sc * y_conf + sg * y_gdn
        z = DN_ALPHA * x_ref[...] + _dot(h, wo, _NN)
        mu = jnp.mean(z, axis=-1, keepdims=True)
        zc = z - mu
        rstd = lax.rsqrt(jnp.mean(zc * zc, axis=-1, keepdims=True) + LN_EPS)
        xhat = zc * rstd
        gain = pg_ref[...]
        err = xhat * gain + pb_ref[...] - y_ref[...]
        tok = jnp.mean(err * err, axis=-1, keepdims=True)
        loss_ref[...] += 0.5 * jnp.sum(tok, axis=0, keepdims=True)

        dy = err * (1.0 / D)
        dpg_ref[0:1, :] += jnp.sum(dy * xhat, axis=0, keepdims=True)
        dpb_ref[0:1, :] += jnp.sum(dy, axis=0, keepdims=True)
        dxh = dy * gain
        dz = rstd * (dxh - jnp.mean(dxh, axis=-1, keepdims=True)
                     - xhat * jnp.mean(dxh * xhat, axis=-1, keepdims=True))
        dx_ref[...] = DN_ALPHA * dz
        dh = _dot(dz, wo, _NT)
        dyc = dh * sc
        dyg = dh * sg
        dgt_ref[:, 0:D] = (dh * y_conf * sc * (1.0 - sc)).astype(BF16)
        dgt_ref[:, D:2 * D] = (dh * y_gdn * sg * (1.0 - sg)).astype(BF16)
        da_ref[...] = _dot(dyc, wc, _NT)
        do_ref[...] = _dot(dyg, wg, _NT)
        h_ref[...] = h.astype(BF16)
        ds_ref[...] = dz.astype(BF16)
        dyc_ref[...] = dyc.astype(BF16)
        dyg_ref[...] = dyg.astype(BF16)

    tile = pl.BlockSpec((tt, D), lambda i: (i, 0))
    wide = pl.BlockSpec((tt, 2 * D), lambda i: (i, 0))
    mat = pl.BlockSpec((D, D), lambda i: (0, 0))
    row = pl.BlockSpec((1, D), lambda i: (0, 0))
    acc = pl.BlockSpec((8, D), lambda i: (0, 0))
    act = lambda dt: jax.ShapeDtypeStruct((t, D), dt)
    return pl.pallas_call(
        body, name="merge", grid=(t // tt,),
        in_specs=[tile, tile, wide, tile, tile, mat, mat, mat, row, row],
        out_specs=[pl.BlockSpec((8, 128), lambda i: (0, 0)), acc, acc, tile, wide, tile, tile, tile, tile, tile, tile],
        out_shape=[jax.ShapeDtypeStruct((8, 128), F32), jax.ShapeDtypeStruct((8, D), F32),
                   jax.ShapeDtypeStruct((8, D), F32), act(F32), jax.ShapeDtypeStruct((t, 2 * D), BF16),
                   act(F32), act(F32), act(BF16), act(BF16), act(BF16), act(BF16)],
        compiler_params=_cparams(("arbitrary",)),
    )(a_out, o_gated, proj_gate, x, target, w_conf, w_gdn, w_o, pg, pb)


def _mesh_place():
    x, y, c = lax.axis_index("x"), lax.axis_index("y"), lax.axis_index("c")
    return x, y, c


def _flat(px, py, pc):
    return 4 * px + 2 * py + pc


def _all_gather(shards):
    n = len(shards)

    def body(*refs):
        ins, outs = refs[:n], refs[n:2 * n]
        send_sems, recv_sems, local_sems = refs[2 * n:]
        x, y, c = _mesh_place()
        me, sibling = (x, y, c), (x, y, 1 - c)
        chips = [(1 - x, y), (x, 1 - y), (1 - x, 1 - y)]

        def copy(a, k, block, to, src=None):
            dst = outs[a].at[_flat(*block)]
            return pltpu.make_async_remote_copy(
                src_ref=dst if src is None else src, dst_ref=dst,
                send_sem=send_sems.at[a, k], recv_sem=recv_sems.at[a, k],
                device_id=to, device_id_type=pl.DeviceIdType.MESH)

        mine = [pltpu.make_async_copy(ins[a], outs[a].at[_flat(*me)], local_sems.at[a]) for a in range(n)]
        for cp in mine:
            cp.start()
        first = []
        for a in range(n):
            first.append(copy(a, 0, me, sibling, src=ins[a]))
            first += [copy(a, 1 + j, me, (*chip, c), src=ins[a]) for j, chip in enumerate(chips)]
        for cp in first:
            cp.start()
        passed = []
        for j, chip in enumerate(chips):
            for a in range(n):
                copy(a, 1 + j, (*chip, c), me).wait_recv()
                fwd = copy(a, 4 + j, (*chip, c), sibling)
                fwd.start()
                passed.append(fwd)
        for a in range(n):
            copy(a, 0, sibling, me).wait_recv()
            for j, chip in enumerate(chips):
                copy(a, 4 + j, (*chip, 1 - c), me).wait_recv()
        for cp in first + passed:
            cp.wait_send()
        for cp in mine:
            cp.wait()

    any_spec = pl.BlockSpec(memory_space=pl.ANY)
    return pl.pallas_call(
        body, name="all_gather_weights",
        in_specs=[any_spec] * n, out_specs=[any_spec] * n,
        out_shape=[jax.ShapeDtypeStruct((N_DEV,) + s.shape, s.dtype) for s in shards],
        scratch_shapes=[pltpu.SemaphoreType.DMA((n, 7)), pltpu.SemaphoreType.DMA((n, 7)),
                        pltpu.SemaphoreType.DMA((n,))],
    )(*shards)


def _exchange_and_grad_x(block_arrays, small, init, segments, tm=512, tk=1024):
    nb = len(block_arrays)
    ns = len(segments)
    m, k1 = init.shape
    widths = [min(tk, a.shape[1]) for a, _ in segments]
    counts = [a.shape[1] // wd for (a, _), wd in zip(segments, widths)]
    starts = [sum(counts[:s]) for s in range(ns)]
    n_j = sum(counts)
    n_i = m // tm

    def body(*refs):
        g_refs, s_ref, i_ref = refs[:nb], refs[nb], refs[nb + 1]
        seg_refs = refs[nb + 2:nb + 2 + 2 * ns]
        outs = refs[nb + 2 + 2 * ns:]
        land_refs, sall_ref, o_ref = outs[:nb], outs[nb], outs[nb + 1]
        send_sems, recv_sems, local_sems = outs[nb + 2:]
        i, j = pl.program_id(0), pl.program_id(1)

        def copies(with_arrivals):
            x, y, c = _mesh_place()
            me = _flat(x, y, c)
            mine = [pltpu.make_async_copy(g_refs[a].at[me], land_refs[a].at[me], local_sems.at[a]) for a in range(nb)]
            mine.append(pltpu.make_async_copy(s_ref, sall_ref.at[me], local_sems.at[nb]))
            sends, recvs = [], []
            for k in range(7):
                mask = k + 1
                px = 1 - x if mask & 4 else x
                py = 1 - y if mask & 2 else y
                pc = 1 - c if mask & 1 else c
                peer = _flat(px, py, pc)
                for a in range(nb + 1):
                    kw = dict(send_sem=send_sems.at[a, k], recv_sem=recv_sems.at[a, k],
                              device_id=(px, py, pc), device_id_type=pl.DeviceIdType.MESH)
                    src = g_refs[a].at[peer] if a < nb else s_ref
                    land = land_refs[a] if a < nb else sall_ref
                    sends.append(pltpu.make_async_remote_copy(src_ref=src, dst_ref=land.at[me], **kw))
                    if with_arrivals:
                        recvs.append(pltpu.make_async_remote_copy(src_ref=src, dst_ref=land.at[peer], **kw))
            return mine, sends, recvs

        @pl.when((i == 0) & (j == 0))
        def _():
            mine, sends, _ = copies(False)
            for cp in mine + sends:
                cp.start()

        @pl.when(j == 0)
        def _():
            o_ref[...] = i_ref[...]

        for s in range(ns):
            @pl.when((j >= starts[s]) & (j < starts[s] + counts[s]))
            def _(s=s):
                o_ref[...] += lax.dot_general(seg_refs[2 * s][...], seg_refs[2 * s + 1][...], (_NT, ((), ())),
                                              preferred_element_type=F32)

        @pl.when((i == n_i - 1) & (j == n_j - 1))
        def _():
            mine, sends, recvs = copies(True)
            for cp in recvs:
                cp.wait_recv()
            for cp in sends:
                cp.wait_send()
            for cp in mine:
                cp.wait()

    any_spec = pl.BlockSpec(memory_space=pl.ANY)
    seg_specs = []
    for s in range(ns):
        col = lambda i, j, s=s: jnp.clip(j - starts[s], 0, counts[s] - 1)
        seg_specs.append(pl.BlockSpec((tm, widths[s]), lambda i, j, col=col: (i, col(i, j))))
        seg_specs.append(pl.BlockSpec((k1, widths[s]), lambda i, j, col=col: (0, col(i, j))))
    tile = pl.BlockSpec((tm, k1), lambda i, j: (i, 0))
    outs = pl.pallas_call(
        body, name="exchange_grads_and_grad_x", grid=(n_i, n_j),
        in_specs=[any_spec] * (nb + 1) + [tile] + seg_specs,
        out_specs=[any_spec] * (nb + 1) + [tile],
        out_shape=[jax.ShapeDtypeStruct(b.shape, b.dtype) for b in block_arrays]
        + [jax.ShapeDtypeStruct((N_DEV,) + small.shape, small.dtype), jax.ShapeDtypeStruct((m, k1), F32)],
        scratch_shapes=[pltpu.SemaphoreType.DMA((nb + 1, 7)), pltpu.SemaphoreType.DMA((nb + 1, 7)),
                        pltpu.SemaphoreType.DMA((nb + 1,))],
        compiler_params=_cparams(("arbitrary", "arbitrary")),
    )(*block_arrays, small, init, *[r for seg in segments for r in seg])
    return outs[:nb], outs[nb], outs[nb + 1]


def _adamw(parts, w, m, v, name, tile):
    rows, cols = w.shape

    def body(p_ref, w_ref, m_ref, v_ref, g_ref, d_ref, nm_ref, nv_ref):
        g = p_ref[0].astype(F32)
        for s in range(1, N_DEV):
            g = g + p_ref[s].astype(F32)
        nm = ADAM_B1 * m_ref[...] + (1.0 - ADAM_B1) * g
        nv = ADAM_B2 * v_ref[...] + (1.0 - ADAM_B2) * jnp.square(g)
        m_hat = nm / (1.0 - ADAM_B1 ** ADAM_STEP)
        v_hat = nv / (1.0 - ADAM_B2 ** ADAM_STEP)
        g_ref[...] = g
        d_ref[...] = -ADAM_LR * (m_hat / (jnp.sqrt(v_hat) + ADAM_EPS) + ADAM_WD * w_ref[...])
        nm_ref[...] = nm
        nv_ref[...] = nv

    blk = pl.BlockSpec((tile, cols), lambda i: (i, 0))
    out = jax.ShapeDtypeStruct((rows, cols), F32)
    return pl.pallas_call(
        body, name=name, grid=(rows // tile,),
        in_specs=[pl.BlockSpec((N_DEV, tile, cols), lambda i: (0, i, 0)), blk, blk, blk],
        out_specs=[blk, blk, blk, blk], out_shape=[out, out, out, out],
        compiler_params=_cparams(("parallel",)),
    )(parts, w, m, v)


def _rows_of(flat, n_rows):
    flat = flat.reshape(-1)
    return jnp.pad(flat, (0, n_rows * D - flat.shape[0])).reshape(n_rows, D)


def _pack_shards(conf_w_out, gdn_w_out, w_o, conf_dw_w, gdn_conv_w):
    return jnp.concatenate([conf_w_out, gdn_w_out, w_o, _rows_of(conf_dw_w, 16), _rows_of(gdn_conv_w, 16)], axis=0)


def _unpack_shards(p):
    dw = p[ROW_DW:ROW_DW + 4].reshape(-1)[:K_CONF * 128].reshape(K_CONF, 128)
    gc = p[ROW_GC:ROW_GC + 2].reshape(-1)[:K_GDN * 384].reshape(K_GDN, 384)
    return p[ROW_CWO:ROW_CWO + 128], p[ROW_GWO:ROW_GWO + 128], p[ROW_WO:ROW_WO + 128], dw, gc


def _pack_small(dw_b, ln_g, ln_b, pg, pb, ng, a_log, dt_bias, loss=None):
    s = jnp.zeros((SMALL_ROWS, D), F32)
    for r, val in enumerate((dw_b, ln_g, ln_b, pg, pb, ng, a_log, dt_bias)):
        s = s.at[r, :val.shape[0]].set(val)
    if loss is not None:
        s = s.at[8, 0].set(loss)
    return s


def _unpack_small(s):
    return (s[0], s[1], s[2], s[3], s[4], s[5, :HEAD_DIM], s[6, :HEADS], s[7, :HEADS])


def _scatter_blocks(g_cwo, g_gwo, g_wo, g_dw, g_gc):
    dw = g_dw.reshape(K_CONF, N_DEV, 128).transpose(1, 0, 2).reshape(N_DEV, K_CONF * 128)
    dw = jnp.pad(dw, ((0, 0), (0, 16 * D - K_CONF * 128))).reshape(N_DEV, 16, D)
    gc = g_gc.reshape(K_GDN, N_DEV, 384).transpose(1, 0, 2).reshape(N_DEV, K_GDN * 384)
    gc = jnp.pad(gc, ((0, 0), (0, 16 * D - K_GDN * 384))).reshape(N_DEV, 16, D)
    return jnp.concatenate([g_cwo.reshape(N_DEV, 128, D), g_gwo.reshape(N_DEV, 128, D),
                            g_wo.reshape(N_DEV, 128, D), dw, gc], axis=1)


def kernel(x, w_in, conf_dw_w, conf_dw_b, conf_ln_g, conf_ln_b, conf_w_out, gdn_conv_w, gdn_A_log, gdn_dt_bias, gdn_norm_g, gdn_w_out, w_o, post_ln_g, post_ln_b, loss_target, m_w_in, m_conf_dw_w, m_conf_dw_b, m_conf_ln_g, m_conf_ln_b, m_conf_w_out, m_gdn_conv_w, m_gdn_A_log, m_gdn_dt_bias, m_gdn_norm_g, m_gdn_w_out, m_w_o, m_post_ln_g, m_post_ln_b, v_w_in, v_conf_dw_w, v_conf_dw_b, v_conf_ln_g, v_conf_ln_b, v_conf_w_out, v_gdn_conv_w, v_gdn_A_log, v_gdn_dt_bias, v_gdn_norm_g, v_gdn_w_out, v_w_o, v_post_ln_g, v_post_ln_b):
    t = x.shape[1]
    x2 = x.reshape(t, D)
    target = loss_target.reshape(t, D)
    x_bf = x2.astype(BF16)

    w_pack = _pack_shards(conf_w_out, gdn_w_out, w_o, conf_dw_w, gdn_conv_w)
    convw = jnp.concatenate([_rows_of(conf_dw_w, 8), _rows_of(gdn_conv_w, 8)], axis=0)
    all_w_in, all_w, all_convw = _all_gather([w_in.astype(BF16), w_pack.astype(BF16), convw])
    w_full = all_w_in.transpose(1, 0, 2).reshape(D, W_IN_COLS)
    w_conf = w_full[:, 0:3 * D]
    w_qkv = w_full[:, 3 * D:6 * D]
    w_gz = w_full[:, 6 * D:7 * D]
    w_ba = jnp.pad(w_full[:, 7 * D:7 * D + 2 * HEADS], ((0, 0), (0, HEAD_DIM - 2 * HEADS)))
    w_gate = w_full[:, 7 * D + 2 * HEADS:]
    cwo_full = all_w[:, ROW_CWO:ROW_CWO + 128].reshape(D, D)
    gwo_full = all_w[:, ROW_GWO:ROW_GWO + 128].reshape(D, D)
    wo_full = all_w[:, ROW_WO:ROW_WO + 128].reshape(D, D)
    dw_full = all_convw[:, 0:4].reshape(N_DEV, 4 * D)[:, :K_CONF * 128].reshape(N_DEV, K_CONF, 128)
    dw_full = jnp.pad(dw_full.transpose(1, 0, 2).reshape(K_CONF, D), ((0, 32 - K_CONF), (0, 0)))
    gc_full = all_convw[:, 8:10].reshape(N_DEV, 2 * D)[:, :K_GDN * 384].reshape(N_DEV, K_GDN, 384)
    gc_full = jnp.pad(gc_full.transpose(1, 0, 2).reshape(K_GDN, 3 * D), ((0, 8 - K_GDN), (0, 0)))

    row = lambda vec: vec.reshape(1, -1)
    lane_row = lambda vec, at: jnp.zeros((1, HEAD_DIM), F32).at[0, at:at + vec.shape[0]].set(vec)
    a_row = lane_row(gdn_A_log, HEADS)
    dt_row = lane_row(gdn_dt_bias, HEADS)
    ng_row = row(gdn_norm_g)

    proj_conf = _matmul_nn(x_bf, w_conf, "proj_conf")
    proj_qkv = _matmul_nn(x_bf, w_qkv, "proj_qkv")
    proj_gz = _matmul_nn(x_bf, w_gz, "proj_gz")
    proj_gate = _matmul_nn(x_bf, w_gate, "proj_gate")
    proj_ba = _matmul_nn(x_bf, w_ba, "proj_ba")
    cpre, a_out = _conf_fwd(proj_conf, dw_full, row(conf_dw_b), row(conf_ln_g), row(conf_ln_b))
    qkv_c = _gdn_conv_fwd(proj_qkv, gc_full)
    o_gated, s_saved = _gdn_chunk_fwd(qkv_c, proj_ba, proj_gz, a_row, dt_row, ng_row)

    (loss_acc, d_pg, d_pb, dx, d_gate, da_out, do_gated, h_bf, dsub_bf, dyc_bf, dyg_bf) = _merge(
        a_out, o_gated, proj_gate, x2, target, cwo_full, gwo_full, wo_full, row(post_ln_g), row(post_ln_b))
    g_wo = _matmul_tn(h_bf, dsub_bf, "grad_w_o")
    g_cwo = _matmul_tn(a_out, dyc_bf, "grad_conf_w_out")
    g_gwo = _matmul_tn(o_gated, dyg_bf, "grad_gdn_w_out")

    dqkv_c, d_ba, d_gz, d_a_row, d_dt_row, d_ng_row = _gdn_chunk_bwd(
        qkv_c, proj_ba, proj_gz, s_saved, do_gated, a_row, dt_row, ng_row)
    d_qkv, g_gc = _gdn_conv_bwd(dqkv_c, proj_qkv, gc_full)

    dcpre, dcz, d_ln_g, d_ln_b = _conf_bwd_post(cpre, proj_conf, da_out, row(conf_ln_g), row(conf_ln_b))
    d_conf, g_dw, g_dwb = _conf_bwd_conv(dcpre, proj_conf, dcz, dw_full)

    segments = [(d_conf, w_conf, "conf"), (d_qkv, w_qkv, "qkv"), (d_gz, w_gz, "gz"),
                (d_ba, w_ba, "ba"), (d_gate, w_gate, "gate")]
    g_cols = {tag: _matmul_tn(x_bf, d_seg, "grad_w_in_" + tag) for d_seg, _, tag in segments}
    g_w_in = jnp.concatenate([g_cols["conf"], g_cols["qkv"], g_cols["gz"], g_cols["ba"][:, :2 * HEADS],
                              g_cols["gate"]], axis=1)

    w_in_blocks = g_w_in.reshape(D, N_DEV, W_IN_SHARD).transpose(1, 0, 2).astype(BF16)
    blocks = _scatter_blocks(g_cwo, g_gwo, g_wo, g_dw[:K_CONF], g_gc[:K_GDN]).astype(BF16)
    small = _pack_small(g_dwb[0], d_ln_g[0], d_ln_b[0], d_pg[0], d_pb[0], d_ng_row[0],
                        d_a_row[0, HEADS:2 * HEADS], d_dt_row[0, HEADS:2 * HEADS], loss_acc[0, 0])
    (landed_w_in, landed), small_all, dx = _exchange_and_grad_x(
        [w_in_blocks, blocks], small, dx, [(d_seg, w_seg) for d_seg, w_seg, _ in segments])

    m_pack = _pack_shards(m_conf_w_out, m_gdn_w_out, m_w_o, m_conf_dw_w, m_gdn_conv_w)
    v_pack = _pack_shards(v_conf_w_out, v_gdn_w_out, v_w_o, v_conf_dw_w, v_gdn_conv_w)
    big_w_in = _adamw(landed_w_in, w_in, m_w_in, v_w_in, "adamw_w_in", W_IN_TILE)
    big = _adamw(landed, w_pack, m_pack, v_pack, "adamw_shards", PACK_TILE)
    ws = _pack_small(conf_dw_b, conf_ln_g, conf_ln_b, post_ln_g, post_ln_b, gdn_norm_g, gdn_A_log, gdn_dt_bias)
    ms = _pack_small(m_conf_dw_b, m_conf_ln_g, m_conf_ln_b, m_post_ln_g, m_post_ln_b, m_gdn_norm_g, m_gdn_A_log,
                     m_gdn_dt_bias)
    vs = _pack_small(v_conf_dw_b, v_conf_ln_g, v_conf_ln_b, v_post_ln_g, v_post_ln_b, v_gdn_norm_g, v_gdn_A_log,
                     v_gdn_dt_bias)
    sml = _adamw(small_all, ws, ms, vs, "adamw_replicated", SMALL_ROWS)

    loss = sml[0][8, 0]
    outs = []
    for b_w_in, big_k, sml_k in zip(big_w_in, big, sml):
        b_cwo, b_gwo, b_wo, b_dw, b_gc = _unpack_shards(big_k)
        s_dwb, s_lng, s_lnb, s_pg, s_pb, s_ng, s_a, s_dt = _unpack_small(sml_k)
        outs.append([b_w_in, b_dw, s_dwb, s_lng, s_lnb, b_cwo, b_gc, s_a, s_dt, s_ng, b_gwo, b_wo, s_pg, s_pb])
    return (loss, dx.reshape(1, t, D), *outs[0], *outs[1], *outs[2], *outs[3])
```

```python
import functools

import jax
import jax.numpy as jnp
from jax import lax
from jax.experimental import pallas as pl
from jax.experimental.pallas import tpu as pltpu

F32 = jnp.float32
BF16 = jnp.bfloat16

N_DEV = 8
D = 1024
HEADS = 8
HEAD_DIM = 128
CHUNK = 64
CHUNKS_PER_STEP = 4
K_CONF = 31
K_GDN = 4
HALO_CONF = 32
HALO_GDN = 8
CONV_PIECE = 32
LN_EPS = 1e-5
RMS_EPS = 1e-6
L2_EPS = 1e-6
DN_ALPHA = 2.0 ** 0.25
ADAM_LR = 0.001
ADAM_B1 = 0.9
ADAM_B2 = 0.999
ADAM_EPS = 1e-08
ADAM_WD = 0.01
ADAM_STEP = 10

W_IN_COLS = 9232
W_IN_SHARD = W_IN_COLS // N_DEV
ROW_CWO = 0
ROW_GWO = ROW_CWO + 128
ROW_WO = ROW_GWO + 128
ROW_DW = ROW_WO + 128
ROW_GC = ROW_DW + 16
PACK_ROWS = ROW_GC + 16
PACK_TILE = PACK_ROWS // 2
W_IN_TILE = 128
SMALL_ROWS = 16
CONVW_ROWS = 16

VMEM_LIMIT = 56 * 1024 * 1024

_NN = ((1,), (0,))
_NT = ((1,), (1,))
_TN = ((0,), (0,))


def _cparams(sem=None):
    return pltpu.CompilerParams(dimension_semantics=sem, vmem_limit_bytes=VMEM_LIMIT)


def _dot(a, b, dims, hi=False):
    dn = (dims, ((), ()))
    a_hi = a.astype(BF16)
    b_hi = b.astype(BF16)
    if not hi:
        return lax.dot_general(a_hi, b_hi, dn, preferred_element_type=F32)
    a_lo = (a - a_hi.astype(F32)).astype(BF16)
    b_lo = (b - b_hi.astype(F32)).astype(BF16)
    d = lambda p, q: lax.dot_general(p, q, dn, preferred_element_type=F32)
    return d(a_hi, b_hi) + (d(a_hi, b_lo) + d(a_lo, b_hi))


def _make_mm(kind, hi):
    dims = {"nn": _NN, "nt": _NT, "tn": _TN}[kind]

    @jax.custom_vjp
    def mm(a, b):
        return _dot(a, b, dims, hi)

    def fwd(a, b):
        return _dot(a, b, dims, hi), (a, b)

    def bwd(res, g):
        a, b = res
        if kind == "nn":
            return _dot(g, b, _NT, hi), _dot(a, g, _TN, hi)
        if kind == "nt":
            return _dot(g, b, _NN, hi), _dot(g, a, _TN, hi)
        return _dot(b, g, _NT, hi), _dot(a, g, _NN, hi)

    mm.defvjp(fwd, bwd)
    return mm


_mm_nn = _make_mm("nn", False)
_mm_nt = _make_mm("nt", False)
_mm_tn = _make_mm("tn", False)
_mm_nn_hi = _make_mm("nn", True)
_mm_tn_hi = _make_mm("tn", True)


def _tri_inv_impl(lows):
    c = lows[0].shape[0]
    eye = (lax.broadcasted_iota(jnp.int32, (c, c), 0) == lax.broadcasted_iota(jnp.int32, (c, c), 1)).astype(F32)
    ms = [-low for low in lows]
    ps = [eye + m for m in ms]
    steps = max(c.bit_length() - 2, 0)
    for _ in range(steps):
        ms = [_dot(m, m, _NN) for m in ms]
        ps = [p + _dot(p, m, _NN) for p, m in zip(ps, ms)]
    rs = [eye - p - _dot(low, p, _NN, True) for low, p in zip(lows, ps)]
    return [p + _dot(p, r, _NN, True) for p, r in zip(ps, rs)]


@jax.custom_vjp
def _tri_inv(lows):
    return _tri_inv_impl(lows)


def _tri_inv_fwd(lows):
    xs = _tri_inv_impl(lows)
    return xs, xs


def _tri_inv_bwd(xs, dxs):
    ts = [_dot(x, dx, _TN) for x, dx in zip(xs, dxs)]
    return ([-_dot(t, x, _NT) for t, x in zip(ts, xs)],)


_tri_inv.defvjp(_tri_inv_fwd, _tri_inv_bwd)


def _sigmoid(x):
    return jax.nn.sigmoid(x)


def _silu(x):
    return x * jax.nn.sigmoid(x)


def _softplus(x):
    u = jnp.exp(-jnp.abs(x))
    log1p_u = jnp.where(u < 1e-3, u * (1.0 - u * (0.5 - u * (1.0 / 3.0))), jnp.log(1.0 + u))
    return jnp.maximum(x, 0.0) + log1p_u


def _layernorm(x, g, b):
    mu = jnp.mean(x, axis=-1, keepdims=True)
    xc = x - mu
    var = jnp.mean(xc * xc, axis=-1, keepdims=True)
    return xc * lax.rsqrt(var + LN_EPS) * g + b


def _pick_lane(x, lane):
    idx = lax.broadcasted_iota(jnp.int32, x.shape, 1)
    return jnp.sum(jnp.where(idx == lane, x, 0.0), axis=1, keepdims=True)


def _gdn_chunk(q_list, k_list, v_list, ba_list, gz_list, s_list, a_row, dt_row, ng_row):
    c = ba_list[0].shape[0]
    n_chunks = len(ba_list)
    pairs = [(ci, h) for ci in range(n_chunks) for h in range(HEADS)]
    every = range(len(pairs))
    rows = lax.broadcasted_iota(jnp.int32, (c, c), 0)
    cols = lax.broadcasted_iota(jnp.int32, (c, c), 1)
    causal = rows >= cols
    strict = rows > cols
    tril = causal.astype(F32)
    triu = (rows <= cols).astype(F32)
    last_row = lax.broadcasted_iota(jnp.int32, (c, 1), 0) == c - 1
    sub8 = lax.broadcasted_iota(jnp.int32, (HEADS, c), 0)

    beta_all = [_sigmoid(ba) for ba in ba_list]
    g_all = [-jnp.exp(a_row) * _softplus(ba + dt_row) for ba in ba_list]
    gc_all = [_mm_nn_hi(tril, g) for g in g_all]
    gc_t = [_mm_tn_hi(g, triu)[HEADS:2 * HEADS, :] for g in g_all]

    q = [_silu(a) for a in q_list]
    k = [_silu(a) for a in k_list]
    v = [_silu(a) for a in v_list]
    q = [a * lax.rsqrt(jnp.sum(a * a, axis=-1, keepdims=True) + L2_EPS) * (HEAD_DIM ** -0.5) for a in q]
    k = [a * lax.rsqrt(jnp.sum(a * a, axis=-1, keepdims=True) + L2_EPS) for a in k]
    beta = [_pick_lane(beta_all[ci], h) for ci, h in pairs]
    gc = [_pick_lane(gc_all[ci], HEADS + h) for ci, h in pairs]
    gc_cols = [jnp.sum(jnp.where(sub8 == h, gc_t[ci], 0.0), axis=0, keepdims=True) for ci, h in pairs]
    decay = [jnp.where(causal, jnp.exp(jnp.where(causal, gc[p] - gc_cols[p], 0.0)), 0.0) for p in every]
    kb = [k[p] * beta[p] for p in every]
    low = [jnp.where(strict, _mm_nt(kb[p], k[p]) * decay[p], 0.0) for p in every]
    x = _tri_inv(low)
    eg = [jnp.exp(gc[p]) for p in every]
    u = [_mm_nn(x[p], v[p] * beta[p]) for p in every]
    w = [_mm_nn(x[p], kb[p] * eg[p]) for p in every]
    intra = [_mm_nt(q[p], k[p]) * decay[p] for p in every]
    q_dec = [q[p] * eg[p] for p in every]
    g_last = [jnp.sum(jnp.where(last_row, gc[p], 0.0), axis=0, keepdims=True) for p in every]
    k_dec = [k[p] * jnp.exp(g_last[p] - gc[p]) for p in every]
    s_dec = [jnp.exp(g_last[p]) for p in every]

    o = []
    state = list(s_list)
    for ci in range(n_chunks):
        at = [ci * HEADS + h for h in range(HEADS)]
        v_new = [u[p] - _mm_nn(w[p], state[h]) for h, p in enumerate(at)]
        o += [_mm_nn(q_dec[p], state[h]) + _mm_nn(intra[p], v_new[h]) for h, p in enumerate(at)]
        state = [state[h] * s_dec[p] + _mm_tn(k_dec[p], v_new[h]) for h, p in enumerate(at)]
    o = [a * lax.rsqrt(jnp.mean(a * a, axis=-1, keepdims=True) + RMS_EPS) * ng_row for a in o]
    o = [o[p] * _silu(gz_list[p]) for p in every]
    return o, state


def _conf_post(cpre, cz, g, b):
    return _silu(_layernorm(cpre, g, b)) * _silu(cz)


def _matmul_nn(a, b, name, tm=512, tn=1024):
    m, k = a.shape
    n = b.shape[1]
    tn = min(tn, n)

    def body(a_ref, b_ref, o_ref):
        o_ref[...] = jnp.dot(a_ref[...], b_ref[...], preferred_element_type=F32)

    return pl.pallas_call(
        body, name=name, grid=(n // tn, m // tm),
        in_specs=[pl.BlockSpec((tm, k), lambda j, i: (i, 0)), pl.BlockSpec((k, tn), lambda j, i: (0, j))],
        out_specs=pl.BlockSpec((tm, tn), lambda j, i: (i, j)),
        out_shape=jax.ShapeDtypeStruct((m, n), F32),
        compiler_params=_cparams(("parallel", "parallel")),
    )(a, b)


def _matmul_tn(a, b, name, tt=512, tn=1024):
    t, k1 = a.shape
    n = b.shape[1]
    tn = min(tn, n)
    n_t = t // tt

    def body(a_ref, b_ref, o_ref, acc_ref):
        @pl.when(pl.program_id(1) == 0)
        def _():
            acc_ref[...] = jnp.zeros_like(acc_ref)

        acc_ref[...] += lax.dot_general(a_ref[...], b_ref[...], (_TN, ((), ())), preferred_element_type=F32)

        @pl.when(pl.program_id(1) == n_t - 1)
        def _():
            o_ref[...] = acc_ref[...].astype(BF16)

    return pl.pallas_call(
        body, name=name, grid=(n // tn, n_t),
        in_specs=[pl.BlockSpec((tt, k1), lambda j, i: (i, 0)), pl.BlockSpec((tt, tn), lambda j, i: (i, j))],
        out_specs=pl.BlockSpec((k1, tn), lambda j, i: (0, j)),
        out_shape=jax.ShapeDtypeStruct((k1, n), BF16),
        scratch_shapes=[pltpu.VMEM((k1, tn), F32)],
        compiler_params=_cparams(("parallel", "arbitrary")),
    )(a, b)


def _build_bank(bank_ref, shifts):
    ext = bank_ref[0]
    rows = ext.shape[0]
    for s in shifts:
        if s:
            bank_ref[s] = pltpu.roll(ext, rows - s, axis=0)


def _conv_taps(bank_ref, w_ref, offsets, n_rows, width, emit):
    def piece(rc, carry):
        r0 = pl.multiple_of(rc * CONV_PIECE, CONV_PIECE)
        for cb in range(width // 128):
            lanes = slice(cb * 128, (cb + 1) * 128)
            acc = jnp.zeros((CONV_PIECE, 128), F32)
            for k, off in enumerate(offsets):
                m, s = divmod(off, 8)
                acc = acc + bank_ref[s, pl.ds(r0 + 8 * m, CONV_PIECE), lanes] * w_ref[k:k + 1, lanes]
            emit(r0, lanes, acc)
        return carry

    lax.fori_loop(0, n_rows // CONV_PIECE, piece, 0)


def _conv_dw(bank_ref, d_ref, offsets, n_rows, width, emit):
    ms = [divmod(off, 8) for off in offsets]
    n_taps = len(offsets)
    group = max(1, 32 // n_taps)
    blocks = [slice(cb * 128, (cb + 1) * 128) for cb in range(width // 128)]
    for g0 in range(0, len(blocks), group):
        lane_group = blocks[g0:g0 + group]

        def piece(rc, accs, lane_group=lane_group):
            r0 = pl.multiple_of(rc * 8, 8)
            out = []
            for b, lanes in enumerate(lane_group):
                d = d_ref[pl.ds(r0, 8), lanes]
                out += [accs[b * n_taps + k] + d * bank_ref[s, pl.ds(r0 + 8 * m, 8), lanes]
                        for k, (m, s) in enumerate(ms)]
            return tuple(out)

        init = tuple(jnp.zeros((8, 128), F32) for _ in range(len(lane_group) * n_taps))
        accs = lax.fori_loop(0, n_rows // 8, piece, init)
        for b, lanes in enumerate(lane_group):
            for k in range(n_taps):
                emit(k, lanes, jnp.sum(accs[b * n_taps + k], axis=0, keepdims=True))


def _conf_fwd(proj_conf, dw_w, dw_b, ln_g, ln_b, tt=256):
    t = proj_conf.shape[0]
    hb = tt // HALO_CONF
    offsets = [HALO_CONF - (K_CONF - 1) + k for k in range(K_CONF)]

    def body(cv_ref, cg_ref, cz_ref, cvh_ref, cgh_ref, w_ref, b_ref, g_ref, bb_ref, cpre_ref, aout_ref, bank_ref):
        first = pl.program_id(0) == 0
        halo = cvh_ref[...] * _sigmoid(cgh_ref[...])
        bank_ref[0, 0:HALO_CONF, :] = jnp.where(first, 0.0, halo)
        bank_ref[0, HALO_CONF:, :] = cv_ref[...] * _sigmoid(cg_ref[...])
        _build_bank(bank_ref, range(8))

        def emit(r0, lanes, acc):
            cpre_ref[pl.ds(r0, CONV_PIECE), lanes] = acc + b_ref[0:1, lanes]

        _conv_taps(bank_ref, w_ref, offsets, tt, D, emit)
        aout_ref[...] = _conf_post(cpre_ref[...], cz_ref[...], g_ref[...], bb_ref[...]).astype(BF16)

    row = pl.BlockSpec((1, D), lambda i: (0, 0))
    return pl.pallas_call(
        body, name="conf_fwd", grid=(t // tt,),
        in_specs=[pl.BlockSpec((tt, D), lambda i: (i, 0)), pl.BlockSpec((tt, D), lambda i: (i, 1)),
                  pl.BlockSpec((tt, D), lambda i: (i, 2)),
                  pl.BlockSpec((HALO_CONF, D), lambda i: (jnp.maximum(i * hb - 1, 0), 0)),
                  pl.BlockSpec((HALO_CONF, D), lambda i: (jnp.maximum(i * hb - 1, 0), 1)),
                  pl.BlockSpec((32, D), lambda i: (0, 0)), row, row, row],
        out_specs=[pl.BlockSpec((tt, D), lambda i: (i, 0)), pl.BlockSpec((tt, D), lambda i: (i, 0))],
        out_shape=[jax.ShapeDtypeStruct((t, D), F32), jax.ShapeDtypeStruct((t, D), BF16)],
        scratch_shapes=[pltpu.VMEM((8, tt + HALO_CONF, D), F32)],
        compiler_params=_cparams(("parallel",)),
    )(proj_conf, proj_conf, proj_conf, proj_conf, proj_conf, dw_w, dw_b, ln_g, ln_b)


def _conf_bwd_post(cpre, proj_conf, da_out, ln_g, ln_b, tt=256):
    t = cpre.shape[0]

    def body(c_ref, z_ref, da_ref, g_ref, b_ref, dc_ref, dz_ref, dg_ref, db_ref):
        @pl.when(pl.program_id(0) == 0)
        def _():
            dg_ref[...] = jnp.zeros_like(dg_ref)
            db_ref[...] = jnp.zeros_like(db_ref)

        _, vjp = jax.vjp(_conf_post, c_ref[...], z_ref[...], g_ref[...], b_ref[...])
        dc, dz, dg, db = vjp(da_ref[...])
        dc_ref[...] = dc
        dz_ref[...] = dz.astype(BF16)
        dg_ref[0:1, :] += dg
        db_ref[0:1, :] += db

    row = pl.BlockSpec((1, D), lambda i: (0, 0))
    acc = pl.BlockSpec((8, D), lambda i: (0, 0))
    return pl.pallas_call(
        body, name="conf_bwd_post", grid=(t // tt,),
        in_specs=[pl.BlockSpec((tt, D), lambda i: (i, 0)), pl.BlockSpec((tt, D), lambda i: (i, 2)),
                  pl.BlockSpec((tt, D), lambda i: (i, 0)), row, row],
        out_specs=[pl.BlockSpec((tt, D), lambda i: (i, 0)), pl.BlockSpec((tt, D), lambda i: (i, 0)), acc, acc],
        out_shape=[jax.ShapeDtypeStruct((t, D), F32), jax.ShapeDtypeStruct((t, D), BF16),
                   jax.ShapeDtypeStruct((8, D), F32), jax.ShapeDtypeStruct((8, D), F32)],
        compiler_params=_cparams(("arbitrary",)),
    )(cpre, proj_conf, da_out, ln_g, ln_b)


def _conf_bwd_conv(dcpre, proj_conf, dcz, dw_w, tt=256):
    t = dcpre.shape[0]
    n_tiles = t // tt
    hb = tt // HALO_CONF
    n_hb = t // HALO_CONF
    offsets = [K_CONF - 1 - k for k in range(K_CONF)]

    def body(d_ref, dn_ref, cv_ref, cg_ref, dz_ref, w_ref, dp_ref, dw_ref, db_ref, bank_d, a_scr, da_scr):
        i = pl.program_id(0)

        @pl.when(i == 0)
        def _():
            dw_ref[...] = jnp.zeros_like(dw_ref)
            db_ref[...] = jnp.zeros_like(db_ref)

        cv = cv_ref[...]
        sg = _sigmoid(cg_ref[...])
        a_scr[...] = cv * sg
        bank_d[0, 0:tt, :] = d_ref[...]
        bank_d[0, tt:, :] = jnp.where(i == n_tiles - 1, 0.0, dn_ref[...])
        _build_bank(bank_d, range(8))

        def emit_da(r0, lanes, acc):
            da_scr[pl.ds(r0, CONV_PIECE), lanes] = acc

        _conv_taps(bank_d, w_ref, offsets, tt, D, emit_da)
        da = da_scr[...]
        dp_ref[:, 0:D] = (da * sg).astype(BF16)
        dp_ref[:, D:2 * D] = (da * cv * sg * (1.0 - sg)).astype(BF16)
        dp_ref[:, 2 * D:3 * D] = dz_ref[...]

        def emit_dw(k, lanes, row):
            dw_ref[k:k + 1, lanes] += row

        _conv_dw(bank_d, a_scr, offsets, tt, D, emit_dw)
        db_ref[0:1, :] += jnp.sum(d_ref[...], axis=0, keepdims=True)

    nxt = lambda i: jnp.minimum((i + 1) * hb, n_hb - 1)
    return pl.pallas_call(
        body, name="conf_bwd_conv", grid=(n_tiles,),
        in_specs=[pl.BlockSpec((tt, D), lambda i: (i, 0)), pl.BlockSpec((HALO_CONF, D), lambda i: (nxt(i), 0)),
                  pl.BlockSpec((tt, D), lambda i: (i, 0)), pl.BlockSpec((tt, D), lambda i: (i, 1)),
                  pl.BlockSpec((tt, D), lambda i: (i, 0)), pl.BlockSpec((32, D), lambda i: (0, 0))],
        out_specs=[pl.BlockSpec((tt, 3 * D), lambda i: (i, 0)), pl.BlockSpec((32, D), lambda i: (0, 0)),
                   pl.BlockSpec((8, D), lambda i: (0, 0))],
        out_shape=[jax.ShapeDtypeStruct((t, 3 * D), BF16), jax.ShapeDtypeStruct((32, D), F32),
                   jax.ShapeDtypeStruct((8, D), F32)],
        scratch_shapes=[pltpu.VMEM((8, tt + HALO_CONF, D), F32), pltpu.VMEM((tt, D), F32), pltpu.VMEM((tt, D), F32)],
        compiler_params=_cparams(("arbitrary",)),
    )(dcpre, dcpre, proj_conf, proj_conf, dcz, dw_w)


def _gdn_conv_fwd(proj_qkv, conv_w, tt=256):
    t, width = proj_qkv.shape
    hb = tt // HALO_GDN
    offsets = [HALO_GDN - (K_GDN - 1) + k for k in range(K_GDN)]
    shifts = sorted({off % 8 for off in offsets})

    def body(x_ref, xh_ref, w_ref, o_ref, bank_ref):
        bank_ref[0, 0:HALO_GDN, :] = jnp.where(pl.program_id(1) == 0, 0.0, xh_ref[...])
        bank_ref[0, HALO_GDN:, :] = x_ref[...]
        _build_bank(bank_ref, shifts)

        def emit(r0, lanes, acc):
            o_ref[pl.ds(r0, CONV_PIECE), lanes] = acc

        _conv_taps(bank_ref, w_ref, offsets, tt, D, emit)

    return pl.pallas_call(
        body, name="gdn_conv_fwd", grid=(width // D, t // tt),
        in_specs=[pl.BlockSpec((tt, D), lambda j, i: (i, j)),
                  pl.BlockSpec((HALO_GDN, D), lambda j, i: (jnp.maximum(i * hb - 1, 0), j)),
                  pl.BlockSpec((8, D), lambda j, i: (0, j))],
        out_specs=pl.BlockSpec((tt, D), lambda j, i: (i, j)),
        out_shape=jax.ShapeDtypeStruct((t, width), F32),
        scratch_shapes=[pltpu.VMEM((8, tt + HALO_GDN, D), F32)],
        compiler_params=_cparams(("parallel", "parallel")),
    )(proj_qkv, proj_qkv, conv_w)


def _gdn_conv_bwd(dqkv_c, proj_qkv, conv_w, tt=256):
    t, width = proj_qkv.shape
    n_tiles = t // tt
    hb = tt // HALO_GDN
    n_hb = t // HALO_GDN
    offsets = [K_GDN - 1 - k for k in range(K_GDN)]

    def body(d_ref, dn_ref, x_ref, w_ref, dx_ref, dw_ref, bank_d):
        i = pl.program_id(1)

        @pl.when(i == 0)
        def _():
            dw_ref[...] = jnp.zeros_like(dw_ref)

        bank_d[0, 0:tt, :] = d_ref[...]
        bank_d[0, tt:, :] = jnp.where(i == n_tiles - 1, 0.0, dn_ref[...])
        _build_bank(bank_d, sorted({off % 8 for off in offsets}))

        def emit_dx(r0, lanes, acc):
            dx_ref[pl.ds(r0, CONV_PIECE), lanes] = acc.astype(BF16)

        _conv_taps(bank_d, w_ref, offsets, tt, D, emit_dx)

        def emit_dw(k, lanes, row):
            dw_ref[k:k + 1, lanes] += row

        _conv_dw(bank_d, x_ref, offsets, tt, D, emit_dw)

    return pl.pallas_call(
        body, name="gdn_conv_bwd", grid=(width // D, n_tiles),
        in_specs=[pl.BlockSpec((tt, D), lambda j, i: (i, j)),
                  pl.BlockSpec((HALO_GDN, D), lambda j, i: (jnp.minimum((i + 1) * hb, n_hb - 1), j)),
                  pl.BlockSpec((tt, D), lambda j, i: (i, j)),
                  pl.BlockSpec((8, D), lambda j, i: (0, j))],
        out_specs=[pl.BlockSpec((tt, D), lambda j, i: (i, j)), pl.BlockSpec((8, D), lambda j, i: (0, j))],
        out_shape=[jax.ShapeDtypeStruct((t, width), BF16), jax.ShapeDtypeStruct((8, width), F32)],
        scratch_shapes=[pltpu.VMEM((8, tt + HALO_GDN, D), F32)],
        compiler_params=_cparams(("parallel", "arbitrary")),
    )(dqkv_c, dqkv_c, proj_qkv, conv_w)


def _pair_rows(ci):
    return slice(ci * CHUNK, (ci + 1) * CHUNK)


def _pair_lanes(h, base=0):
    return slice(base + h * HEAD_DIM, base + (h + 1) * HEAD_DIM)


def _pair_slices(ref):
    return [ref[_pair_rows(ci), _pair_lanes(h)] for ci in range(CHUNKS_PER_STEP) for h in range(HEADS)]


def _gdn_chunk_fwd(qkv_c, proj_ba, proj_gz, a_row, dt_row, ng_row):
    t = qkv_c.shape[0]
    rows = CHUNKS_PER_STEP * CHUNK
    n_steps = t // rows

    def body(q_ref, k_ref, v_ref, ba_ref, gz_ref, a_ref, dt_ref, ng_ref, o_ref, ssave_ref, s_scr):
        @pl.when(pl.program_id(0) == 0)
        def _():
            s_scr[...] = jnp.zeros_like(s_scr)

        s_list = [s_scr[h] for h in range(HEADS)]
        for h in range(HEADS):
            ssave_ref[0, h] = s_list[h]
        ba_list = [ba_ref[_pair_rows(ci), :] for ci in range(CHUNKS_PER_STEP)]
        o_list, s_new = _gdn_chunk(_pair_slices(q_ref), _pair_slices(k_ref), _pair_slices(v_ref), ba_list,
                                   _pair_slices(gz_ref), s_list, a_ref[...], dt_ref[...], ng_ref[...])
        for ci in range(CHUNKS_PER_STEP):
            for h in range(HEADS):
                o_ref[_pair_rows(ci), _pair_lanes(h)] = o_list[ci * HEADS + h].astype(BF16)
        for h in range(HEADS):
            s_scr[h] = s_new[h]

    row = pl.BlockSpec((1, HEAD_DIM), lambda i: (0, 0))
    return pl.pallas_call(
        body, name="gdn_chunk_fwd", grid=(n_steps,),
        in_specs=[pl.BlockSpec((rows, D), lambda i: (i, 0)), pl.BlockSpec((rows, D), lambda i: (i, 1)),
                  pl.BlockSpec((rows, D), lambda i: (i, 2)), pl.BlockSpec((rows, HEAD_DIM), lambda i: (i, 0)),
                  pl.BlockSpec((rows, D), lambda i: (i, 0)), row, row, row],
        out_specs=[pl.BlockSpec((rows, D), lambda i: (i, 0)),
                   pl.BlockSpec((1, HEADS, HEAD_DIM, HEAD_DIM), lambda i: (i, 0, 0, 0))],
        out_shape=[jax.ShapeDtypeStruct((t, D), BF16),
                   jax.ShapeDtypeStruct((n_steps, HEADS, HEAD_DIM, HEAD_DIM), F32)],
        scratch_shapes=[pltpu.VMEM((HEADS, HEAD_DIM, HEAD_DIM), F32)],
        compiler_params=_cparams(("arbitrary",)),
    )(qkv_c, qkv_c, qkv_c, proj_ba, proj_gz, a_row, dt_row, ng_row)


def _gdn_chunk_bwd(qkv_c, proj_ba, proj_gz, s_saved, do_gated, a_row, dt_row, ng_row):
    t = qkv_c.shape[0]
    rows = CHUNKS_PER_STEP * CHUNK
    n_steps = t // rows

    def body(q_ref, k_ref, v_ref, ba_ref, gz_ref, s_ref, do_ref, a_ref, dt_ref, ng_ref,
             dqkv_ref, dba_ref, dgz_ref, da_ref, ddt_ref, dng_ref, ds_scr):
        @pl.when(pl.program_id(0) == 0)
        def _():
            ds_scr[...] = jnp.zeros_like(ds_scr)
            da_ref[...] = jnp.zeros_like(da_ref)
            ddt_ref[...] = jnp.zeros_like(ddt_ref)
            dng_ref[...] = jnp.zeros_like(dng_ref)

        s_list = [s_ref[0, h] for h in range(HEADS)]
        ba_list = [ba_ref[_pair_rows(ci), :] for ci in range(CHUNKS_PER_STEP)]
        _, vjp = jax.vjp(_gdn_chunk, _pair_slices(q_ref), _pair_slices(k_ref), _pair_slices(v_ref), ba_list,
                         _pair_slices(gz_ref), s_list, a_ref[...], dt_ref[...], ng_ref[...])
        ds_list = [ds_scr[h] for h in range(HEADS)]
        dq, dk, dv, dba, dgz, ds_in, da, ddt, dng = vjp((_pair_slices(do_ref), ds_list))
        for ci in range(CHUNKS_PER_STEP):
            for h in range(HEADS):
                p = ci * HEADS + h
                dqkv_ref[_pair_rows(ci), _pair_lanes(h)] = dq[p]
                dqkv_ref[_pair_rows(ci), _pair_lanes(h, D)] = dk[p]
                dqkv_ref[_pair_rows(ci), _pair_lanes(h, 2 * D)] = dv[p]
                dgz_ref[_pair_rows(ci), _pair_lanes(h)] = dgz[p].astype(BF16)
            dba_ref[_pair_rows(ci), :] = dba[ci].astype(BF16)
        for h in range(HEADS):
            ds_scr[h] = ds_in[h]
        da_ref[0:1, :] += da
        ddt_ref[0:1, :] += ddt
        dng_ref[0:1, :] += dng

    rev = lambda i: n_steps - 1 - i
    row = pl.BlockSpec((1, HEAD_DIM), lambda i: (0, 0))
    acc = pl.BlockSpec((8, HEAD_DIM), lambda i: (0, 0))
    outs = pl.pallas_call(
        body, name="gdn_chunk_bwd", grid=(n_steps,),
        in_specs=[pl.BlockSpec((rows, D), lambda i: (rev(i), 0)), pl.BlockSpec((rows, D), lambda i: (rev(i), 1)),
                  pl.BlockSpec((rows, D), lambda i: (rev(i), 2)),
                  pl.BlockSpec((rows, HEAD_DIM), lambda i: (rev(i), 0)),
                  pl.BlockSpec((rows, D), lambda i: (rev(i), 0)),
                  pl.BlockSpec((1, HEADS, HEAD_DIM, HEAD_DIM), lambda i: (rev(i), 0, 0, 0)),
                  pl.BlockSpec((rows, D), lambda i: (rev(i), 0)), row, row, row],
        out_specs=[pl.BlockSpec((rows, 3 * D), lambda i: (rev(i), 0)),
                   pl.BlockSpec((rows, HEAD_DIM), lambda i: (rev(i), 0)),
                   pl.BlockSpec((rows, D), lambda i: (rev(i), 0)), acc, acc, acc],
        out_shape=[jax.ShapeDtypeStruct((t, 3 * D), F32)]
        + [jax.ShapeDtypeStruct((t, HEAD_DIM), BF16), jax.ShapeDtypeStruct((t, D), BF16)]
        + [jax.ShapeDtypeStruct((8, HEAD_DIM), F32)] * 3,
        scratch_shapes=[pltpu.VMEM((HEADS, HEAD_DIM, HEAD_DIM), F32)],
        compiler_params=_cparams(("arbitrary",)),
    )(qkv_c, qkv_c, qkv_c, proj_ba, proj_gz, s_saved, do_gated, a_row, dt_row, ng_row)
    return outs


def _merge(a_out, o_gated, proj_gate, x, target, w_conf, w_gdn, w_o, pg, pb, tt=256):
    t = x.shape[0]

    def body(a_ref, o_ref, gt_ref, x_ref, y_ref, wc_ref, wg_ref, wo_ref, pg_ref, pb_ref,
             loss_ref, dpg_ref, dpb_ref, dx_ref, dgt_ref, da_ref, do_ref, h_ref, ds_ref, dyc_ref, dyg_ref):
        @pl.when(pl.program_id(0) == 0)
        def _():
            loss_ref[...] = jnp.zeros_like(loss_ref)
            dpg_ref[...] = jnp.zeros_like(dpg_ref)
            dpb_ref[...] = jnp.zeros_like(dpb_ref)

        wc, wg, wo = wc_ref[...], wg_ref[...], wo_ref[...]
        y_conf = _dot(a_ref[...], wc, _NN)
        y_gdn = _dot(o_ref[...], wg, _NN)
        sc = _sigmoid(gt_ref[:, 0:D])
        sg = _sigmoid(gt_ref[:, D:2 * D])
        h = sc * y_conf + sg * y_gdn
        z = DN_ALPHA * x_ref[...] + _dot(h, wo, _NN)
        mu = jnp.mean(z, axis=-1, keepdims=True)
        zc = z - mu
        rstd = lax.rsqrt(jnp.mean(zc * zc, axis=-1, keepdims=True) + LN_EPS)
        xhat = zc * rstd
        gain = pg_ref[...]
        err = xhat * gain + pb_ref[...] - y_ref[...]
        tok = jnp.mean(err * err, axis=-1, keepdims=True)
        loss_ref[...] += 0.5 * jnp.sum(tok, axis=0, keepdims=True)

        dy = err * (1.0 / D)
        dpg_ref[0:1, :] += jnp.sum(dy * xhat, axis=0, keepdims=True)
        dpb_ref[0:1, :] += jnp.sum(dy, axis=0, keepdims=True)
        dxh = dy * gain
        dz = rstd * (dxh - jnp.mean(dxh, axis=-1, keepdims=True)
                     - xhat * jnp.mean(dxh * xhat, axis=-1, keepdims=True))
        dx_ref[...] = DN_ALPHA * dz
        dh = _dot(dz, wo, _NT)
        dyc = dh * sc
        dyg = dh * sg
        dgt_ref[:, 0:D] = (dh * y_conf * sc * (1.0 - sc)).astype(BF16)
        dgt_ref[:, D:2 * D] = (dh * y_gdn * sg * (1.0 - sg)).astype(BF16)
        da_ref[...] = _dot(dyc, wc, _NT)
        do_ref[...] = _dot(dyg, wg, _NT)
        h_ref[...] = h.astype(BF16)
        ds_ref[...] = dz.astype(BF16)
        dyc_ref[...] = dyc.astype(BF16)
        dyg_ref[...] = dyg.astype(BF16)

    tile = pl.BlockSpec((tt, D), lambda i: (i, 0))
    wide = pl.BlockSpec((tt, 2 * D), lambda i: (i, 0))
    mat = pl.BlockSpec((D, D), lambda i: (0, 0))
    row = pl.BlockSpec((1, D), lambda i: (0, 0))
    acc = pl.BlockSpec((8, D), lambda i: (0, 0))
    act = lambda dt: jax.ShapeDtypeStruct((t, D), dt)
    return pl.pallas_call(
        body, name="merge", grid=(t // tt,),
        in_specs=[tile, tile, wide, tile, tile, mat, mat, mat, row, row],
        out_specs=[pl.BlockSpec((8, 128), lambda i: (0, 0)), acc, acc, tile, wide, tile, tile, tile, tile, tile, tile],
        out_shape=[jax.ShapeDtypeStruct((8, 128), F32), jax.ShapeDtypeStruct((8, D), F32),
                   jax.ShapeDtypeStruct((8, D), F32), act(F32), jax.ShapeDtypeStruct((t, 2 * D), BF16),
                   act(F32), act(F32), act(BF16), act(BF16), act(BF16), act(BF16)],
        compiler_params=_cparams(("arbitrary",)),
    )(a_out, o_gated, proj_gate, x, target, w_conf, w_gdn, w_o, pg, pb)


def _mesh_place():
    x, y, c = lax.axis_index("x"), lax.axis_index("y"), lax.axis_index("c")
    return x, y, c


def _flat(px, py, pc):
    return 4 * px + 2 * py + pc


def _all_gather(shards):
    n = len(shards)

    def body(*refs):
        ins, outs = refs[:n], refs[n:2 * n]
        send_sems, recv_sems, local_sems = refs[2 * n:]
        x, y, c = _mesh_place()
        me, sibling = (x, y, c), (x, y, 1 - c)
        chips = [(1 - x, y), (x, 1 - y), (1 - x, 1 - y)]

        def copy(a, k, block, to, src=None):
            dst = outs[a].at[_flat(*block)]
            return pltpu.make_async_remote_copy(
                src_ref=dst if src is None else src, dst_ref=dst,
                send_sem=send_sems.at[a, k], recv_sem=recv_sems.at[a, k],
                device_id=to, device_id_type=pl.DeviceIdType.MESH)

        mine = [pltpu.make_async_copy(ins[a], outs[a].at[_flat(*me)], local_sems.at[a]) for a in range(n)]
        for cp in mine:
            cp.start()
        first = []
        for a in range(n):
            first.append(copy(a, 0, me, sibling, src=ins[a]))
            first += [copy(a, 1 + j, me, (*chip, c), src=ins[a]) for j, chip in enumerate(chips)]
        for cp in first:
            cp.start()
        passed = []
        for j, chip in enumerate(chips):
            for a in range(n):
                copy(a, 1 + j, (*chip, c), me).wait_recv()
                fwd = copy(a, 4 + j, (*chip, c), sibling)
                fwd.start()
                passed.append(fwd)
        for a in range(n):
            copy(a, 0, sibling, me).wait_recv()
            for j, chip in enumerate(chips):
                copy(a, 4 + j, (*chip, 1 - c), me).wait_recv()
        for cp in first + passed:
            cp.wait_send()
        for cp in mine:
            cp.wait()

    any_spec = pl.BlockSpec(memory_space=pl.ANY)
    return pl.pallas_call(
        body, name="all_gather_weights",
        in_specs=[any_spec] * n, out_specs=[any_spec] * n,
        out_shape=[jax.ShapeDtypeStruct((N_DEV,) + s.shape, s.dtype) for s in shards],
        scratch_shapes=[pltpu.SemaphoreType.DMA((n, 7)), pltpu.SemaphoreType.DMA((n, 7)),
                        pltpu.SemaphoreType.DMA((n,))],
    )(*shards)


def _exchange_and_grad_x(block_arrays, small, init, segments, tm=512, tk=1024):
    nb = len(block_arrays)
    ns = len(segments)
    m, k1 = init.shape
    widths = [min(tk, a.shape[1]) for a, _ in segments]
    counts = [a.shape[1] // wd for (a, _), wd in zip(segments, widths)]
    starts = [sum(counts[:s]) for s in range(ns)]
    n_j = sum(counts)
    n_i = m // tm

    def body(*refs):
        g_refs, s_ref, i_ref = refs[:nb], refs[nb], refs[nb + 1]
        seg_refs = refs[nb + 2:nb + 2 + 2 * ns]
        outs = refs[nb + 2 + 2 * ns:]
        land_refs, sall_ref, o_ref = outs[:nb], outs[nb], outs[nb + 1]
        send_sems, recv_sems, local_sems = outs[nb + 2:]
        i, j = pl.program_id(0), pl.program_id(1)

        def copies(with_arrivals):
            x, y, c = _mesh_place()
            me = _flat(x, y, c)
            mine = [pltpu.make_async_copy(g_refs[a].at[me], land_refs[a].at[me], local_sems.at[a]) for a in range(nb)]
            mine.append(pltpu.make_async_copy(s_ref, sall_ref.at[me], local_sems.at[nb]))
            sends, recvs = [], []
            for k in range(7):
                mask = k + 1
                px = 1 - x if mask & 4 else x
                py = 1 - y if mask & 2 else y
                pc = 1 - c if mask & 1 else c
                peer = _flat(px, py, pc)
                for a in range(nb + 1):
                    kw = dict(send_sem=send_sems.at[a, k], recv_sem=recv_sems.at[a, k],
                              device_id=(px, py, pc), device_id_type=pl.DeviceIdType.MESH)
                    src = g_refs[a].at[peer] if a < nb else s_ref
                    land = land_refs[a] if a < nb else sall_ref
                    sends.append(pltpu.make_async_remote_copy(src_ref=src, dst_ref=land.at[me], **kw))
                    if with_arrivals:
                        recvs.append(pltpu.make_async_remote_copy(src_ref=src, dst_ref=land.at[peer], **kw))
            return mine, sends, recvs

        @pl.when((i == 0) & (j == 0))
        def _():
            mine, sends, _ = copies(False)
            for cp in mine + sends:
                cp.start()

        @pl.when(j == 0)
        def _():
            o_ref[...] = i_ref[...]

        for s in range(ns):
            @pl.when((j >= starts[s]) & (j < starts[s] + counts[s]))
            def _(s=s):
                o_ref[...] += lax.dot_general(seg_refs[2 * s][...], seg_refs[2 * s + 1][...], (_NT, ((), ())),
                                              preferred_element_type=F32)

        @pl.when((i == n_i - 1) & (j == n_j - 1))
        def _():
            mine, sends, recvs = copies(True)
            for cp in recvs:
                cp.wait_recv()
            for cp in sends:
                cp.wait_send()
            for cp in mine:
                cp.wait()

    any_spec = pl.BlockSpec(memory_space=pl.ANY)
    seg_specs = []
    for s in range(ns):
        col = lambda i, j, s=s: jnp.clip(j - starts[s], 0, counts[s] - 1)
        seg_specs.append(pl.BlockSpec((tm, widths[s]), lambda i, j, col=col: (i, col(i, j))))
        seg_specs.append(pl.BlockSpec((k1, widths[s]), lambda i, j, col=col: (0, col(i, j))))
    tile = pl.BlockSpec((tm, k1), lambda i, j: (i, 0))
    outs = pl.pallas_call(
        body, name="exchange_grads_and_grad_x", grid=(n_i, n_j),
        in_specs=[any_spec] * (nb + 1) + [tile] + seg_specs,
        out_specs=[any_spec] * (nb + 1) + [tile],
        out_shape=[jax.ShapeDtypeStruct(b.shape, b.dtype) for b in block_arrays]
        + [jax.ShapeDtypeStruct((N_DEV,) + small.shape, small.dtype), jax.ShapeDtypeStruct((m, k1), F32)],
        scratch_shapes=[pltpu.SemaphoreType.DMA((nb + 1, 7)), pltpu.SemaphoreType.DMA((nb + 1, 7)),
                        pltpu.SemaphoreType.DMA((nb + 1,))],
        compiler_params=_cparams(("arbitrary", "arbitrary")),
    )(*block_arrays, small, init, *[r for seg in segments for r in seg])
    return outs[:nb], outs[nb], outs[nb + 1]


def _adamw(parts, w, m, v, name, tile):
    rows, cols = w.shape

    def body(p_ref, w_ref, m_ref, v_ref, g_ref, d_ref, nm_ref, nv_ref):
        g = p_ref[0].astype(F32)
        for s in range(1, N_DEV):
            g = g + p_ref[s].astype(F32)
        nm = ADAM_B1 * m_ref[...] + (1.0 - ADAM_B1) * g
        nv = ADAM_B2 * v_ref[...] + (1.0 - ADAM_B2) * jnp.square(g)
        m_hat = nm / (1.0 - ADAM_B1 ** ADAM_STEP)
        v_hat = nv / (1.0 - ADAM_B2 ** ADAM_STEP)
        g_ref[...] = g
        d_ref[...] = -ADAM_LR * (m_hat / (jnp.sqrt(v_hat) + ADAM_EPS) + ADAM_WD * w_ref[...])
        nm_ref[...] = nm
        nv_ref[...] = nv

    blk = pl.BlockSpec((tile, cols), lambda i: (i, 0))
    out = jax.ShapeDtypeStruct((rows, cols), F32)
    return pl.pallas_call(
        body, name=name, grid=(rows // tile,),
        in_specs=[pl.BlockSpec((N_DEV, tile, cols), lambda i: (0, i, 0)), blk, blk, blk],
        out_specs=[blk, blk, blk, blk], out_shape=[out, out, out, out],
        compiler_params=_cparams(("parallel",)),
    )(parts, w, m, v)


def _rows_of(flat, n_rows):
    flat = flat.reshape(-1)
    return jnp.pad(flat, (0, n_rows * D - flat.shape[0])).reshape(n_rows, D)


def _pack_shards(conf_w_out, gdn_w_out, w_o, conf_dw_w, gdn_conv_w):
    return jnp.concatenate([conf_w_out, gdn_w_out, w_o, _rows_of(conf_dw_w, 16), _rows_of(gdn_conv_w, 16)], axis=0)


def _unpack_shards(p):
    dw = p[ROW_DW:ROW_DW + 4].reshape(-1)[:K_CONF * 128].reshape(K_CONF, 128)
    gc = p[ROW_GC:ROW_GC + 2].reshape(-1)[:K_GDN * 384].reshape(K_GDN, 384)
    return p[ROW_CWO:ROW_CWO + 128], p[ROW_GWO:ROW_GWO + 128], p[ROW_WO:ROW_WO + 128], dw, gc


def _pack_small(dw_b, ln_g, ln_b, pg, pb, ng, a_log, dt_bias, loss=None):
    s = jnp.zeros((SMALL_ROWS, D), F32)
    for r, val in enumerate((dw_b, ln_g, ln_b, pg, pb, ng, a_log, dt_bias)):
        s = s.at[r, :val.shape[0]].set(val)
    if loss is not None:
        s = s.at[8, 0].set(loss)
    return s


def _unpack_small(s):
    return (s[0], s[1], s[2], s[3], s[4], s[5, :HEAD_DIM], s[6, :HEADS], s[7, :HEADS])


def _scatter_blocks(g_cwo, g_gwo, g_wo, g_dw, g_gc):
    dw = g_dw.reshape(K_CONF, N_DEV, 128).transpose(1, 0, 2).reshape(N_DEV, K_CONF * 128)
    dw = jnp.pad(dw, ((0, 0), (0, 16 * D - K_CONF * 128))).reshape(N_DEV, 16, D)
    gc = g_gc.reshape(K_GDN, N_DEV, 384).transpose(1, 0, 2).reshape(N_DEV, K_GDN * 384)
    gc = jnp.pad(gc, ((0, 0), (0, 16 * D - K_GDN * 384))).reshape(N_DEV, 16, D)
    return jnp.concatenate([g_cwo.reshape(N_DEV, 128, D), g_gwo.reshape(N_DEV, 128, D),
                            g_wo.reshape(N_DEV, 128, D), dw, gc], axis=1)


def kernel(x, w_in, conf_dw_w, conf_dw_b, conf_ln_g, conf_ln_b, conf_w_out, gdn_conv_w, gdn_A_log, gdn_dt_bias, gdn_norm_g, gdn_w_out, w_o, post_ln_g, post_ln_b, loss_target, m_w_in, m_conf_dw_w, m_conf_dw_b, m_conf_ln_g, m_conf_ln_b, m_conf_w_out, m_gdn_conv_w, m_gdn_A_log, m_gdn_dt_bias, m_gdn_norm_g, m_gdn_w_out, m_w_o, m_post_ln_g, m_post_ln_b, v_w_in, v_conf_dw_w, v_conf_dw_b, v_conf_ln_g, v_conf_ln_b, v_conf_w_out, v_gdn_conv_w, v_gdn_A_log, v_gdn_dt_bias, v_gdn_norm_g, v_gdn_w_out, v_w_o, v_post_ln_g, v_post_ln_b):
    t = x.shape[1]
    x2 = x.reshape(t, D)
    target = loss_target.reshape(t, D)
    x_bf = x2.astype(BF16)

    w_pack = _pack_shards(conf_w_out, gdn_w_out, w_o, conf_dw_w, gdn_conv_w)
    convw = jnp.concatenate([_rows_of(conf_dw_w, 8), _rows_of(gdn_conv_w, 8)], axis=0)
    all_w_in, all_w, all_convw = _all_gather([w_in.astype(BF16), w_pack.astype(BF16), convw])
    w_full = all_w_in.transpose(1, 0, 2).reshape(D, W_IN_COLS)
    w_conf = w_full[:, 0:3 * D]
    w_qkv = w_full[:, 3 * D:6 * D]
    w_gz = w_full[:, 6 * D:7 * D]
    w_ba = jnp.pad(w_full[:, 7 * D:7 * D + 2 * HEADS], ((0, 0), (0, HEAD_DIM - 2 * HEADS)))
    w_gate = w_full[:, 7 * D + 2 * HEADS:]
    cwo_full = all_w[:, ROW_CWO:ROW_CWO + 128].reshape(D, D)
    gwo_full = all_w[:, ROW_GWO:ROW_GWO + 128].reshape(D, D)
    wo_full = all_w[:, ROW_WO:ROW_WO + 128].reshape(D, D)
    dw_full = all_convw[:, 0:4].reshape(N_DEV, 4 * D)[:, :K_CONF * 128].reshape(N_DEV, K_CONF, 128)
    dw_full = jnp.pad(dw_full.transpose(1, 0, 2).reshape(K_CONF, D), ((0, 32 - K_CONF), (0, 0)))
    gc_full = all_convw[:, 8:10].reshape(N_DEV, 2 * D)[:, :K_GDN * 384].reshape(N_DEV, K_GDN, 384)
    gc_full = jnp.pad(gc_full.transpose(1, 0, 2).reshape(K_GDN, 3 * D), ((0, 8 - K_GDN), (0, 0)))

    row = lambda vec: vec.reshape(1, -1)
    lane_row = lambda vec, at: jnp.zeros((1, HEAD_DIM), F32).at[0, at:at + vec.shape[0]].set(vec)
    a_row = lane_row(gdn_A_log, HEADS)
    dt_row = lane_row(gdn_dt_bias, HEADS)
    ng_row = row(gdn_norm_g)

    proj_conf = _matmul_nn(x_bf, w_conf, "proj_conf")
    proj_qkv = _matmul_nn(x_bf, w_qkv, "proj_qkv")
    proj_gz = _matmul_nn(x_bf, w_gz, "proj_gz")
    proj_gate = _matmul_nn(x_bf, w_gate, "proj_gate")
    proj_ba = _matmul_nn(x_bf, w_ba, "proj_ba")
    cpre, a_out = _conf_fwd(proj_conf, dw_full, row(conf_dw_b), row(conf_ln_g), row(conf_ln_b))
    qkv_c = _gdn_conv_fwd(proj_qkv, gc_full)
    o_gated, s_saved = _gdn_chunk_fwd(qkv_c, proj_ba, proj_gz, a_row, dt_row, ng_row)

    (loss_acc, d_pg, d_pb, dx, d_gate, da_out, do_gated, h_bf, dsub_bf, dyc_bf, dyg_bf) = _merge(
        a_out, o_gated, proj_gate, x2, target, cwo_full, gwo_full, wo_full, row(post_ln_g), row(post_ln_b))
    g_wo = _matmul_tn(h_bf, dsub_bf, "grad_w_o")
    g_cwo = _matmul_tn(a_out, dyc_bf, "grad_conf_w_out")
    g_gwo = _matmul_tn(o_gated, dyg_bf, "grad_gdn_w_out")

    dqkv_c, d_ba, d_gz, d_a_row, d_dt_row, d_ng_row = _gdn_chunk_bwd(
        qkv_c, proj_ba, proj_gz, s_saved, do_gated, a_row, dt_row, ng_row)
    d_qkv, g_gc = _gdn_conv_bwd(dqkv_c, proj_qkv, gc_full)

    dcpre, dcz, d_ln_g, d_ln_b = _conf_bwd_post(cpre, proj_conf, da_out, row(conf_ln_g), row(conf_ln_b))
    d_conf, g_dw, g_dwb = _conf_bwd_conv(dcpre, proj_conf, dcz, dw_full)

    segments = [(d_conf, w_conf, "conf"), (d_qkv, w_qkv, "qkv"), (d_gz, w_gz, "gz"),
                (d_ba, w_ba, "ba"), (d_gate, w_gate, "gate")]
    g_cols = {tag: _matmul_tn(x_bf, d_seg, "grad_w_in_" + tag) for d_seg, _, tag in segments}
    g_w_in = jnp.concatenate([g_cols["conf"], g_cols["qkv"], g_cols["gz"], g_cols["ba"][:, :2 * HEADS],
                              g_cols["gate"]], axis=1)

    w_in_blocks = g_w_in.reshape(D, N_DEV, W_IN_SHARD).transpose(1, 0, 2).astype(BF16)
    blocks = _scatter_blocks(g_cwo, g_gwo, g_wo, g_dw[:K_CONF], g_gc[:K_GDN]).astype(BF16)
    small = _pack_small(g_dwb[0], d_ln_g[0], d_ln_b[0], d_pg[0], d_pb[0], d_ng_row[0],
                        d_a_row[0, HEADS:2 * HEADS], d_dt_row[0, HEADS:2 * HEADS], loss_acc[0, 0])
    (landed_w_in, landed), small_all, dx = _exchange_and_grad_x(
        [w_in_blocks, blocks], small, dx, [(d_seg, w_seg) for d_seg, w_seg, _ in segments])

    m_pack = _pack_shards(m_conf_w_out, m_gdn_w_out, m_w_o, m_conf_dw_w, m_gdn_conv_w)
    v_pack = _pack_shards(v_conf_w_out, v_gdn_w_out, v_w_o, v_conf_dw_w, v_gdn_conv_w)
    big_w_in = _adamw(landed_w_in, w_in, m_w_in, v_w_in, "adamw_w_in", W_IN_TILE)
    big = _adamw(landed, w_pack, m_pack, v_pack, "adamw_shards", PACK_TILE)
    ws = _pack_small(conf_dw_b, conf_ln_g, conf_ln_b, post_ln_g, post_ln_b, gdn_norm_g, gdn_A_log, gdn_dt_bias)
    ms = _pack_small(m_conf_dw_b, m_conf_ln_g, m_conf_ln_b, m_post_ln_g, m_post_ln_b, m_gdn_norm_g, m_gdn_A_log,
                     m_gdn_dt_bias)
    vs = _pack_small(v_conf_dw_b, v_conf_ln_g, v_conf_ln_b, v_post_ln_g, v_post_ln_b, v_gdn_norm_g, v_gdn_A_log,
                     v_gdn_dt_bias)
    sml = _adamw(small_all, ws, ms, vs, "adamw_replicated", SMALL_ROWS)

    loss = sml[0][8, 0]
    outs = []
    for b_w_in, big_k, sml_k in zip(big_w_in, big, sml):
        b_cwo, b_gwo, b_wo, b_dw, b_gc = _unpack_shards(big_k)
        s_dwb, s_lng, s_lnb, s_pg, s_pb, s_ng, s_a, s_dt = _unpack_small(sml_k)
        outs.append([b_w_in, b_dw, s_dwb, s_lng, s_lnb, b_cwo, b_gc, s_a, s_dt, s_ng, b_gwo, b_wo, s_pg, s_pb])
    return (loss, dx.reshape(1, t, D), *outs[0], *outs[1], *outs[2], *outs[3])
```

```python
import functools

import jax
import jax.numpy as jnp
from jax import lax
from jax.experimental import pallas as pl
from jax.experimental.pallas import tpu as pltpu

F32 = jnp.float32
BF16 = jnp.bfloat16

N_DEV = 8
D = 1024
HEADS = 8
HEAD_DIM = 128
CHUNK = 64
CHUNKS_PER_STEP = 4
K_CONF = 31
K_GDN = 4
HALO_CONF = 32
HALO_GDN = 8
CONV_PIECE = 16
LN_EPS = 1e-5
RMS_EPS = 1e-6
L2_EPS = 1e-6
DN_ALPHA = 2.0 ** 0.25
ADAM_LR = 0.001
ADAM_B1 = 0.9
ADAM_B2 = 0.999
ADAM_EPS = 1e-08
ADAM_WD = 0.01
ADAM_STEP = 10

W_IN_COLS = 9232
W_IN_SHARD = W_IN_COLS // N_DEV
ROW_CWO = 0
ROW_GWO = ROW_CWO + 128
ROW_WO = ROW_GWO + 128
ROW_DW = ROW_WO + 128
ROW_GC = ROW_DW + 16
PACK_ROWS = ROW_GC + 16
PACK_TILE = PACK_ROWS // 2
W_IN_TILE = 128
SMALL_ROWS = 16
CONVW_ROWS = 16

VMEM_LIMIT = 56 * 1024 * 1024

_NN = ((1,), (0,))
_NT = ((1,), (1,))
_TN = ((0,), (0,))


def _cparams(sem=None):
    return pltpu.CompilerParams(dimension_semantics=sem, vmem_limit_bytes=VMEM_LIMIT)


def _dot(a, b, dims, hi=False):
    dn = (dims, ((), ()))
    a_hi = a.astype(BF16)
    b_hi = b.astype(BF16)
    if not hi:
        return lax.dot_general(a_hi, b_hi, dn, preferred_element_type=F32)
    a_lo = (a - a_hi.astype(F32)).astype(BF16)
    b_lo = (b - b_hi.astype(F32)).astype(BF16)
    d = lambda p, q: lax.dot_general(p, q, dn, preferred_element_type=F32)
    return d(a_hi, b_hi) + (d(a_hi, b_lo) + d(a_lo, b_hi))


def _make_mm(kind, hi):
    dims = {"nn": _NN, "nt": _NT, "tn": _TN}[kind]

    @jax.custom_vjp
    def mm(a, b):
        return _dot(a, b, dims, hi)

    def fwd(a, b):
        return _dot(a, b, dims, hi), (a, b)

    def bwd(res, g):
        a, b = res
        if kind == "nn":
            return _dot(g, b, _NT, hi), _dot(a, g, _TN, hi)
        if kind == "nt":
            return _dot(g, b, _NN, hi), _dot(g, a, _TN, hi)
        return _dot(b, g, _NT, hi), _dot(a, g, _NN, hi)

    mm.defvjp(fwd, bwd)
    return mm


_mm_nn = _make_mm("nn", False)
_mm_nt = _make_mm("nt", False)
_mm_tn = _make_mm("tn", False)
_mm_nn_hi = _make_mm("nn", True)
_mm_tn_hi = _make_mm("tn", True)


def _tri_inv_impl(lows):
    c = lows[0].shape[0]
    eye = (lax.broadcasted_iota(jnp.int32, (c, c), 0) == lax.broadcasted_iota(jnp.int32, (c, c), 1)).astype(F32)
    ms = [-low for low in lows]
    ps = [eye + m for m in ms]
    steps = max(c.bit_length() - 2, 0)
    for _ in range(steps):
        ms = [_dot(m, m, _NN) for m in ms]
        ps = [p + _dot(p, m, _NN) for p, m in zip(ps, ms)]
    rs = [eye - p - _dot(low, p, _NN, True) for low, p in zip(lows, ps)]
    return [p + _dot(p, r, _NN, True) for p, r in zip(ps, rs)]


@jax.custom_vjp
def _tri_inv(lows):
    return _tri_inv_impl(lows)


def _tri_inv_fwd(lows):
    xs = _tri_inv_impl(lows)
    return xs, xs


def _tri_inv_bwd(xs, dxs):
    ts = [_dot(x, dx, _TN) for x, dx in zip(xs, dxs)]
    return ([-_dot(t, x, _NT) for t, x in zip(ts, xs)],)


_tri_inv.defvjp(_tri_inv_fwd, _tri_inv_bwd)


def _sigmoid(x):
    return jax.nn.sigmoid(x)


def _silu(x):
    return x * jax.nn.sigmoid(x)


def _softplus(x):
    u = jnp.exp(-jnp.abs(x))
    log1p_u = jnp.where(u < 1e-3, u * (1.0 - u * (0.5 - u * (1.0 / 3.0))), jnp.log(1.0 + u))
    return jnp.maximum(x, 0.0) + log1p_u


def _layernorm(x, g, b):
    mu = jnp.mean(x, axis=-1, keepdims=True)
    xc = x - mu
    var = jnp.mean(xc * xc, axis=-1, keepdims=True)
    return xc * lax.rsqrt(var + LN_EPS) * g + b


def _pick_lane(x, lane):
    idx = lax.broadcasted_iota(jnp.int32, x.shape, 1)
    return jnp.sum(jnp.where(idx == lane, x, 0.0), axis=1, keepdims=True)


def _gdn_chunk(q_list, k_list, v_list, ba_list, gz_list, s_list, a_row, dt_row, ng_row):
    c = ba_list[0].shape[0]
    n_chunks = len(ba_list)
    pairs = [(ci, h) for ci in range(n_chunks) for h in range(HEADS)]
    every = range(len(pairs))
    rows = lax.broadcasted_iota(jnp.int32, (c, c), 0)
    cols = lax.broadcasted_iota(jnp.int32, (c, c), 1)
    causal = rows >= cols
    strict = rows > cols
    tril = causal.astype(F32)
    triu = (rows <= cols).astype(F32)
    last_row = lax.broadcasted_iota(jnp.int32, (c, 1), 0) == c - 1
    sub8 = lax.broadcasted_iota(jnp.int32, (HEADS, c), 0)

    beta_all = [_sigmoid(ba) for ba in ba_list]
    g_all = [-jnp.exp(a_row) * _softplus(ba + dt_row) for ba in ba_list]
    gc_all = [_mm_nn_hi(tril, g) for g in g_all]
    gc_t = [_mm_tn_hi(g, triu)[HEADS:2 * HEADS, :] for g in g_all]

    q = [_silu(a) for a in q_list]
    k = [_silu(a) for a in k_list]
    v = [_silu(a) for a in v_list]
    q = [a * lax.rsqrt(jnp.sum(a * a, axis=-1, keepdims=True) + L2_EPS) * (HEAD_DIM ** -0.5) for a in q]
    k = [a * lax.rsqrt(jnp.sum(a * a, axis=-1, keepdims=True) + L2_EPS) for a in k]
    beta = [_pick_lane(beta_all[ci], h) for ci, h in pairs]
    gc = [_pick_lane(gc_all[ci], HEADS + h) for ci, h in pairs]
    gc_cols = [jnp.sum(jnp.where(sub8 == h, gc_t[ci], 0.0), axis=0, keepdims=True) for ci, h in pairs]
    decay = [jnp.where(causal, jnp.exp(jnp.where(causal, gc[p] - gc_cols[p], 0.0)), 0.0) for p in every]
    kb = [k[p] * beta[p] for p in every]
    low = [jnp.where(strict, _mm_nt(kb[p], k[p]) * decay[p], 0.0) for p in every]
    x = _tri_inv(low)
    eg = [jnp.exp(gc[p]) for p in every]
    u = [_mm_nn(x[p], v[p] * beta[p]) for p in every]
    w = [_mm_nn(x[p], kb[p] * eg[p]) for p in every]
    intra = [_mm_nt(q[p], k[p]) * decay[p] for p in every]
    q_dec = [q[p] * eg[p] for p in every]
    g_last = [jnp.sum(jnp.where(last_row, gc[p], 0.0), axis=0, keepdims=True) for p in every]
    k_dec = [k[p] * jnp.exp(g_last[p] - gc[p]) for p in every]
    s_dec = [jnp.exp(g_last[p]) for p in every]

    o = []
    state = list(s_list)
    for ci in range(n_chunks):
        at = [ci * HEADS + h for h in range(HEADS)]
        v_new = [u[p] - _mm_nn(w[p], state[h]) for h, p in enumerate(at)]
        o += [_mm_nn(q_dec[p], state[h]) + _mm_nn(intra[p], v_new[h]) for h, p in enumerate(at)]
        state = [state[h] * s_dec[p] + _mm_tn(k_dec[p], v_new[h]) for h, p in enumerate(at)]
    o = [a * lax.rsqrt(jnp.mean(a * a, axis=-1, keepdims=True) + RMS_EPS) * ng_row for a in o]
    o = [o[p] * _silu(gz_list[p]) for p in every]
    return o, state


def _conf_post(cpre, cz, g, b):
    return _silu(_layernorm(cpre, g, b)) * _silu(cz)


def _matmul_nn(a, b, name, tm=1024, tn=1024):
    m, k = a.shape
    n = b.shape[1]
    tn = min(tn, n)

    def body(a_ref, b_ref, o_ref):
        o_ref[...] = jnp.dot(a_ref[...], b_ref[...], preferred_element_type=F32)

    return pl.pallas_call(
        body, name=name, grid=(n // tn, m // tm),
        in_specs=[pl.BlockSpec((tm, k), lambda j, i: (i, 0)), pl.BlockSpec((k, tn), lambda j, i: (0, j))],
        out_specs=pl.BlockSpec((tm, tn), lambda j, i: (i, j)),
        out_shape=jax.ShapeDtypeStruct((m, n), F32),
        compiler_params=_cparams(("parallel", "parallel")),
    )(a, b)


def _matmul_tn(a, b, name, tt=1024, tn=1024):
    t, k1 = a.shape
    n = b.shape[1]
    tn = min(tn, n)
    n_t = t // tt

    def body(a_ref, b_ref, o_ref, acc_ref):
        @pl.when(pl.program_id(1) == 0)
        def _():
            acc_ref[...] = jnp.zeros_like(acc_ref)

        acc_ref[...] += lax.dot_general(a_ref[...], b_ref[...], (_TN, ((), ())), preferred_element_type=F32)

        @pl.when(pl.program_id(1) == n_t - 1)
        def _():
            o_ref[...] = acc_ref[...].astype(BF16)

    return pl.pallas_call(
        body, name=name, grid=(n // tn, n_t),
        in_specs=[pl.BlockSpec((tt, k1), lambda j, i: (i, 0)), pl.BlockSpec((tt, tn), lambda j, i: (i, j))],
        out_specs=pl.BlockSpec((k1, tn), lambda j, i: (0, j)),
        out_shape=jax.ShapeDtypeStruct((k1, n), BF16),
        scratch_shapes=[pltpu.VMEM((k1, tn), F32)],
        compiler_params=_cparams(("parallel", "arbitrary")),
    )(a, b)


def _build_bank(bank_ref, shifts):
    ext = bank_ref[0]
    rows = ext.shape[0]
    for s in shifts:
        if s:
            bank_ref[s] = pltpu.roll(ext, rows - s, axis=0)


def _conv_taps(bank_ref, w_ref, offsets, n_rows, width, emit):
    def piece(rc, carry):
        r0 = pl.multiple_of(rc * CONV_PIECE, CONV_PIECE)
        for cb in range(width // 128):
            lanes = slice(cb * 128, (cb + 1) * 128)
            acc = jnp.zeros((CONV_PIECE, 128), F32)
            for k, off in enumerate(offsets):
                m, s = divmod(off, 8)
                acc = acc + bank_ref[s, pl.ds(r0 + 8 * m, CONV_PIECE), lanes] * w_ref[k:k + 1, lanes]
            emit(r0, lanes, acc)
        return carry

    lax.fori_loop(0, n_rows // CONV_PIECE, piece, 0)


def _conv_dw(bank_ref, d_ref, offsets, n_rows, width, emit):
    ms = [divmod(off, 8) for off in offsets]
    n_taps = len(offsets)
    group = max(1, 32 // n_taps)
    blocks = [slice(cb * 128, (cb + 1) * 128) for cb in range(width // 128)]
    for g0 in range(0, len(blocks), group):
        lane_group = blocks[g0:g0 + group]

        def piece(rc, accs, lane_group=lane_group):
            r0 = pl.multiple_of(rc * 8, 8)
            out = []
            for b, lanes in enumerate(lane_group):
                d = d_ref[pl.ds(r0, 8), lanes]
                out += [accs[b * n_taps + k] + d * bank_ref[s, pl.ds(r0 + 8 * m, 8), lanes]
                        for k, (m, s) in enumerate(ms)]
            return tuple(out)

        init = tuple(jnp.zeros((8, 128), F32) for _ in range(len(lane_group) * n_taps))
        accs = lax.fori_loop(0, n_rows // 8, piece, init)
        for b, lanes in enumerate(lane_group):
            for k in range(n_taps):
                emit(k, lanes, jnp.sum(accs[b * n_taps + k], axis=0, keepdims=True))


def _conf_fwd(proj_conf, dw_w, dw_b, ln_g, ln_b, tt=256):
    t = proj_conf.shape[0]
    hb = tt // HALO_CONF
    offsets = [HALO_CONF - (K_CONF - 1) + k for k in range(K_CONF)]

    def body(cv_ref, cg_ref, cz_ref, cvh_ref, cgh_ref, w_ref, b_ref, g_ref, bb_ref, cpre_ref, aout_ref, bank_ref):
        first = pl.program_id(0) == 0
        halo = cvh_ref[...] * _sigmoid(cgh_ref[...])
        bank_ref[0, 0:HALO_CONF, :] = jnp.where(first, 0.0, halo)
        bank_ref[0, HALO_CONF:, :] = cv_ref[...] * _sigmoid(cg_ref[...])
        _build_bank(bank_ref, range(8))

        def emit(r0, lanes, acc):
            cpre_ref[pl.ds(r0, CONV_PIECE), lanes] = acc + b_ref[0:1, lanes]

        _conv_taps(bank_ref, w_ref, offsets, tt, D, emit)
        aout_ref[...] = _conf_post(cpre_ref[...], cz_ref[...], g_ref[...], bb_ref[...]).astype(BF16)

    row = pl.BlockSpec((1, D), lambda i: (0, 0))
    return pl.pallas_call(
        body, name="conf_fwd", grid=(t // tt,),
        in_specs=[pl.BlockSpec((tt, D), lambda i: (i, 0)), pl.BlockSpec((tt, D), lambda i: (i, 1)),
                  pl.BlockSpec((tt, D), lambda i: (i, 2)),
                  pl.BlockSpec((HALO_CONF, D), lambda i: (jnp.maximum(i * hb - 1, 0), 0)),
                  pl.BlockSpec((HALO_CONF, D), lambda i: (jnp.maximum(i * hb - 1, 0), 1)),
                  pl.BlockSpec((32, D), lambda i: (0, 0)), row, row, row],
        out_specs=[pl.BlockSpec((tt, D), lambda i: (i, 0)), pl.BlockSpec((tt, D), lambda i: (i, 0))],
        out_shape=[jax.ShapeDtypeStruct((t, D), F32), jax.ShapeDtypeStruct((t, D), BF16)],
        scratch_shapes=[pltpu.VMEM((8, tt + HALO_CONF, D), F32)],
        compiler_params=_cparams(("parallel",)),
    )(proj_conf, proj_conf, proj_conf, proj_conf, proj_conf, dw_w, dw_b, ln_g, ln_b)


def _conf_bwd_post(cpre, proj_conf, da_out, ln_g, ln_b, tt=256):
    t = cpre.shape[0]

    def body(c_ref, z_ref, da_ref, g_ref, b_ref, dc_ref, dz_ref, dg_ref, db_ref):
        @pl.when(pl.program_id(0) == 0)
        def _():
            dg_ref[...] = jnp.zeros_like(dg_ref)
            db_ref[...] = jnp.zeros_like(db_ref)

        _, vjp = jax.vjp(_conf_post, c_ref[...], z_ref[...], g_ref[...], b_ref[...])
        dc, dz, dg, db = vjp(da_ref[...])
        dc_ref[...] = dc
        dz_ref[...] = dz.astype(BF16)
        dg_ref[0:1, :] += dg
        db_ref[0:1, :] += db

    row = pl.BlockSpec((1, D), lambda i: (0, 0))
    acc = pl.BlockSpec((8, D), lambda i: (0, 0))
    return pl.pallas_call(
        body, name="conf_bwd_post", grid=(t // tt,),
        in_specs=[pl.BlockSpec((tt, D), lambda i: (i, 0)), pl.BlockSpec((tt, D), lambda i: (i, 2)),
                  pl.BlockSpec((tt, D), lambda i: (i, 0)), row, row],
        out_specs=[pl.BlockSpec((tt, D), lambda i: (i, 0)), pl.BlockSpec((tt, D), lambda i: (i, 0)), acc, acc],
        out_shape=[jax.ShapeDtypeStruct((t, D), F32), jax.ShapeDtypeStruct((t, D), BF16),
                   jax.ShapeDtypeStruct((8, D), F32), jax.ShapeDtypeStruct((8, D), F32)],
        compiler_params=_cparams(("arbitrary",)),
    )(cpre, proj_conf, da_out, ln_g, ln_b)


def _conf_bwd_conv(dcpre, proj_conf, dcz, dw_w, tt=256):
    t = dcpre.shape[0]
    n_tiles = t // tt
    hb = tt // HALO_CONF
    n_hb = t // HALO_CONF
    offsets = [K_CONF - 1 - k for k in range(K_CONF)]

    def body(d_ref, dn_ref, cv_ref, cg_ref, dz_ref, w_ref, dp_ref, dw_ref, db_ref, bank_d, a_scr, da_scr):
        i = pl.program_id(0)

        @pl.when(i == 0)
        def _():
            dw_ref[...] = jnp.zeros_like(dw_ref)
            db_ref[...] = jnp.zeros_like(db_ref)

        cv = cv_ref[...]
        sg = _sigmoid(cg_ref[...])
        a_scr[...] = cv * sg
        bank_d[0, 0:tt, :] = d_ref[...]
        bank_d[0, tt:, :] = jnp.where(i == n_tiles - 1, 0.0, dn_ref[...])
        _build_bank(bank_d, range(8))

        def emit_da(r0, lanes, acc):
            da_scr[pl.ds(r0, CONV_PIECE), lanes] = acc

        _conv_taps(bank_d, w_ref, offsets, tt, D, emit_da)
        da = da_scr[...]
        dp_ref[:, 0:D] = (da * sg).astype(BF16)
        dp_ref[:, D:2 * D] = (da * cv * sg * (1.0 - sg)).astype(BF16)
        dp_ref[:, 2 * D:3 * D] = dz_ref[...]

        def emit_dw(k, lanes, row):
            dw_ref[k:k + 1, lanes] += row

        _conv_dw(bank_d, a_scr, offsets, tt, D, emit_dw)
        db_ref[0:1, :] += jnp.sum(d_ref[...], axis=0, keepdims=True)

    nxt = lambda i: jnp.minimum((i + 1) * hb, n_hb - 1)
    return pl.pallas_call(
        body, name="conf_bwd_conv", grid=(n_tiles,),
        in_specs=[pl.BlockSpec((tt, D), lambda i: (i, 0)), pl.BlockSpec((HALO_CONF, D), lambda i: (nxt(i), 0)),
                  pl.BlockSpec((tt, D), lambda i: (i, 0)), pl.BlockSpec((tt, D), lambda i: (i, 1)),
                  pl.BlockSpec((tt, D), lambda i: (i, 0)), pl.BlockSpec((32, D), lambda i: (0, 0))],
        out_specs=[pl.BlockSpec((tt, 3 * D), lambda i: (i, 0)), pl.BlockSpec((32, D), lambda i: (0, 0)),
                   pl.BlockSpec((8, D), lambda i: (0, 0))],
        out_shape=[jax.ShapeDtypeStruct((t, 3 * D), BF16), jax.ShapeDtypeStruct((32, D), F32),
                   jax.ShapeDtypeStruct((8, D), F32)],
        scratch_shapes=[pltpu.VMEM((8, tt + HALO_CONF, D), F32), pltpu.VMEM((tt, D), F32), pltpu.VMEM((tt, D), F32)],
        compiler_params=_cparams(("arbitrary",)),
    )(dcpre, dcpre, proj_conf, proj_conf, dcz, dw_w)


def _gdn_conv_fwd(proj_qkv, conv_w, tt=256):
    t, width = proj_qkv.shape
    hb = tt // HALO_GDN
    offsets = [HALO_GDN - (K_GDN - 1) + k for k in range(K_GDN)]
    shifts = sorted({off % 8 for off in offsets})

    def body(x_ref, xh_ref, w_ref, o_ref, bank_ref):
        bank_ref[0, 0:HALO_GDN, :] = jnp.where(pl.program_id(1) == 0, 0.0, xh_ref[...])
        bank_ref[0, HALO_GDN:, :] = x_ref[...]
        _build_bank(bank_ref, shifts)

        def emit(r0, lanes, acc):
            o_ref[pl.ds(r0, CONV_PIECE), lanes] = acc

        _conv_taps(bank_ref, w_ref, offsets, tt, D, emit)

    return pl.pallas_call(
        body, name="gdn_conv_fwd", grid=(width // D, t // tt),
        in_specs=[pl.BlockSpec((tt, D), lambda j, i: (i, j)),
                  pl.BlockSpec((HALO_GDN, D), lambda j, i: (jnp.maximum(i * hb - 1, 0), j)),
                  pl.BlockSpec((8, D), lambda j, i: (0, j))],
        out_specs=pl.BlockSpec((tt, D), lambda j, i: (i, j)),
        out_shape=jax.ShapeDtypeStruct((t, width), F32),
        scratch_shapes=[pltpu.VMEM((8, tt + HALO_GDN, D), F32)],
        compiler_params=_cparams(("parallel", "parallel")),
    )(proj_qkv, proj_qkv, conv_w)


def _gdn_conv_bwd(dqkv_c, proj_qkv, conv_w, tt=256):
    t, width = proj_qkv.shape
    n_tiles = t // tt
    hb = tt // HALO_GDN
    n_hb = t // HALO_GDN
    offsets = [K_GDN - 1 - k for k in range(K_GDN)]

    def body(d_ref, dn_ref, x_ref, w_ref, dx_ref, dw_ref, bank_d):
        i = pl.program_id(1)

        @pl.when(i == 0)
        def _():
            dw_ref[...] = jnp.zeros_like(dw_ref)

        bank_d[0, 0:tt, :] = d_ref[...]
        bank_d[0, tt:, :] = jnp.where(i == n_tiles - 1, 0.0, dn_ref[...])
        _build_bank(bank_d, sorted({off % 8 for off in offsets}))

        def emit_dx(r0, lanes, acc):
            dx_ref[pl.ds(r0, CONV_PIECE), lanes] = acc.astype(BF16)

        _conv_taps(bank_d, w_ref, offsets, tt, D, emit_dx)

        def emit_dw(k, lanes, row):
            dw_ref[k:k + 1, lanes] += row

        _conv_dw(bank_d, x_ref, offsets, tt, D, emit_dw)

    return pl.pallas_call(
        body, name="gdn_conv_bwd", grid=(width // D, n_tiles),
        in_specs=[pl.BlockSpec((tt, D), lambda j, i: (i, j)),
                  pl.BlockSpec((HALO_GDN, D), lambda j, i: (jnp.minimum((i + 1) * hb, n_hb - 1), j)),
                  pl.BlockSpec((tt, D), lambda j, i: (i, j)),
                  pl.BlockSpec((8, D), lambda j, i: (0, j))],
        out_specs=[pl.BlockSpec((tt, D), lambda j, i: (i, j)), pl.BlockSpec((8, D), lambda j, i: (0, j))],
        out_shape=[jax.ShapeDtypeStruct((t, width), BF16), jax.ShapeDtypeStruct((8, width), F32)],
        scratch_shapes=[pltpu.VMEM((8, tt + HALO_GDN, D), F32)],
        compiler_params=_cparams(("parallel", "arbitrary")),
    )(dqkv_c, dqkv_c, proj_qkv, conv_w)


def _pair_rows(ci):
    return slice(ci * CHUNK, (ci + 1) * CHUNK)


def _pair_lanes(h, base=0):
    return slice(base + h * HEAD_DIM, base + (h + 1) * HEAD_DIM)


def _pair_slices(ref):
    return [ref[_pair_rows(ci), _pair_lanes(h)] for ci in range(CHUNKS_PER_STEP) for h in range(HEADS)]


def _gdn_chunk_fwd(qkv_c, proj_ba, proj_gz, a_row, dt_row, ng_row):
    t = qkv_c.shape[0]
    rows = CHUNKS_PER_STEP * CHUNK
    n_steps = t // rows

    def body(q_ref, k_ref, v_ref, ba_ref, gz_ref, a_ref, dt_ref, ng_ref, o_ref, ssave_ref, s_scr):
        @pl.when(pl.program_id(0) == 0)
        def _():
            s_scr[...] = jnp.zeros_like(s_scr)

        s_list = [s_scr[h] for h in range(HEADS)]
        for h in range(HEADS):
            ssave_ref[0, h] = s_list[h]
        ba_list = [ba_ref[_pair_rows(ci), :] for ci in range(CHUNKS_PER_STEP)]
        o_list, s_new = _gdn_chunk(_pair_slices(q_ref), _pair_slices(k_ref), _pair_slices(v_ref), ba_list,
                                   _pair_slices(gz_ref), s_list, a_ref[...], dt_ref[...], ng_ref[...])
        for ci in range(CHUNKS_PER_STEP):
            for h in range(HEADS):
                o_ref[_pair_rows(ci), _pair_lanes(h)] = o_list[ci * HEADS + h].astype(BF16)
        for h in range(HEADS):
            s_scr[h] = s_new[h]

    row = pl.BlockSpec((1, HEAD_DIM), lambda i: (0, 0))
    return pl.pallas_call(
        body, name="gdn_chunk_fwd", grid=(n_steps,),
        in_specs=[pl.BlockSpec((rows, D), lambda i: (i, 0)), pl.BlockSpec((rows, D), lambda i: (i, 1)),
                  pl.BlockSpec((rows, D), lambda i: (i, 2)), pl.BlockSpec((rows, HEAD_DIM), lambda i: (i, 0)),
                  pl.BlockSpec((rows, D), lambda i: (i, 0)), row, row, row],
        out_specs=[pl.BlockSpec((rows, D), lambda i: (i, 0)),
                   pl.BlockSpec((1, HEADS, HEAD_DIM, HEAD_DIM), lambda i: (i, 0, 0, 0))],
        out_shape=[jax.ShapeDtypeStruct((t, D), BF16),
                   jax.ShapeDtypeStruct((n_steps, HEADS, HEAD_DIM, HEAD_DIM), F32)],
        scratch_shapes=[pltpu.VMEM((HEADS, HEAD_DIM, HEAD_DIM), F32)],
        compiler_params=_cparams(("arbitrary",)),
    )(qkv_c, qkv_c, qkv_c, proj_ba, proj_gz, a_row, dt_row, ng_row)


def _gdn_chunk_bwd(qkv_c, proj_ba, proj_gz, s_saved, do_gated, a_row, dt_row, ng_row):
    t = qkv_c.shape[0]
    rows = CHUNKS_PER_STEP * CHUNK
    n_steps = t // rows

    def body(q_ref, k_ref, v_ref, ba_ref, gz_ref, s_ref, do_ref, a_ref, dt_ref, ng_ref,
             dqkv_ref, dba_ref, dgz_ref, da_ref, ddt_ref, dng_ref, ds_scr):
        @pl.when(pl.program_id(0) == 0)
        def _():
            ds_scr[...] = jnp.zeros_like(ds_scr)
            da_ref[...] = jnp.zeros_like(da_ref)
            ddt_ref[...] = jnp.zeros_like(ddt_ref)
            dng_ref[...] = jnp.zeros_like(dng_ref)

        s_list = [s_ref[0, h] for h in range(HEADS)]
        ba_list = [ba_ref[_pair_rows(ci), :] for ci in range(CHUNKS_PER_STEP)]
        _, vjp = jax.vjp(_gdn_chunk, _pair_slices(q_ref), _pair_slices(k_ref), _pair_slices(v_ref), ba_list,
                         _pair_slices(gz_ref), s_list, a_ref[...], dt_ref[...], ng_ref[...])
        ds_list = [ds_scr[h] for h in range(HEADS)]
        dq, dk, dv, dba, dgz, ds_in, da, ddt, dng = vjp((_pair_slices(do_ref), ds_list))
        for ci in range(CHUNKS_PER_STEP):
            for h in range(HEADS):
                p = ci * HEADS + h
                dqkv_ref[_pair_rows(ci), _pair_lanes(h)] = dq[p]
                dqkv_ref[_pair_rows(ci), _pair_lanes(h, D)] = dk[p]
                dqkv_ref[_pair_rows(ci), _pair_lanes(h, 2 * D)] = dv[p]
                dgz_ref[_pair_rows(ci), _pair_lanes(h)] = dgz[p].astype(BF16)
            dba_ref[_pair_rows(ci), :] = dba[ci].astype(BF16)
        for h in range(HEADS):
            ds_scr[h] = ds_in[h]
        da_ref[0:1, :] += da
        ddt_ref[0:1, :] += ddt
        dng_ref[0:1, :] += dng

    rev = lambda i: n_steps - 1 - i
    row = pl.BlockSpec((1, HEAD_DIM), lambda i: (0, 0))
    acc = pl.BlockSpec((8, HEAD_DIM), lambda i: (0, 0))
    outs = pl.pallas_call(
        body, name="gdn_chunk_bwd", grid=(n_steps,),
        in_specs=[pl.BlockSpec((rows, D), lambda i: (rev(i), 0)), pl.BlockSpec((rows, D), lambda i: (rev(i), 1)),
                  pl.BlockSpec((rows, D), lambda i: (rev(i), 2)),
                  pl.BlockSpec((rows, HEAD_DIM), lambda i: (rev(i), 0)),
                  pl.BlockSpec((rows, D), lambda i: (rev(i), 0)),
                  pl.BlockSpec((1, HEADS, HEAD_DIM, HEAD_DIM), lambda i: (rev(i), 0, 0, 0)),
                  pl.BlockSpec((rows, D), lambda i: (rev(i), 0)), row, row, row],
        out_specs=[pl.BlockSpec((rows, 3 * D), lambda i: (rev(i), 0)),
                   pl.BlockSpec((rows, HEAD_DIM), lambda i: (rev(i), 0)),
                   pl.BlockSpec((rows, D), lambda i: (rev(i), 0)), acc, acc, acc],
        out_shape=[jax.ShapeDtypeStruct((t, 3 * D), F32)]
        + [jax.ShapeDtypeStruct((t, HEAD_DIM), BF16), jax.ShapeDtypeStruct((t, D), BF16)]
        + [jax.ShapeDtypeStruct((8, HEAD_DIM), F32)] * 3,
        scratch_shapes=[pltpu.VMEM((HEADS, HEAD_DIM, HEAD_DIM), F32)],
        compiler_params=_cparams(("arbitrary",)),
    )(qkv_c, qkv_c, qkv_c, proj_ba, proj_gz, s_saved, do_gated, a_row, dt_row, ng_row)
    return outs


def _merge(a_out, o_gated, proj_gate, x, target, w_conf, w_gdn, w_o, pg, pb, tt=256):
    t = x.shape[0]

    def body(a_ref, o_ref, gt_ref, x_ref, y_ref, wc_ref, wg_ref, wo_ref, pg_ref, pb_ref,
             loss_ref, dpg_ref, dpb_ref, dx_ref, dgt_ref, da_ref, do_ref, h_ref, ds_ref, dyc_ref, dyg_ref):
        @pl.when(pl.program_id(0) == 0)
        def _():
            loss_ref[...] = jnp.zeros_like(loss_ref)
            dpg_ref[...] = jnp.zeros_like(dpg_ref)
            dpb_ref[...] = jnp.zeros_like(dpb_ref)

        wc, wg, wo = wc_ref[...], wg_ref[...], wo_ref[...]
        y_conf = _dot(a_ref[...], wc, _NN)
        y_gdn = _dot(o_ref[...], wg, _NN)
        sc = _sigmoid(gt_ref[:, 0:D])
        sg = _sigmoid(gt_ref[:, D:2 * D])
        h = sc * y_conf + sg * y_gdn
        z = DN_ALPHA * x_ref[...] + _dot(h, wo, _NN)
        mu = jnp.mean(z, axis=-1, keepdims=True)
        zc = z - mu
        rstd = lax.rsqrt(jnp.mean(zc * zc, axis=-1, keepdims=True) + LN_EPS)
        xhat = zc * rstd
        gain = pg_ref[...]
        err = xhat * gain + pb_ref[...] - y_ref[...]
        tok = jnp.mean(err * err, axis=-1, keepdims=True)
        loss_ref[...] += 0.5 * jnp.sum(tok, axis=0, keepdims=True)

        dy = err * (1.0 / D)
        dpg_ref[0:1, :] += jnp.sum(dy * xhat, axis=0, keepdims=True)
        dpb_ref[0:1, :] += jnp.sum(dy, axis=0, keepdims=True)
        dxh = dy * gain
        dz = rstd * (dxh - jnp.mean(dxh, axis=-1, keepdims=True)
                     - xhat * jnp.mean(dxh * xhat, axis=-1, keepdims=True))
        dx_ref[...] = DN_ALPHA * dz
        dh = _dot(dz, wo, _NT)
        dyc = dh * sc
        dyg = dh * sg
        dgt_ref[:, 0:D] = (dh * y_conf * sc * (1.0 - sc)).astype(BF16)
        dgt_ref[:, D:2 * D] = (dh * y_gdn * sg * (1.0 - sg)).astype(BF16)
        da_ref[...] = _dot(dyc, wc, _NT)
        do_ref[...] = _dot(dyg, wg, _NT)
        h_ref[...] = h.astype(BF16)
        ds_ref[...] = dz.astype(BF16)
        dyc_ref[...] = dyc.astype(BF16)
        dyg_ref[...] = dyg.astype(BF16)

    tile = pl.BlockSpec((tt, D), lambda i: (i, 0))
    wide = pl.BlockSpec((tt, 2 * D), lambda i: (i, 0))
    mat = pl.BlockSpec((D, D), lambda i: (0, 0))
    row = pl.BlockSpec((1, D), lambda i: (0, 0))
    acc = pl.BlockSpec((8, D), lambda i: (0, 0))
    act = lambda dt: jax.ShapeDtypeStruct((t, D), dt)
    return pl.pallas_call(
        body, name="merge", grid=(t // tt,),
        in_specs=[tile, tile, wide, tile, tile, mat, mat, mat, row, row],
        out_specs=[pl.BlockSpec((8, 128), lambda i: (0, 0)), acc, acc, tile, wide, tile, tile, tile, tile, tile, tile],
        out_shape=[jax.ShapeDtypeStruct((8, 128), F32), jax.ShapeDtypeStruct((8, D), F32),
                   jax.ShapeDtypeStruct((8, D), F32), act(F32), jax.ShapeDtypeStruct((t, 2 * D), BF16),
                   act(F32), act(F32), act(BF16), act(BF16), act(BF16), act(BF16)],
        compiler_params=_cparams(("arbitrary",)),
    )(a_out, o_gated, proj_gate, x, target, w_conf, w_gdn, w_o, pg, pb)


def _mesh_place():
    x, y, c = lax.axis_index("x"), lax.axis_index("y"), lax.axis_index("c")
    return x, y, c


def _flat(px, py, pc):
    return 4 * px + 2 * py + pc


def _all_gather(shards):
    n = len(shards)

    def body(*refs):
        ins, outs = refs[:n], refs[n:2 * n]
        send_sems, recv_sems, local_sems = refs[2 * n:]
        x, y, c = _mesh_place()
        me, sibling = (x, y, c), (x, y, 1 - c)
        chips = [(1 - x, y), (x, 1 - y), (1 - x, 1 - y)]

        def copy(a, k, block, to, src=None):
            dst = outs[a].at[_flat(*block)]
            return pltpu.make_async_remote_copy(
                src_ref=dst if src is None else src, dst_ref=dst,
                send_sem=send_sems.at[a, k], recv_sem=recv_sems.at[a, k],
                device_id=to, device_id_type=pl.DeviceIdType.MESH)

        mine = [pltpu.make_async_copy(ins[a], outs[a].at[_flat(*me)], local_sems.at[a]) for a in range(n)]
        for cp in mine:
            cp.start()
        first = []
        for a in range(n):
            first.append(copy(a, 0, me, sibling, src=ins[a]))
            first += [copy(a, 1 + j, me, (*chip, c), src=ins[a]) for j, chip in enumerate(chips)]
        for cp in first:
            cp.start()
        passed = []
        for j, chip in enumerate(chips):
            for a in range(n):
                copy(a, 1 + j, (*chip, c), me).wait_recv()
                fwd = copy(a, 4 + j, (*chip, c), sibling)
                fwd.start()
                passed.append(fwd)
        for a in range(n):
            copy(a, 0, sibling, me).wait_recv()
            for j, chip in enumerate(chips):
                copy(a, 4 + j, (*chip, 1 - c), me).wait_recv()
        for cp in first + passed:
            cp.wait_send()
        for cp in mine:
            cp.wait()

    any_spec = pl.BlockSpec(memory_space=pl.ANY)
    return pl.pallas_call(
        body, name="all_gather_weights",
        in_specs=[any_spec] * n, out_specs=[any_spec] * n,
        out_shape=[jax.ShapeDtypeStruct((N_DEV,) + s.shape, s.dtype) for s in shards],
        scratch_shapes=[pltpu.SemaphoreType.DMA((n, 7)), pltpu.SemaphoreType.DMA((n, 7)),
                        pltpu.SemaphoreType.DMA((n,))],
    )(*shards)


def _exchange_and_grad_x(block_arrays, small, init, segments, tm=512, tk=1024):
    nb = len(block_arrays)
    ns = len(segments)
    m, k1 = init.shape
    widths = [min(tk, a.shape[1]) for a, _ in segments]
    counts = [a.shape[1] // wd for (a, _), wd in zip(segments, widths)]
    starts = [sum(counts[:s]) for s in range(ns)]
    n_j = sum(counts)
    n_i = m // tm

    def body(*refs):
        g_refs, s_ref, i_ref = refs[:nb], refs[nb], refs[nb + 1]
        seg_refs = refs[nb + 2:nb + 2 + 2 * ns]
        outs = refs[nb + 2 + 2 * ns:]
        land_refs, sall_ref, o_ref = outs[:nb], outs[nb], outs[nb + 1]
        send_sems, recv_sems, local_sems = outs[nb + 2:]
        i, j = pl.program_id(0), pl.program_id(1)

        def copies(with_arrivals):
            x, y, c = _mesh_place()
            me = _flat(x, y, c)
            mine = [pltpu.make_async_copy(g_refs[a].at[me], land_refs[a].at[me], local_sems.at[a]) for a in range(nb)]
            mine.append(pltpu.make_async_copy(s_ref, sall_ref.at[me], local_sems.at[nb]))
            sends, recvs = [], []
            for k in range(7):
                mask = k + 1
                px = 1 - x if mask & 4 else x
                py = 1 - y if mask & 2 else y
                pc = 1 - c if mask & 1 else c
                peer = _flat(px, py, pc)
                for a in range(nb + 1):
                    kw = dict(send_sem=send_sems.at[a, k], recv_sem=recv_sems.at[a, k],
                              device_id=(px, py, pc), device_id_type=pl.DeviceIdType.MESH)
                    src = g_refs[a].at[peer] if a < nb else s_ref
                    land = land_refs[a] if a < nb else sall_ref
                    sends.append(pltpu.make_async_remote_copy(src_ref=src, dst_ref=land.at[me], **kw))
                    if with_arrivals:
                        recvs.append(pltpu.make_async_remote_copy(src_ref=src, dst_ref=land.at[peer], **kw))
            return mine, sends, recvs

        @pl.when((i == 0) & (j == 0))
        def _():
            mine, sends, _ = copies(False)
            for cp in mine + sends:
                cp.start()

        @pl.when(j == 0)
        def _():
            o_ref[...] = i_ref[...]

        for s in range(ns):
            @pl.when((j >= starts[s]) & (j < starts[s] + counts[s]))
            def _(s=s):
                o_ref[...] += lax.dot_general(seg_refs[2 * s][...], seg_refs[2 * s + 1][...], (_NT, ((), ())),
                                              preferred_element_type=F32)

        @pl.when((i == n_i - 1) & (j == n_j - 1))
        def _():
            mine, sends, recvs = copies(True)
            for cp in recvs:
                cp.wait_recv()
            for cp in sends:
                cp.wait_send()
            for cp in mine:
                cp.wait()

    any_spec = pl.BlockSpec(memory_space=pl.ANY)
    seg_specs = []
    for s in range(ns):
        col = lambda i, j, s=s: jnp.clip(j - starts[s], 0, counts[s] - 1)
        seg_specs.append(pl.BlockSpec((tm, widths[s]), lambda i, j, col=col: (i, col(i, j))))
        seg_specs.append(pl.BlockSpec((k1, widths[s]), lambda i, j, col=col: (0, col(i, j))))
    tile = pl.BlockSpec((tm, k1), lambda i, j: (i, 0))
    outs = pl.pallas_call(
        body, name="exchange_grads_and_grad_x", grid=(n_i, n_j),
        in_specs=[any_spec] * (nb + 1) + [tile] + seg_specs,
        out_specs=[any_spec] * (nb + 1) + [tile],
        out_shape=[jax.ShapeDtypeStruct(b.shape, b.dtype) for b in block_arrays]
        + [jax.ShapeDtypeStruct((N_DEV,) + small.shape, small.dtype), jax.ShapeDtypeStruct((m, k1), F32)],
        scratch_shapes=[pltpu.SemaphoreType.DMA((nb + 1, 7)), pltpu.SemaphoreType.DMA((nb + 1, 7)),
                        pltpu.SemaphoreType.DMA((nb + 1,))],
        compiler_params=_cparams(("arbitrary", "arbitrary")),
    )(*block_arrays, small, init, *[r for seg in segments for r in seg])
    return outs[:nb], outs[nb], outs[nb + 1]


def _adamw(parts, w, m, v, name, tile):
    rows, cols = w.shape

    def body(p_ref, w_ref, m_ref, v_ref, g_ref, d_ref, nm_ref, nv_ref):
        g = p_ref[0].astype(F32)
        for s in range(1, N_DEV):
            g = g + p_ref[s].astype(F32)
        nm = ADAM_B1 * m_ref[...] + (1.0 - ADAM_B1) * g
        nv = ADAM_B2 * v_ref[...] + (1.0 - ADAM_B2) * jnp.square(g)
        m_hat = nm / (1.0 - ADAM_B1 ** ADAM_STEP)
        v_hat = nv / (1.0 - ADAM_B2 ** ADAM_STEP)
        g_ref[...] = g
        d_ref[...] = -ADAM_LR * (m_hat / (jnp.sqrt(v_hat) + ADAM_EPS) + ADAM_WD * w_ref[...])
        nm_ref[...] = nm
        nv_ref[...] = nv

    blk = pl.BlockSpec((tile, cols), lambda i: (i, 0))
    out = jax.ShapeDtypeStruct((rows, cols), F32)
    return pl.pallas_call(
        body, name=name, grid=(rows // tile,),
        in_specs=[pl.BlockSpec((N_DEV, tile, cols), lambda i: (0, i, 0)), blk, blk, blk],
        out_specs=[blk, blk, blk, blk], out_shape=[out, out, out, out],
        compiler_params=_cparams(("parallel",)),
    )(parts, w, m, v)


def _rows_of(flat, n_rows):
    flat = flat.reshape(-1)
    return jnp.pad(flat, (0, n_rows * D - flat.shape[0])).reshape(n_rows, D)


def _pack_shards(conf_w_out, gdn_w_out, w_o, conf_dw_w, gdn_conv_w):
    return jnp.concatenate([conf_w_out, gdn_w_out, w_o, _rows_of(conf_dw_w, 16), _rows_of(gdn_conv_w, 16)], axis=0)


def _unpack_shards(p):
    dw = p[ROW_DW:ROW_DW + 4].reshape(-1)[:K_CONF * 128].reshape(K_CONF, 128)
    gc = p[ROW_GC:ROW_GC + 2].reshape(-1)[:K_GDN * 384].reshape(K_GDN, 384)
    return p[ROW_CWO:ROW_CWO + 128], p[ROW_GWO:ROW_GWO + 128], p[ROW_WO:ROW_WO + 128], dw, gc


def _pack_small(dw_b, ln_g, ln_b, pg, pb, ng, a_log, dt_bias, loss=None):
    s = jnp.zeros((SMALL_ROWS, D), F32)
    for r, val in enumerate((dw_b, ln_g, ln_b, pg, pb, ng, a_log, dt_bias)):
        s = s.at[r, :val.shape[0]].set(val)
    if loss is not None:
        s = s.at[8, 0].set(loss)
    return s


def _unpack_small(s):
    return (s[0], s[1], s[2], s[3], s[4], s[5, :HEAD_DIM], s[6, :HEADS], s[7, :HEADS])


def _scatter_blocks(g_cwo, g_gwo, g_wo, g_dw, g_gc):
    dw = g_dw.reshape(K_CONF, N_DEV, 128).transpose(1, 0, 2).reshape(N_DEV, K_CONF * 128)
    dw = jnp.pad(dw, ((0, 0), (0, 16 * D - K_CONF * 128))).reshape(N_DEV, 16, D)
    gc = g_gc.reshape(K_GDN, N_DEV, 384).transpose(1, 0, 2).reshape(N_DEV, K_GDN * 384)
    gc = jnp.pad(gc, ((0, 0), (0, 16 * D - K_GDN * 384))).reshape(N_DEV, 16, D)
    return jnp.concatenate([g_cwo.reshape(N_DEV, 128, D), g_gwo.reshape(N_DEV, 128, D),
                            g_wo.reshape(N_DEV, 128, D), dw, gc], axis=1)


def kernel(x, w_in, conf_dw_w, conf_dw_b, conf_ln_g, conf_ln_b, conf_w_out, gdn_conv_w, gdn_A_log, gdn_dt_bias, gdn_norm_g, gdn_w_out, w_o, post_ln_g, post_ln_b, loss_target, m_w_in, m_conf_dw_w, m_conf_dw_b, m_conf_ln_g, m_conf_ln_b, m_conf_w_out, m_gdn_conv_w, m_gdn_A_log, m_gdn_dt_bias, m_gdn_norm_g, m_gdn_w_out, m_w_o, m_post_ln_g, m_post_ln_b, v_w_in, v_conf_dw_w, v_conf_dw_b, v_conf_ln_g, v_conf_ln_b, v_conf_w_out, v_gdn_conv_w, v_gdn_A_log, v_gdn_dt_bias, v_gdn_norm_g, v_gdn_w_out, v_w_o, v_post_ln_g, v_post_ln_b):
    t = x.shape[1]
    x2 = x.reshape(t, D)
    target = loss_target.reshape(t, D)
    x_bf = x2.astype(BF16)

    w_pack = _pack_shards(conf_w_out, gdn_w_out, w_o, conf_dw_w, gdn_conv_w)
    convw = jnp.concatenate([_rows_of(conf_dw_w, 8), _rows_of(gdn_conv_w, 8)], axis=0)
    all_w_in, all_w, all_convw = _all_gather([w_in.astype(BF16), w_pack.astype(BF16), convw])
    w_full = all_w_in.transpose(1, 0, 2).reshape(D, W_IN_COLS)
    w_conf = w_full[:, 0:3 * D]
    w_qkv = w_full[:, 3 * D:6 * D]
    w_gz = w_full[:, 6 * D:7 * D]
    w_ba = jnp.pad(w_full[:, 7 * D:7 * D + 2 * HEADS], ((0, 0), (0, HEAD_DIM - 2 * HEADS)))
    w_gate = w_full[:, 7 * D + 2 * HEADS:]
    cwo_full = all_w[:, ROW_CWO:ROW_CWO + 128].reshape(D, D)
    gwo_full = all_w[:, ROW_GWO:ROW_GWO + 128].reshape(D, D)
    wo_full = all_w[:, ROW_WO:ROW_WO + 128].reshape(D, D)
    dw_full = all_convw[:, 0:4].reshape(N_DEV, 4 * D)[:, :K_CONF * 128].reshape(N_DEV, K_CONF, 128)
    dw_full = jnp.pad(dw_full.transpose(1, 0, 2).reshape(K_CONF, D), ((0, 32 - K_CONF), (0, 0)))
    gc_full = all_convw[:, 8:10].reshape(N_DEV, 2 * D)[:, :K_GDN * 384].reshape(N_DEV, K_GDN, 384)
    gc_full = jnp.pad(gc_full.transpose(1, 0, 2).reshape(K_GDN, 3 * D), ((0, 8 - K_GDN), (0, 0)))

    row = lambda vec: vec.reshape(1, -1)
    lane_row = lambda vec, at: jnp.zeros((1, HEAD_DIM), F32).at[0, at:at + vec.shape[0]].set(vec)
    a_row = lane_row(gdn_A_log, HEADS)
    dt_row = lane_row(gdn_dt_bias, HEADS)
    ng_row = row(gdn_norm_g)

    proj_conf = _matmul_nn(x_bf, w_conf, "proj_conf")
    proj_qkv = _matmul_nn(x_bf, w_qkv, "proj_qkv")
    proj_gz = _matmul_nn(x_bf, w_gz, "proj_gz")
    proj_gate = _matmul_nn(x_bf, w_gate, "proj_gate")
    proj_ba = _matmul_nn(x_bf, w_ba, "proj_ba")
    cpre, a_out = _conf_fwd(proj_conf, dw_full, row(conf_dw_b), row(conf_ln_g), row(conf_ln_b))
    qkv_c = _gdn_conv_fwd(proj_qkv, gc_full)
    o_gated, s_saved = _gdn_chunk_fwd(qkv_c, proj_ba, proj_gz, a_row, dt_row, ng_row)

    (loss_acc, d_pg, d_pb, dx, d_gate, da_out, do_gated, h_bf, dsub_bf, dyc_bf, dyg_bf) = _merge(
        a_out, o_gated, proj_gate, x2, target, cwo_full, gwo_full, wo_full, row(post_ln_g), row(post_ln_b))
    g_wo = _matmul_tn(h_bf, dsub_bf, "grad_w_o")
    g_cwo = _matmul_tn(a_out, dyc_bf, "grad_conf_w_out")
    g_gwo = _matmul_tn(o_gated, dyg_bf, "grad_gdn_w_out")

    dqkv_c, d_ba, d_gz, d_a_row, d_dt_row, d_ng_row = _gdn_chunk_bwd(
        qkv_c, proj_ba, proj_gz, s_saved, do_gated, a_row, dt_row, ng_row)
    d_qkv, g_gc = _gdn_conv_bwd(dqkv_c, proj_qkv, gc_full)

    dcpre, dcz, d_ln_g, d_ln_b = _conf_bwd_post(cpre, proj_conf, da_out, row(conf_ln_g), row(conf_ln_b))
    d_conf, g_dw, g_dwb = _conf_bwd_conv(dcpre, proj_conf, dcz, dw_full)

    segments = [(d_conf, w_conf, "conf"), (d_qkv, w_qkv, "qkv"), (d_gz, w_gz, "gz"),
                (d_ba, w_ba, "ba"), (d_gate, w_gate, "gate")]
    g_cols = {tag: _matmul_tn(x_bf, d_seg, "grad_w_in_" + tag) for d_seg, _, tag in segments}
    g_w_in = jnp.concatenate([g_cols["conf"], g_cols["qkv"], g_cols["gz"], g_cols["ba"][:, :2 * HEADS],
                              g_cols["gate"]], axis=1)

    w_in_blocks = g_w_in.reshape(D, N_DEV, W_IN_SHARD).transpose(1, 0, 2).astype(BF16)
    blocks = _scatter_blocks(g_cwo, g_gwo, g_wo, g_dw[:K_CONF], g_gc[:K_GDN]).astype(BF16)
    small = _pack_small(g_dwb[0], d_ln_g[0], d_ln_b[0], d_pg[0], d_pb[0], d_ng_row[0],
                        d_a_row[0, HEADS:2 * HEADS], d_dt_row[0, HEADS:2 * HEADS], loss_acc[0, 0])
    (landed_w_in, landed), small_all, dx = _exchange_and_grad_x(
        [w_in_blocks, blocks], small, dx, [(d_seg, w_seg) for d_seg, w_seg, _ in segments])

    m_pack = _pack_shards(m_conf_w_out, m_gdn_w_out, m_w_o, m_conf_dw_w, m_gdn_conv_w)
    v_pack = _pack_shards(v_conf_w_out, v_gdn_w_out, v_w_o, v_conf_dw_w, v_gdn_conv_w)
    big_w_in = _adamw(landed_w_in, w_in, m_w_in, v_w_in, "adamw_w_in", W_IN_TILE)
    big = _adamw(landed, w_pack, m_pack, v_pack, "adamw_shards", PACK_TILE)
    ws = _pack_small(conf_dw_b, conf_ln_g, conf_ln_b, post_ln_g, post_ln_b, gdn_norm_g, gdn_A_log, gdn_dt_bias)
    ms = _pack_small(m_conf_dw_b, m_conf_ln_g, m_conf_ln_b, m_post_ln_g, m_post_ln_b, m_gdn_norm_g, m_gdn_A_log,
                     m_gdn_dt_bias)
    vs = _pack_small(v_conf_dw_b, v_conf_ln_g, v_conf_ln_b, v_post_ln_g, v_post_ln_b, v_gdn_norm_g, v_gdn_A_log,
                     v_gdn_dt_bias)
    sml = _adamw(small_all, ws, ms, vs, "adamw_replicated", SMALL_ROWS)

    loss = sml[0][8, 0]
    outs = []
    for b_w_in, big_k, sml_k in zip(big_w_in, big, sml):
        b_cwo, b_gwo, b_wo, b_dw, b_gc = _unpack_shards(big_k)
        s_dwb, s_lng, s_lnb, s_pg, s_pb, s_ng, s_a, s_dt = _unpack_small(sml_k)
        outs.append([b_w_in, b_dw, s_dwb, s_lng, s_lnb, b_cwo, b_gc, s_a, s_dt, s_ng, b_gwo, b_wo, s_pg, s_pb])
    return (loss, dx.reshape(1, t, D), *outs[0], *outs[1], *outs[2], *outs[3])
```

```python
import functools

import jax
import jax.numpy as jnp
from jax import lax
from jax.experimental import pallas as pl
from jax.experimental.pallas import tpu as pltpu

F32 = jnp.float32
BF16 = jnp.bfloat16

N_DEV = 8
D = 1024
HEADS = 8
HEAD_DIM = 128
CHUNK = 64
CHUNKS_PER_STEP = 4
K_CONF = 31
K_GDN = 4
HALO_CONF = 32
HALO_GDN = 8
CONV_PIECE = 16
LN_EPS = 1e-5
RMS_EPS = 1e-6
L2_EPS = 1e-6
DN_ALPHA = 2.0 ** 0.25
ADAM_LR = 0.001
ADAM_B1 = 0.9
ADAM_B2 = 0.999
ADAM_EPS = 1e-08
ADAM_WD = 0.01
ADAM_STEP = 10

W_IN_COLS = 9232
W_IN_SHARD = W_IN_COLS // N_DEV
ROW_CWO = 0
ROW_GWO = ROW_CWO + 128
ROW_WO = ROW_GWO + 128
ROW_DW = ROW_WO + 128
ROW_GC = ROW_DW + 16
PACK_ROWS = ROW_GC + 16
PACK_TILE = PACK_ROWS // 2
W_IN_TILE = 128
SMALL_ROWS = 16
CONVW_ROWS = 16

VMEM_LIMIT = 56 * 1024 * 1024

_NN = ((1,), (0,))
_NT = ((1,), (1,))
_TN = ((0,), (0,))


def _cparams(sem=None):
    return pltpu.CompilerParams(dimension_semantics=sem, vmem_limit_bytes=VMEM_LIMIT)


def _dot(a, b, dims, hi=False):
    dn = (dims, ((), ()))
    a_hi = a.astype(BF16)
    b_hi = b.astype(BF16)
    if not hi:
        return lax.dot_general(a_hi, b_hi, dn, preferred_element_type=F32)
    a_lo = (a - a_hi.astype(F32)).astype(BF16)
    b_lo = (b - b_hi.astype(F32)).astype(BF16)
    d = lambda p, q: lax.dot_general(p, q, dn, preferred_element_type=F32)
    return d(a_hi, b_hi) + (d(a_hi, b_lo) + d(a_lo, b_hi))


def _make_mm(kind, hi):
    dims = {"nn": _NN, "nt": _NT, "tn": _TN}[kind]

    @jax.custom_vjp
    def mm(a, b):
        return _dot(a, b, dims, hi)

    def fwd(a, b):
        return _dot(a, b, dims, hi), (a, b)

    def bwd(res, g):
        a, b = res
        if kind == "nn":
            return _dot(g, b, _NT, hi), _dot(a, g, _TN, hi)
        if kind == "nt":
            return _dot(g, b, _NN, hi), _dot(g, a, _TN, hi)
        return _dot(b, g, _NT, hi), _dot(a, g, _NN, hi)

    mm.defvjp(fwd, bwd)
    return mm


_mm_nn = _make_mm("nn", False)
_mm_nt = _make_mm("nt", False)
_mm_tn = _make_mm("tn", False)
_mm_nn_hi = _make_mm("nn", True)
_mm_tn_hi = _make_mm("tn", True)


def _tri_inv_impl(lows):
    c = lows[0].shape[0]
    eye = (lax.broadcasted_iota(jnp.int32, (c, c), 0) == lax.broadcasted_iota(jnp.int32, (c, c), 1)).astype(F32)
    ms = [-low for low in lows]
    ps = [eye + m for m in ms]
    steps = max(c.bit_length() - 2, 0)
    for _ in range(steps):
        ms = [_dot(m, m, _NN) for m in ms]
        ps = [p + _dot(p, m, _NN) for p, m in zip(ps, ms)]
    rs = [eye - p - _dot(low, p, _NN, True) for low, p in zip(lows, ps)]
    return [p + _dot(p, r, _NN, True) for p, r in zip(ps, rs)]


@jax.custom_vjp
def _tri_inv(lows):
    return _tri_inv_impl(lows)


def _tri_inv_fwd(lows):
    xs = _tri_inv_impl(lows)
    return xs, xs


def _tri_inv_bwd(xs, dxs):
    ts = [_dot(x, dx, _TN) for x, dx in zip(xs, dxs)]
    return ([-_dot(t, x, _NT) for t, x in zip(ts, xs)],)


_tri_inv.defvjp(_tri_inv_fwd, _tri_inv_bwd)


def _sigmoid(x):
    return jax.nn.sigmoid(x)


def _silu(x):
    return x * jax.nn.sigmoid(x)


def _softplus(x):
    u = jnp.exp(-jnp.abs(x))
    log1p_u = jnp.where(u < 1e-3, u * (1.0 - u * (0.5 - u * (1.0 / 3.0))), jnp.log(1.0 + u))
    return jnp.maximum(x, 0.0) + log1p_u


def _layernorm(x, g, b):
    mu = jnp.mean(x, axis=-1, keepdims=True)
    xc = x - mu
    var = jnp.mean(xc * xc, axis=-1, keepdims=True)
    return xc * lax.rsqrt(var + LN_EPS) * g + b


def _pick_lane(x, lane):
    idx = lax.broadcasted_iota(jnp.int32, x.shape, 1)
    return jnp.sum(jnp.where(idx == lane, x, 0.0), axis=1, keepdims=True)


def _gdn_chunk(q_list, k_list, v_list, ba_list, gz_list, s_list, a_row, dt_row, ng_row):
    c = ba_list[0].shape[0]
    n_chunks = len(ba_list)
    pairs = [(ci, h) for ci in range(n_chunks) for h in range(HEADS)]
    every = range(len(pairs))
    rows = lax.broadcasted_iota(jnp.int32, (c, c), 0)
    cols = lax.broadcasted_iota(jnp.int32, (c, c), 1)
    causal = rows >= cols
    strict = rows > cols
    tril = causal.astype(F32)
    triu = (rows <= cols).astype(F32)
    last_row = lax.broadcasted_iota(jnp.int32, (c, 1), 0) == c - 1
    sub8 = lax.broadcasted_iota(jnp.int32, (HEADS, c), 0)

    beta_all = [_sigmoid(ba) for ba in ba_list]
    g_all = [-jnp.exp(a_row) * _softplus(ba + dt_row) for ba in ba_list]
    gc_all = [_mm_nn_hi(tril, g) for g in g_all]
    gc_t = [_mm_tn_hi(g, triu)[HEADS:2 * HEADS, :] for g in g_all]

    q = [_silu(a) for a in q_list]
    k = [_silu(a) for a in k_list]
    v = [_silu(a) for a in v_list]
    q = [a * lax.rsqrt(jnp.sum(a * a, axis=-1, keepdims=True) + L2_EPS) * (HEAD_DIM ** -0.5) for a in q]
    k = [a * lax.rsqrt(jnp.sum(a * a, axis=-1, keepdims=True) + L2_EPS) for a in k]
    beta = [_pick_lane(beta_all[ci], h) for ci, h in pairs]
    gc = [_pick_lane(gc_all[ci], HEADS + h) for ci, h in pairs]
    gc_cols = [jnp.sum(jnp.where(sub8 == h, gc_t[ci], 0.0), axis=0, keepdims=True) for ci, h in pairs]
    decay = [jnp.where(causal, jnp.exp(jnp.where(causal, gc[p] - gc_cols[p], 0.0)), 0.0) for p in every]
    kb = [k[p] * beta[p] for p in every]
    low = [jnp.where(strict, _mm_nt(kb[p], k[p]) * decay[p], 0.0) for p in every]
    x = _tri_inv(low)
    eg = [jnp.exp(gc[p]) for p in every]
    u = [_mm_nn(x[p], v[p] * beta[p]) for p in every]
    w = [_mm_nn(x[p], kb[p] * eg[p]) for p in every]
    intra = [_mm_nt(q[p], k[p]) * decay[p] for p in every]
    q_dec = [q[p] * eg[p] for p in every]
    g_last = [jnp.sum(jnp.where(last_row, gc[p], 0.0), axis=0, keepdims=True) for p in every]
    k_dec = [k[p] * jnp.exp(g_last[p] - gc[p]) for p in every]
    s_dec = [jnp.exp(g_last[p]) for p in every]

    o = []
    state = list(s_list)
    for ci in range(n_chunks):
        at = [ci * HEADS + h for h in range(HEADS)]
        v_new = [u[p] - _mm_nn(w[p], state[h]) for h, p in enumerate(at)]
        o += [_mm_nn(q_dec[p], state[h]) + _mm_nn(intra[p], v_new[h]) for h, p in enumerate(at)]
        state = [state[h] * s_dec[p] + _mm_tn(k_dec[p], v_new[h]) for h, p in enumerate(at)]
    o = [a * lax.rsqrt(jnp.mean(a * a, axis=-1, keepdims=True) + RMS_EPS) * ng_row for a in o]
    o = [o[p] * _silu(gz_list[p]) for p in every]
    return o, state


def _conf_post(cpre, cz, g, b):
    return _silu(_layernorm(cpre, g, b)) * _silu(cz)


def _matmul_nn(a, b, name, tm=2048, tn=1024):
    m, k = a.shape
    n = b.shape[1]
    tn = min(tn, n)

    def body(a_ref, b_ref, o_ref):
        o_ref[...] = jnp.dot(a_ref[...], b_ref[...], preferred_element_type=F32)

    return pl.pallas_call(
        body, name=name, grid=(n // tn, m // tm),
        in_specs=[pl.BlockSpec((tm, k), lambda j, i: (i, 0)), pl.BlockSpec((k, tn), lambda j, i: (0, j))],
        out_specs=pl.BlockSpec((tm, tn), lambda j, i: (i, j)),
        out_shape=jax.ShapeDtypeStruct((m, n), F32),
        compiler_params=_cparams(("parallel", "parallel")),
    )(a, b)


def _matmul_tn(a, b, name, tt=2048, tn=1024):
    t, k1 = a.shape
    n = b.shape[1]
    tn = min(tn, n)
    n_t = t // tt

    def body(a_ref, b_ref, o_ref, acc_ref):
        @pl.when(pl.program_id(1) == 0)
        def _():
            acc_ref[...] = jnp.zeros_like(acc_ref)

        acc_ref[...] += lax.dot_general(a_ref[...], b_ref[...], (_TN, ((), ())), preferred_element_type=F32)

        @pl.when(pl.program_id(1) == n_t - 1)
        def _():
            o_ref[...] = acc_ref[...].astype(BF16)

    return pl.pallas_call(
        body, name=name, grid=(n // tn, n_t),
        in_specs=[pl.BlockSpec((tt, k1), lambda j, i: (i, 0)), pl.BlockSpec((tt, tn), lambda j, i: (i, j))],
        out_specs=pl.BlockSpec((k1, tn), lambda j, i: (0, j)),
        out_shape=jax.ShapeDtypeStruct((k1, n), BF16),
        scratch_shapes=[pltpu.VMEM((k1, tn), F32)],
        compiler_params=_cparams(("parallel", "arbitrary")),
    )(a, b)


def _build_bank(bank_ref, shifts):
    ext = bank_ref[0]
    rows = ext.shape[0]
    for s in shifts:
        if s:
            bank_ref[s] = pltpu.roll(ext, rows - s, axis=0)


def _conv_taps(bank_ref, w_ref, offsets, n_rows, width, emit):
    def piece(rc, carry):
        r0 = pl.multiple_of(rc * CONV_PIECE, CONV_PIECE)
        for cb in range(width // 128):
            lanes = slice(cb * 128, (cb + 1) * 128)
            acc = jnp.zeros((CONV_PIECE, 128), F32)
            for k, off in enumerate(offsets):
                m, s = divmod(off, 8)
                acc = acc + bank_ref[s, pl.ds(r0 + 8 * m, CONV_PIECE), lanes] * w_ref[k:k + 1, lanes]
            emit(r0, lanes, acc)
        return carry

    lax.fori_loop(0, n_rows // CONV_PIECE, piece, 0)


def _conv_dw(bank_ref, d_ref, offsets, n_rows, width, emit):
    ms = [divmod(off, 8) for off in offsets]
    n_taps = len(offsets)
    group = max(1, 32 // n_taps)
    blocks = [slice(cb * 128, (cb + 1) * 128) for cb in range(width // 128)]
    for g0 in range(0, len(blocks), group):
        lane_group = blocks[g0:g0 + group]

        def piece(rc, accs, lane_group=lane_group):
            r0 = pl.multiple_of(rc * 8, 8)
            out = []
            for b, lanes in enumerate(lane_group):
                d = d_ref[pl.ds(r0, 8), lanes]
                out += [accs[b * n_taps + k] + d * bank_ref[s, pl.ds(r0 + 8 * m, 8), lanes]
                        for k, (m, s) in enumerate(ms)]
            return tuple(out)

        init = tuple(jnp.zeros((8, 128), F32) for _ in range(len(lane_group) * n_taps))
        accs = lax.fori_loop(0, n_rows // 8, piece, init)
        for b, lanes in enumerate(lane_group):
            for k in range(n_taps):
                emit(k, lanes, jnp.sum(accs[b * n_taps + k], axis=0, keepdims=True))


def _conf_fwd(proj_conf, dw_w, dw_b, ln_g, ln_b, tt=256):
    t = proj_conf.shape[0]
    hb = tt // HALO_CONF
    offsets = [HALO_CONF - (K_CONF - 1) + k for k in range(K_CONF)]

    def body(cv_ref, cg_ref, cz_ref, cvh_ref, cgh_ref, w_ref, b_ref, g_ref, bb_ref, cpre_ref, aout_ref, bank_ref):
        first = pl.program_id(0) == 0
        halo = cvh_ref[...] * _sigmoid(cgh_ref[...])
        bank_ref[0, 0:HALO_CONF, :] = jnp.where(first, 0.0, halo)
        bank_ref[0, HALO_CONF:, :] = cv_ref[...] * _sigmoid(cg_ref[...])
        _build_bank(bank_ref, range(8))

        def emit(r0, lanes, acc):
            cpre_ref[pl.ds(r0, CONV_PIECE), lanes] = acc + b_ref[0:1, lanes]

        _conv_taps(bank_ref, w_ref, offsets, tt, D, emit)
        aout_ref[...] = _conf_post(cpre_ref[...], cz_ref[...], g_ref[...], bb_ref[...]).astype(BF16)

    row = pl.BlockSpec((1, D), lambda i: (0, 0))
    return pl.pallas_call(
        body, name="conf_fwd", grid=(t // tt,),
        in_specs=[pl.BlockSpec((tt, D), lambda i: (i, 0)), pl.BlockSpec((tt, D), lambda i: (i, 1)),
                  pl.BlockSpec((tt, D), lambda i: (i, 2)),
                  pl.BlockSpec((HALO_CONF, D), lambda i: (jnp.maximum(i * hb - 1, 0), 0)),
                  pl.BlockSpec((HALO_CONF, D), lambda i: (jnp.maximum(i * hb - 1, 0), 1)),
                  pl.BlockSpec((32, D), lambda i: (0, 0)), row, row, row],
        out_specs=[pl.BlockSpec((tt, D), lambda i: (i, 0)), pl.BlockSpec((tt, D), lambda i: (i, 0))],
        out_shape=[jax.ShapeDtypeStruct((t, D), F32), jax.ShapeDtypeStruct((t, D), BF16)],
        scratch_shapes=[pltpu.VMEM((8, tt + HALO_CONF, D), F32)],
        compiler_params=_cparams(("parallel",)),
    )(proj_conf, proj_conf, proj_conf, proj_conf, proj_conf, dw_w, dw_b, ln_g, ln_b)


def _conf_bwd_post(cpre, proj_conf, da_out, ln_g, ln_b, tt=256):
    t = cpre.shape[0]

    def body(c_ref, z_ref, da_ref, g_ref, b_ref, dc_ref, dz_ref, dg_ref, db_ref):
        @pl.when(pl.program_id(0) == 0)
        def _():
            dg_ref[...] = jnp.zeros_like(dg_ref)
            db_ref[...] = jnp.zeros_like(db_ref)

        _, vjp = jax.vjp(_conf_post, c_ref[...], z_ref[...], g_ref[...], b_ref[...])
        dc, dz, dg, db = vjp(da_ref[...])
        dc_ref[...] = dc
        dz_ref[...] = dz.astype(BF16)
        dg_ref[0:1, :] += dg
        db_ref[0:1, :] += db

    row = pl.BlockSpec((1, D), lambda i: (0, 0))
    acc = pl.BlockSpec((8, D), lambda i: (0, 0))
    return pl.pallas_call(
        body, name="conf_bwd_post", grid=(t // tt,),
        in_specs=[pl.BlockSpec((tt, D), lambda i: (i, 0)), pl.BlockSpec((tt, D), lambda i: (i, 2)),
                  pl.BlockSpec((tt, D), lambda i: (i, 0)), row, row],
        out_specs=[pl.BlockSpec((tt, D), lambda i: (i, 0)), pl.BlockSpec((tt, D), lambda i: (i, 0)), acc, acc],
        out_shape=[jax.ShapeDtypeStruct((t, D), F32), jax.ShapeDtypeStruct((t, D), BF16),
                   jax.ShapeDtypeStruct((8, D), F32), jax.ShapeDtypeStruct((8, D), F32)],
        compiler_params=_cparams(("arbitrary",)),
    )(cpre, proj_conf, da_out, ln_g, ln_b)


def _conf_bwd_conv(dcpre, proj_conf, dcz, dw_w, tt=256):
    t = dcpre.shape[0]
    n_tiles = t // tt
    hb = tt // HALO_CONF
    n_hb = t // HALO_CONF
    offsets = [K_CONF - 1 - k for k in range(K_CONF)]

    def body(d_ref, dn_ref, cv_ref, cg_ref, dz_ref, w_ref, dp_ref, dw_ref, db_ref, bank_d, a_scr, da_scr):
        i = pl.program_id(0)

        @pl.when(i == 0)
        def _():
            dw_ref[...] = jnp.zeros_like(dw_ref)
            db_ref[...] = jnp.zeros_like(db_ref)

        cv = cv_ref[...]
        sg = _sigmoid(cg_ref[...])
        a_scr[...] = cv * sg
        bank_d[0, 0:tt, :] = d_ref[...]
        bank_d[0, tt:, :] = jnp.where(i == n_tiles - 1, 0.0, dn_ref[...])
        _build_bank(bank_d, range(8))

        def emit_da(r0, lanes, acc):
            da_scr[pl.ds(r0, CONV_PIECE), lanes] = acc

        _conv_taps(bank_d, w_ref, offsets, tt, D, emit_da)
        da = da_scr[...]
        dp_ref[:, 0:D] = (da * sg).astype(BF16)
        dp_ref[:, D:2 * D] = (da * cv * sg * (1.0 - sg)).astype(BF16)
        dp_ref[:, 2 * D:3 * D] = dz_ref[...]

        def emit_dw(k, lanes, row):
            dw_ref[k:k + 1, lanes] += row

        _conv_dw(bank_d, a_scr, offsets, tt, D, emit_dw)
        db_ref[0:1, :] += jnp.sum(d_ref[...], axis=0, keepdims=True)

    nxt = lambda i: jnp.minimum((i + 1) * hb, n_hb - 1)
    return pl.pallas_call(
        body, name="conf_bwd_conv", grid=(n_tiles,),
        in_specs=[pl.BlockSpec((tt, D), lambda i: (i, 0)), pl.BlockSpec((HALO_CONF, D), lambda i: (nxt(i), 0)),
                  pl.BlockSpec((tt, D), lambda i: (i, 0)), pl.BlockSpec((tt, D), lambda i: (i, 1)),
                  pl.BlockSpec((tt, D), lambda i: (i, 0)), pl.BlockSpec((32, D), lambda i: (0, 0))],
        out_specs=[pl.BlockSpec((tt, 3 * D), lambda i: (i, 0)), pl.BlockSpec((32, D), lambda i: (0, 0)),
                   pl.BlockSpec((8, D), lambda i: (0, 0))],
        out_shape=[jax.ShapeDtypeStruct((t, 3 * D), BF16), jax.ShapeDtypeStruct((32, D), F32),
                   jax.ShapeDtypeStruct((8, D), F32)],
        scratch_shapes=[pltpu.VMEM((8, tt + HALO_CONF, D), F32), pltpu.VMEM((tt, D), F32), pltpu.VMEM((tt, D), F32)],
        compiler_params=_cparams(("arbitrary",)),
    )(dcpre, dcpre, proj_conf, proj_conf, dcz, dw_w)


def _gdn_conv_fwd(proj_qkv, conv_w, tt=256):
    t, width = proj_qkv.shape
    hb = tt // HALO_GDN
    offsets = [HALO_GDN - (K_GDN - 1) + k for k in range(K_GDN)]
    shifts = sorted({off % 8 for off in offsets})

    def body(x_ref, xh_ref, w_ref, o_ref, bank_ref):
        bank_ref[0, 0:HALO_GDN, :] = jnp.where(pl.program_id(1) == 0, 0.0, xh_ref[...])
        bank_ref[0, HALO_GDN:, :] = x_ref[...]
        _build_bank(bank_ref, shifts)

        def emit(r0, lanes, acc):
            o_ref[pl.ds(r0, CONV_PIECE), lanes] = acc

        _conv_taps(bank_ref, w_ref, offsets, tt, D, emit)

    return pl.pallas_call(
        body, name="gdn_conv_fwd", grid=(width // D, t // tt),
        in_specs=[pl.BlockSpec((tt, D), lambda j, i: (i, j)),
                  pl.BlockSpec((HALO_GDN, D), lambda j, i: (jnp.maximum(i * hb - 1, 0), j)),
                  pl.BlockSpec((8, D), lambda j, i: (0, j))],
        out_specs=pl.BlockSpec((tt, D), lambda j, i: (i, j)),
        out_shape=jax.ShapeDtypeStruct((t, width), F32),
        scratch_shapes=[pltpu.VMEM((8, tt + HALO_GDN, D), F32)],
        compiler_params=_cparams(("parallel", "parallel")),
    )(proj_qkv, proj_qkv, conv_w)


def _gdn_conv_bwd(dqkv_c, proj_qkv, conv_w, tt=256):
    t, width = proj_qkv.shape
    n_tiles = t // tt
    hb = tt // HALO_GDN
    n_hb = t // HALO_GDN
    offsets = [K_GDN - 1 - k for k in range(K_GDN)]

    def body(d_ref, dn_ref, x_ref, w_ref, dx_ref, dw_ref, bank_d):
        i = pl.program_id(1)

        @pl.when(i == 0)
        def _():
            dw_ref[...] = jnp.zeros_like(dw_ref)

        bank_d[0, 0:tt, :] = d_ref[...]
        bank_d[0, tt:, :] = jnp.where(i == n_tiles - 1, 0.0, dn_ref[...])
        _build_bank(bank_d, sorted({off % 8 for off in offsets}))

        def emit_dx(r0, lanes, acc):
            dx_ref[pl.ds(r0, CONV_PIECE), lanes] = acc.astype(BF16)

        _conv_taps(bank_d, w_ref, offsets, tt, D, emit_dx)

        def emit_dw(k, lanes, row):
            dw_ref[k:k + 1, lanes] += row

        _conv_dw(bank_d, x_ref, offsets, tt, D, emit_dw)

    return pl.pallas_call(
        body, name="gdn_conv_bwd", grid=(width // D, n_tiles),
        in_specs=[pl.BlockSpec((tt, D), lambda j, i: (i, j)),
                  pl.BlockSpec((HALO_GDN, D), lambda j, i: (jnp.minimum((i + 1) * hb, n_hb - 1), j)),
                  pl.BlockSpec((tt, D), lambda j, i: (i, j)),
                  pl.BlockSpec((8, D), lambda j, i: (0, j))],
        out_specs=[pl.BlockSpec((tt, D), lambda j, i: (i, j)), pl.BlockSpec((8, D), lambda j, i: (0, j))],
        out_shape=[jax.ShapeDtypeStruct((t, width), BF16), jax.ShapeDtypeStruct((8, width), F32)],
        scratch_shapes=[pltpu.VMEM((8, tt + HALO_GDN, D), F32)],
        compiler_params=_cparams(("parallel", "arbitrary")),
    )(dqkv_c, dqkv_c, proj_qkv, conv_w)


def _pair_rows(ci):
    return slice(ci * CHUNK, (ci + 1) * CHUNK)


def _pair_lanes(h, base=0):
    return slice(base + h * HEAD_DIM, base + (h + 1) * HEAD_DIM)


def _pair_slices(ref):
    return [ref[_pair_rows(ci), _pair_lanes(h)] for ci in range(CHUNKS_PER_STEP) for h in range(HEADS)]


def _gdn_chunk_fwd(qkv_c, proj_ba, proj_gz, a_row, dt_row, ng_row):
    t = qkv_c.shape[0]
    rows = CHUNKS_PER_STEP * CHUNK
    n_steps = t // rows

    def body(q_ref, k_ref, v_ref, ba_ref, gz_ref, a_ref, dt_ref, ng_ref, o_ref, ssave_ref, s_scr):
        @pl.when(pl.program_id(0) == 0)
        def _():
            s_scr[...] = jnp.zeros_like(s_scr)

        s_list = [s_scr[h] for h in range(HEADS)]
        for h in range(HEADS):
            ssave_ref[0, h] = s_list[h]
        ba_list = [ba_ref[_pair_rows(ci), :] for ci in range(CHUNKS_PER_STEP)]
        o_list, s_new = _gdn_chunk(_pair_slices(q_ref), _pair_slices(k_ref), _pair_slices(v_ref), ba_list,
                                   _pair_slices(gz_ref), s_list, a_ref[...], dt_ref[...], ng_ref[...])
        for ci in range(CHUNKS_PER_STEP):
            for h in range(HEADS):
                o_ref[_pair_rows(ci), _pair_lanes(h)] = o_list[ci * HEADS + h].astype(BF16)
        for h in range(HEADS):
            s_scr[h] = s_new[h]

    row = pl.BlockSpec((1, HEAD_DIM), lambda i: (0, 0))
    return pl.pallas_call(
        body, name="gdn_chunk_fwd", grid=(n_steps,),
        in_specs=[pl.BlockSpec((rows, D), lambda i: (i, 0)), pl.BlockSpec((rows, D), lambda i: (i, 1)),
                  pl.BlockSpec((rows, D), lambda i: (i, 2)), pl.BlockSpec((rows, HEAD_DIM), lambda i: (i, 0)),
                  pl.BlockSpec((rows, D), lambda i: (i, 0)), row, row, row],
        out_specs=[pl.BlockSpec((rows, D), lambda i: (i, 0)),
                   pl.BlockSpec((1, HEADS, HEAD_DIM, HEAD_DIM), lambda i: (i, 0, 0, 0))],
        out_shape=[jax.ShapeDtypeStruct((t, D), BF16),
                   jax.ShapeDtypeStruct((n_steps, HEADS, HEAD_DIM, HEAD_DIM), F32)],
        scratch_shapes=[pltpu.VMEM((HEADS, HEAD_DIM, HEAD_DIM), F32)],
        compiler_params=_cparams(("arbitrary",)),
    )(qkv_c, qkv_c, qkv_c, proj_ba, proj_gz, a_row, dt_row, ng_row)


def _gdn_chunk_bwd(qkv_c, proj_ba, proj_gz, s_saved, do_gated, a_row, dt_row, ng_row):
    t = qkv_c.shape[0]
    rows = CHUNKS_PER_STEP * CHUNK
    n_steps = t // rows

    def body(q_ref, k_ref, v_ref, ba_ref, gz_ref, s_ref, do_ref, a_ref, dt_ref, ng_ref,
             dqkv_ref, dba_ref, dgz_ref, da_ref, ddt_ref, dng_ref, ds_scr):
        @pl.when(pl.program_id(0) == 0)
        def _():
            ds_scr[...] = jnp.zeros_like(ds_scr)
            da_ref[...] = jnp.zeros_like(da_ref)
            ddt_ref[...] = jnp.zeros_like(ddt_ref)
            dng_ref[...] = jnp.zeros_like(dng_ref)

        s_list = [s_ref[0, h] for h in range(HEADS)]
        ba_list = [ba_ref[_pair_rows(ci), :] for ci in range(CHUNKS_PER_STEP)]
        _, vjp = jax.vjp(_gdn_chunk, _pair_slices(q_ref), _pair_slices(k_ref), _pair_slices(v_ref), ba_list,
                         _pair_slices(gz_ref), s_list, a_ref[...], dt_ref[...], ng_ref[...])
        ds_list = [ds_scr[h] for h in range(HEADS)]
        dq, dk, dv, dba, dgz, ds_in, da, ddt, dng = vjp((_pair_slices(do_ref), ds_list))
        for ci in range(CHUNKS_PER_STEP):
            for h in range(HEADS):
                p = ci * HEADS + h
                dqkv_ref[_pair_rows(ci), _pair_lanes(h)] = dq[p]
                dqkv_ref[_pair_rows(ci), _pair_lanes(h, D)] = dk[p]
                dqkv_ref[_pair_rows(ci), _pair_lanes(h, 2 * D)] = dv[p]
                dgz_ref[_pair_rows(ci), _pair_lanes(h)] = dgz[p].astype(BF16)
            dba_ref[_pair_rows(ci), :] = dba[ci].astype(BF16)
        for h in range(HEADS):
            ds_scr[h] = ds_in[h]
        da_ref[0:1, :] += da
        ddt_ref[0:1, :] += ddt
        dng_ref[0:1, :] += dng

    rev = lambda i: n_steps - 1 - i
    row = pl.BlockSpec((1, HEAD_DIM), lambda i: (0, 0))
    acc = pl.BlockSpec((8, HEAD_DIM), lambda i: (0, 0))
    outs = pl.pallas_call(
        body, name="gdn_chunk_bwd", grid=(n_steps,),
        in_specs=[pl.BlockSpec((rows, D), lambda i: (rev(i), 0)), pl.BlockSpec((rows, D), lambda i: (rev(i), 1)),
                  pl.BlockSpec((rows, D), lambda i: (rev(i), 2)),
                  pl.BlockSpec((rows, HEAD_DIM), lambda i: (rev(i), 0)),
                  pl.BlockSpec((rows, D), lambda i: (rev(i), 0)),
                  pl.BlockSpec((1, HEADS, HEAD_DIM, HEAD_DIM), lambda i: (rev(i), 0, 0, 0)),
                  pl.BlockSpec((rows, D), lambda i: (rev(i), 0)), row, row, row],
        out_specs=[pl.BlockSpec((rows, 3 * D), lambda i: (rev(i), 0)),
                   pl.BlockSpec((rows, HEAD_DIM), lambda i: (rev(i), 0)),
                   pl.BlockSpec((rows, D), lambda i: (rev(i), 0)), acc, acc, acc],
        out_shape=[jax.ShapeDtypeStruct((t, 3 * D), F32)]
        + [jax.ShapeDtypeStruct((t, HEAD_DIM), BF16), jax.ShapeDtypeStruct((t, D), BF16)]
        + [jax.ShapeDtypeStruct((8, HEAD_DIM), F32)] * 3,
        scratch_shapes=[pltpu.VMEM((HEADS, HEAD_DIM, HEAD_DIM), F32)],
        compiler_params=_cparams(("arbitrary",)),
    )(qkv_c, qkv_c, qkv_c, proj_ba, proj_gz, s_saved, do_gated, a_row, dt_row, ng_row)
    return outs


def _merge(a_out, o_gated, proj_gate, x, target, w_conf, w_gdn, w_o, pg, pb, tt=256):
    t = x.shape[0]

    def body(a_ref, o_ref, gt_ref, x_ref, y_ref, wc_ref, wg_ref, wo_ref, pg_ref, pb_ref,
             loss_ref, dpg_ref, dpb_ref, dx_ref, dgt_ref, da_ref, do_ref, h_ref, ds_ref, dyc_ref, dyg_ref):
        @pl.when(pl.program_id(0) == 0)
        def _():
            loss_ref[...] = jnp.zeros_like(loss_ref)
            dpg_ref[...] = jnp.zeros_like(dpg_ref)
            dpb_ref[...] = jnp.zeros_like(dpb_ref)

        wc, wg, wo = wc_ref[...], wg_ref[...], wo_ref[...]
        y_conf = _dot(a_ref[...], wc, _NN)
        y_gdn = _dot(o_ref[...], wg, _NN)
        sc = _sigmoid(gt_ref[:, 0:D])
        sg = _sigmoid(gt_ref[:, D:2 * D])
        h = sc * y_conf + sg * y_gdn
        z = DN_ALPHA * x_ref[...] + _dot(h, wo, _NN)
        mu = jnp.mean(z, axis=-1, keepdims=True)
        zc = z - mu
        rstd = lax.rsqrt(jnp.mean(zc * zc, axis=-1, keepdims=True) + LN_EPS)
        xhat = zc * rstd
        gain = pg_ref[...]
        err = xhat * gain + pb_ref[...] - y_ref[...]
        tok = jnp.mean(err * err, axis=-1, keepdims=True)
        loss_ref[...] += 0.5 * jnp.sum(tok, axis=0, keepdims=True)

        dy = err * (1.0 / D)
        dpg_ref[0:1, :] += jnp.sum(dy * xhat, axis=0, keepdims=True)
        dpb_ref[0:1, :] += jnp.sum(dy, axis=0, keepdims=True)
        dxh = dy * gain
        dz = rstd * (dxh - jnp.mean(dxh, axis=-1, keepdims=True)
                     - xhat * jnp.mean(dxh * xhat, axis=-1, keepdims=True))
        dx_ref[...] = DN_ALPHA * dz
        dh = _dot(dz, wo, _NT)
        dyc = dh * sc
        dyg = dh * sg
        dgt_ref[:, 0:D] = (dh * y_conf * sc * (1.0 - sc)).astype(BF16)
        dgt_ref[:, D:2 * D] = (dh * y_gdn * sg * (1.0 - sg)).astype(BF16)
        da_ref[...] = _dot(dyc, wc, _NT)
        do_ref[...] = _dot(dyg, wg, _NT)
        h_ref[...] = h.astype(BF16)
        ds_ref[...] = dz.astype(BF16)
        dyc_ref[...] = dyc.astype(BF16)
        dyg_ref[...] = dyg.astype(BF16)

    tile = pl.BlockSpec((tt, D), lambda i: (i, 0))
    wide = pl.BlockSpec((tt, 2 * D), lambda i: (i, 0))
    mat = pl.BlockSpec((D, D), lambda i: (0, 0))
    row = pl.BlockSpec((1, D), lambda i: (0, 0))
    acc = pl.BlockSpec((8, D), lambda i: (0, 0))
    act = lambda dt: jax.ShapeDtypeStruct((t, D), dt)
    return pl.pallas_call(
        body, name="merge", grid=(t // tt,),
        in_specs=[tile, tile, wide, tile, tile, mat, mat, mat, row, row],
        out_specs=[pl.BlockSpec((8, 128), lambda i: (0, 0)), acc, acc, tile, wide, tile, tile, tile, tile, tile, tile],
        out_shape=[jax.ShapeDtypeStruct((8, 128), F32), jax.ShapeDtypeStruct((8, D), F32),
                   jax.ShapeDtypeStruct((8, D), F32), act(F32), jax.ShapeDtypeStruct((t, 2 * D), BF16),
                   act(F32), act(F32), act(BF16), act(BF16), act(BF16), act(BF16)],
        compiler_params=_cparams(("arbitrary",)),
    )(a_out, o_gated, proj_gate, x, target, w_conf, w_gdn, w_o, pg, pb)


def _mesh_place():
    x, y, c = lax.axis_index("x"), lax.axis_index("y"), lax.axis_index("c")
    return x, y, c


def _flat(px, py, pc):
    return 4 * px + 2 * py + pc


def _all_gather(shards):
    n = len(shards)

    def body(*refs):
        ins, outs = refs[:n], refs[n:2 * n]
        send_sems, recv_sems, local_sems = refs[2 * n:]
        x, y, c = _mesh_place()
        me, sibling = (x, y, c), (x, y, 1 - c)
        chips = [(1 - x, y), (x, 1 - y), (1 - x, 1 - y)]

        def copy(a, k, block, to, src=None):
            dst = outs[a].at[_flat(*block)]
            return pltpu.make_async_remote_copy(
                src_ref=dst if src is None else src, dst_ref=dst,
                send_sem=send_sems.at[a, k], recv_sem=recv_sems.at[a, k],
                device_id=to, device_id_type=pl.DeviceIdType.MESH)

        mine = [pltpu.make_async_copy(ins[a], outs[a].at[_flat(*me)], local_sems.at[a]) for a in range(n)]
        for cp in mine:
            cp.start()
        first = []
        for a in range(n):
            first.append(copy(a, 0, me, sibling, src=ins[a]))
            first += [copy(a, 1 + j, me, (*chip, c), src=ins[a]) for j, chip in enumerate(chips)]
        for cp in first:
            cp.start()
        passed = []
        for j, chip in enumerate(chips):
            for a in range(n):
                copy(a, 1 + j, (*chip, c), me).wait_recv()
                fwd = copy(a, 4 + j, (*chip, c), sibling)
                fwd.start()
                passed.append(fwd)
        for a in range(n):
            copy(a, 0, sibling, me).wait_recv()
            for j, chip in enumerate(chips):
                copy(a, 4 + j, (*chip, 1 - c), me).wait_recv()
        for cp in first + passed:
            cp.wait_send()
        for cp in mine:
            cp.wait()

    any_spec = pl.BlockSpec(memory_space=pl.ANY)
    return pl.pallas_call(
        body, name="all_gather_weights",
        in_specs=[any_spec] * n, out_specs=[any_spec] * n,
        out_shape=[jax.ShapeDtypeStruct((N_DEV,) + s.shape, s.dtype) for s in shards],
        scratch_shapes=[pltpu.SemaphoreType.DMA((n, 7)), pltpu.SemaphoreType.DMA((n, 7)),
                        pltpu.SemaphoreType.DMA((n,))],
    )(*shards)


def _exchange_and_grad_x(block_arrays, small, init, segments, tm=512, tk=1024):
    nb = len(block_arrays)
    ns = len(segments)
    m, k1 = init.shape
    widths = [min(tk, a.shape[1]) for a, _ in segments]
    counts = [a.shape[1] // wd for (a, _), wd in zip(segments, widths)]
    starts = [sum(counts[:s]) for s in range(ns)]
    n_j = sum(counts)
    n_i = m // tm

    def body(*refs):
        g_refs, s_ref, i_ref = refs[:nb], refs[nb], refs[nb + 1]
        seg_refs = refs[nb + 2:nb + 2 + 2 * ns]
        outs = refs[nb + 2 + 2 * ns:]
        land_refs, sall_ref, o_ref = outs[:nb], outs[nb], outs[nb + 1]
        send_sems, recv_sems, local_sems = outs[nb + 2:]
        i, j = pl.program_id(0), pl.program_id(1)

        def copies(with_arrivals):
            x, y, c = _mesh_place()
            me = _flat(x, y, c)
            mine = [pltpu.make_async_copy(g_refs[a].at[me], land_refs[a].at[me], local_sems.at[a]) for a in range(nb)]
            mine.append(pltpu.make_async_copy(s_ref, sall_ref.at[me], local_sems.at[nb]))
            sends, recvs = [], []
            for k in range(7):
                mask = k + 1
                px = 1 - x if mask & 4 else x
                py = 1 - y if mask & 2 else y
                pc = 1 - c if mask & 1 else c
                peer = _flat(px, py, pc)
                for a in range(nb + 1):
                    kw = dict(send_sem=send_sems.at[a, k], recv_sem=recv_sems.at[a, k],
                              device_id=(px, py, pc), device_id_type=pl.DeviceIdType.MESH)
                    src = g_refs[a].at[peer] if a < nb else s_ref
                    land = land_refs[a] if a < nb else sall_ref
                    sends.append(pltpu.make_async_remote_copy(src_ref=src, dst_ref=land.at[me], **kw))
                    if with_arrivals:
                        recvs.append(pltpu.make_async_remote_copy(src_ref=src, dst_ref=land.at[peer], **kw))
            return mine, sends, recvs

        @pl.when((i == 0) & (j == 0))
        def _():
            mine, sends, _ = copies(False)
            for cp in mine + sends:
                cp.start()

        @pl.when(j == 0)
        def _():
            o_ref[...] = i_ref[...]

        for s in range(ns):
            @pl.when((j >= starts[s]) & (j < starts[s] + counts[s]))
            def _(s=s):
                o_ref[...] += lax.dot_general(seg_refs[2 * s][...], seg_refs[2 * s + 1][...], (_NT, ((), ())),
                                              preferred_element_type=F32)

        @pl.when((i == n_i - 1) & (j == n_j - 1))
        def _():
            mine, sends, recvs = copies(True)
            for cp in recvs:
                cp.wait_recv()
            for cp in sends:
                cp.wait_send()
            for cp in mine:
                cp.wait()

    any_spec = pl.BlockSpec(memory_space=pl.ANY)
    seg_specs = []
    for s in range(ns):
        col = lambda i, j, s=s: jnp.clip(j - starts[s], 0, counts[s] - 1)
        seg_specs.append(pl.BlockSpec((tm, widths[s]), lambda i, j, col=col: (i, col(i, j))))
        seg_specs.append(pl.BlockSpec((k1, widths[s]), lambda i, j, col=col: (0, col(i, j))))
    tile = pl.BlockSpec((tm, k1), lambda i, j: (i, 0))
    outs = pl.pallas_call(
        body, name="exchange_grads_and_grad_x", grid=(n_i, n_j),
        in_specs=[any_spec] * (nb + 1) + [tile] + seg_specs,
        out_specs=[any_spec] * (nb + 1) + [tile],
        out_shape=[jax.ShapeDtypeStruct(b.shape, b.dtype) for b in block_arrays]
        + [jax.ShapeDtypeStruct((N_DEV,) + small.shape, small.dtype), jax.ShapeDtypeStruct((m, k1), F32)],
        scratch_shapes=[pltpu.SemaphoreType.DMA((nb + 1, 7)), pltpu.SemaphoreType.DMA((nb + 1, 7)),
                        pltpu.SemaphoreType.DMA((nb + 1,))],
        compiler_params=_cparams(("arbitrary", "arbitrary")),
    )(*block_arrays, small, init, *[r for seg in segments for r in seg])
    return outs[:nb], outs[nb], outs[nb + 1]


def _adamw(parts, w, m, v, name, tile):
    rows, cols = w.shape

    def body(p_ref, w_ref, m_ref, v_ref, g_ref, d_ref, nm_ref, nv_ref):
        g = p_ref[0].astype(F32)
        for s in range(1, N_DEV):
            g = g + p_ref[s].astype(F32)
        nm = ADAM_B1 * m_ref[...] + (1.0 - ADAM_B1) * g
        nv = ADAM_B2 * v_ref[...] + (1.0 - ADAM_B2) * jnp.square(g)
        m_hat = nm / (1.0 - ADAM_B1 ** ADAM_STEP)
        v_hat = nv / (1.0 - ADAM_B2 ** ADAM_STEP)
        g_ref[...] = g
        d_ref[...] = -ADAM_LR * (m_hat / (jnp.sqrt(v_hat) + ADAM_EPS) + ADAM_WD * w_ref[...])
        nm_ref[...] = nm
        nv_ref[...] = nv

    blk = pl.BlockSpec((tile, cols), lambda i: (i, 0))
    out = jax.ShapeDtypeStruct((rows, cols), F32)
    return pl.pallas_call(
        body, name=name, grid=(rows // tile,),
        in_specs=[pl.BlockSpec((N_DEV, tile, cols), lambda i: (0, i, 0)), blk, blk, blk],
        out_specs=[blk, blk, blk, blk], out_shape=[out, out, out, out],
        compiler_params=_cparams(("parallel",)),
    )(parts, w, m, v)


def _rows_of(flat, n_rows):
    flat = flat.reshape(-1)
    return jnp.pad(flat, (0, n_rows * D - flat.shape[0])).reshape(n_rows, D)


def _pack_shards(conf_w_out, gdn_w_out, w_o, conf_dw_w, gdn_conv_w):
    return jnp.concatenate([conf_w_out, gdn_w_out, w_o, _rows_of(conf_dw_w, 16), _rows_of(gdn_conv_w, 16)], axis=0)


def _unpack_shards(p):
    dw = p[ROW_DW:ROW_DW + 4].reshape(-1)[:K_CONF * 128].reshape(K_CONF, 128)
    gc = p[ROW_GC:ROW_GC + 2].reshape(-1)[:K_GDN * 384].reshape(K_GDN, 384)
    return p[ROW_CWO:ROW_CWO + 128], p[ROW_GWO:ROW_GWO + 128], p[ROW_WO:ROW_WO + 128], dw, gc


def _pack_small(dw_b, ln_g, ln_b, pg, pb, ng, a_log, dt_bias, loss=None):
    s = jnp.zeros((SMALL_ROWS, D), F32)
    for r, val in enumerate((dw_b, ln_g, ln_b, pg, pb, ng, a_log, dt_bias)):
        s = s.at[r, :val.shape[0]].set(val)
    if loss is not None:
        s = s.at[8, 0].set(loss)
    return s


def _unpack_small(s):
    return (s[0], s[1], s[2], s[3], s[4], s[5, :HEAD_DIM], s[6, :HEADS], s[7, :HEADS])


def _scatter_blocks(g_cwo, g_gwo, g_wo, g_dw, g_gc):
    dw = g_dw.reshape(K_CONF, N_DEV, 128).transpose(1, 0, 2).reshape(N_DEV, K_CONF * 128)
    dw = jnp.pad(dw, ((0, 0), (0, 16 * D - K_CONF * 128))).reshape(N_DEV, 16, D)
    gc = g_gc.reshape(K_GDN, N_DEV, 384).transpose(1, 0, 2).reshape(N_DEV, K_GDN * 384)
    gc = jnp.pad(gc, ((0, 0), (0, 16 * D - K_GDN * 384))).reshape(N_DEV, 16, D)
    return jnp.concatenate([g_cwo.reshape(N_DEV, 128, D), g_gwo.reshape(N_DEV, 128, D),
                            g_wo.reshape(N_DEV, 128, D), dw, gc], axis=1)


def kernel(x, w_in, conf_dw_w, conf_dw_b, conf_ln_g, conf_ln_b, conf_w_out, gdn_conv_w, gdn_A_log, gdn_dt_bias, gdn_norm_g, gdn_w_out, w_o, post_ln_g, post_ln_b, loss_target, m_w_in, m_conf_dw_w, m_conf_dw_b, m_conf_ln_g, m_conf_ln_b, m_conf_w_out, m_gdn_conv_w, m_gdn_A_log, m_gdn_dt_bias, m_gdn_norm_g, m_gdn_w_out, m_w_o, m_post_ln_g, m_post_ln_b, v_w_in, v_conf_dw_w, v_conf_dw_b, v_conf_ln_g, v_conf_ln_b, v_conf_w_out, v_gdn_conv_w, v_gdn_A_log, v_gdn_dt_bias, v_gdn_norm_g, v_gdn_w_out, v_w_o, v_post_ln_g, v_post_ln_b):
    t = x.shape[1]
    x2 = x.reshape(t, D)
    target = loss_target.reshape(t, D)
    x_bf = x2.astype(BF16)

    w_pack = _pack_shards(conf_w_out, gdn_w_out, w_o, conf_dw_w, gdn_conv_w)
    convw = jnp.concatenate([_rows_of(conf_dw_w, 8), _rows_of(gdn_conv_w, 8)], axis=0)
    all_w_in, all_w, all_convw = _all_gather([w_in.astype(BF16), w_pack.astype(BF16), convw])
    w_full = all_w_in.transpose(1, 0, 2).reshape(D, W_IN_COLS)
    w_conf = w_full[:, 0:3 * D]
    w_qkv = w_full[:, 3 * D:6 * D]
    w_gz = w_full[:, 6 * D:7 * D]
    w_ba = jnp.pad(w_full[:, 7 * D:7 * D + 2 * HEADS], ((0, 0), (0, HEAD_DIM - 2 * HEADS)))
    w_gate = w_full[:, 7 * D + 2 * HEADS:]
    cwo_full = all_w[:, ROW_CWO:ROW_CWO + 128].reshape(D, D)
    gwo_full = all_w[:, ROW_GWO:ROW_GWO + 128].reshape(D, D)
    wo_full = all_w[:, ROW_WO:ROW_WO + 128].reshape(D, D)
    dw_full = all_convw[:, 0:4].reshape(N_DEV, 4 * D)[:, :K_CONF * 128].reshape(N_DEV, K_CONF, 128)
    dw_full = jnp.pad(dw_full.transpose(1, 0, 2).reshape(K_CONF, D), ((0, 32 - K_CONF), (0, 0)))
    gc_full = all_convw[:, 8:10].reshape(N_DEV, 2 * D)[:, :K_GDN * 384].reshape(N_DEV, K_GDN, 384)
    gc_full = jnp.pad(gc_full.transpose(1, 0, 2).reshape(K_GDN, 3 * D), ((0, 8 - K_GDN), (0, 0)))

    row = lambda vec: vec.reshape(1, -1)
    lane_row = lambda vec, at: jnp.zeros((1, HEAD_DIM), F32).at[0, at:at + vec.shape[0]].set(vec)
    a_row = lane_row(gdn_A_log, HEADS)
    dt_row = lane_row(gdn_dt_bias, HEADS)
    ng_row = row(gdn_norm_g)

    proj_conf = _matmul_nn(x_bf, w_conf, "proj_conf")
    proj_qkv = _matmul_nn(x_bf, w_qkv, "proj_qkv")
    proj_gz = _matmul_nn(x_bf, w_gz, "proj_gz")
    proj_gate = _matmul_nn(x_bf, w_gate, "proj_gate")
    proj_ba = _matmul_nn(x_bf, w_ba, "proj_ba")
    cpre, a_out = _conf_fwd(proj_conf, dw_full, row(conf_dw_b), row(conf_ln_g), row(conf_ln_b))
    qkv_c = _gdn_conv_fwd(proj_qkv, gc_full)
    o_gated, s_saved = _gdn_chunk_fwd(qkv_c, proj_ba, proj_gz, a_row, dt_row, ng_row)

    (loss_acc, d_pg, d_pb, dx, d_gate, da_out, do_gated, h_bf, dsub_bf, dyc_bf, dyg_bf) = _merge(
        a_out, o_gated, proj_gate, x2, target, cwo_full, gwo_full, wo_full, row(post_ln_g), row(post_ln_b))
    g_wo = _matmul_tn(h_bf, dsub_bf, "grad_w_o")
    g_cwo = _matmul_tn(a_out, dyc_bf, "grad_conf_w_out")
    g_gwo = _matmul_tn(o_gated, dyg_bf, "grad_gdn_w_out")

    dqkv_c, d_ba, d_gz, d_a_row, d_dt_row, d_ng_row = _gdn_chunk_bwd(
        qkv_c, proj_ba, proj_gz, s_saved, do_gated, a_row, dt_row, ng_row)
    d_qkv, g_gc = _gdn_conv_bwd(dqkv_c, proj_qkv, gc_full)

    dcpre, dcz, d_ln_g, d_ln_b = _conf_bwd_post(cpre, proj_conf, da_out, row(conf_ln_g), row(conf_ln_b))
    d_conf, g_dw, g_dwb = _conf_bwd_conv(dcpre, proj_conf, dcz, dw_full)

    segments = [(d_conf, w_conf, "conf"), (d_qkv, w_qkv, "qkv"), (d_gz, w_gz, "gz"),
                (d_ba, w_ba, "ba"), (d_gate, w_gate, "gate")]
    g_cols = {tag: _matmul_tn(x_bf, d_seg, "grad_w_in_" + tag) for d_seg, _, tag in segments}
    g_w_in = jnp.concatenate([g_cols["conf"], g_cols["qkv"], g_cols["gz"], g_cols["ba"][:, :2 * HEADS],
                              g_cols["gate"]], axis=1)

    w_in_blocks = g_w_in.reshape(D, N_DEV, W_IN_SHARD).transpose(1, 0, 2).astype(BF16)
    blocks = _scatter_blocks(g_cwo, g_gwo, g_wo, g_dw[:K_CONF], g_gc[:K_GDN]).astype(BF16)
    small = _pack_small(g_dwb[0], d_ln_g[0], d_ln_b[0], d_pg[0], d_pb[0], d_ng_row[0],
                        d_a_row[0, HEADS:2 * HEADS], d_dt_row[0, HEADS:2 * HEADS], loss_acc[0, 0])
    (landed_w_in, landed), small_all, dx = _exchange_and_grad_x(
        [w_in_blocks, blocks], small, dx, [(d_seg, w_seg) for d_seg, w_seg, _ in segments])

    m_pack = _pack_shards(m_conf_w_out, m_gdn_w_out, m_w_o, m_conf_dw_w, m_gdn_conv_w)
    v_pack = _pack_shards(v_conf_w_out, v_gdn_w_out, v_w_o, v_conf_dw_w, v_gdn_conv_w)
    big_w_in = _adamw(landed_w_in, w_in, m_w_in, v_w_in, "adamw_w_in", W_IN_TILE)
    big = _adamw(landed, w_pack, m_pack, v_pack, "adamw_shards", PACK_TILE)
    ws = _pack_small(conf_dw_b, conf_ln_g, conf_ln_b, post_ln_g, post_ln_b, gdn_norm_g, gdn_A_log, gdn_dt_bias)
    ms = _pack_small(m_conf_dw_b, m_conf_ln_g, m_conf_ln_b, m_post_ln_g, m_post_ln_b, m_gdn_norm_g, m_gdn_A_log,
                     m_gdn_dt_bias)
    vs = _pack_small(v_conf_dw_b, v_conf_ln_g, v_conf_ln_b, v_post_ln_g, v_post_ln_b, v_gdn_norm_g, v_gdn_A_log,
                     v_gdn_dt_bias)
    sml = _adamw(small_all, ws, ms, vs, "adamw_replicated", SMALL_ROWS)

    loss = sml[0][8, 0]
    outs = []
    for b_w_in, big_k, sml_k in zip(big_w_in, big, sml):
        b_cwo, b_gwo, b_wo, b_dw, b_gc = _unpack_shards(big_k)
        s_dwb, s_lng, s_lnb, s_pg, s_pb, s_ng, s_a, s_dt = _unpack_small(sml_k)
        outs.append([b_w_in, b_dw, s_dwb, s_lng, s_lnb, b_cwo, b_gc, s_a, s_dt, s_ng, b_gwo, b_wo, s_pg, s_pb])
    return (loss, dx.reshape(1, t, D), *outs[0], *outs[1], *outs[2], *outs[3])
```

```python
import functools

import jax
import jax.numpy as jnp
from jax import lax
from jax.experimental import pallas as pl
from jax.experimental.pallas import tpu as pltpu

F32 = jnp.float32
BF16 = jnp.bfloat16

N_DEV = 8
D = 1024
HEADS = 8
HEAD_DIM = 128
CHUNK = 64
CHUNKS_PER_STEP = 4
K_CONF = 31
K_GDN = 4
HALO_CONF = 32
HALO_GDN = 8
CONV_PIECE = 16
LN_EPS = 1e-5
RMS_EPS = 1e-6
L2_EPS = 1e-6
DN_ALPHA = 2.0 ** 0.25
ADAM_LR = 0.001
ADAM_B1 = 0.9
ADAM_B2 = 0.999
ADAM_EPS = 1e-08
ADAM_WD = 0.01
ADAM_STEP = 10

W_IN_COLS = 9232
W_IN_SHARD = W_IN_COLS // N_DEV
ROW_CWO = 0
ROW_GWO = ROW_CWO + 128
ROW_WO = ROW_GWO + 128
ROW_DW = ROW_WO + 128
ROW_GC = ROW_DW + 16
PACK_ROWS = ROW_GC + 16
PACK_TILE = PACK_ROWS // 2
W_IN_TILE = 128
SMALL_ROWS = 16
CONVW_ROWS = 16

VMEM_LIMIT = 56 * 1024 * 1024

_NN = ((1,), (0,))
_NT = ((1,), (1,))
_TN = ((0,), (0,))


def _cparams(sem=None):
    return pltpu.CompilerParams(dimension_semantics=sem, vmem_limit_bytes=VMEM_LIMIT)


def _dot(a, b, dims, hi=False):
    dn = (dims, ((), ()))
    a_hi = a.astype(BF16)
    b_hi = b.astype(BF16)
    if not hi:
        return lax.dot_general(a_hi, b_hi, dn, preferred_element_type=F32)
    a_lo = (a - a_hi.astype(F32)).astype(BF16)
    b_lo = (b - b_hi.astype(F32)).astype(BF16)
    d = lambda p, q: lax.dot_general(p, q, dn, preferred_element_type=F32)
    return d(a_hi, b_hi) + (d(a_hi, b_lo) + d(a_lo, b_hi))


def _make_mm(kind, hi):
    dims = {"nn": _NN, "nt": _NT, "tn": _TN}[kind]

    @jax.custom_vjp
    def mm(a, b):
        return _dot(a, b, dims, hi)

    def fwd(a, b):
        return _dot(a, b, dims, hi), (a, b)

    def bwd(res, g):
        a, b = res
        if kind == "nn":
            return _dot(g, b, _NT, hi), _dot(a, g, _TN, hi)
        if kind == "nt":
            return _dot(g, b, _NN, hi), _dot(g, a, _TN, hi)
        return _dot(b, g, _NT, hi), _dot(a, g, _NN, hi)

    mm.defvjp(fwd, bwd)
    return mm


_mm_nn = _make_mm("nn", False)
_mm_nt = _make_mm("nt", False)
_mm_tn = _make_mm("tn", False)
_mm_nn_hi = _make_mm("nn", True)
_mm_tn_hi = _make_mm("tn", True)


def _tri_inv_impl(lows):
    c = lows[0].shape[0]
    eye = (lax.broadcasted_iota(jnp.int32, (c, c), 0) == lax.broadcasted_iota(jnp.int32, (c, c), 1)).astype(F32)
    ms = [-low for low in lows]
    ps = [eye + m for m in ms]
    steps = max(c.bit_length() - 2, 0)
    for _ in range(steps):
        ms = [_dot(m, m, _NN) for m in ms]
        ps = [p + _dot(p, m, _NN) for p, m in zip(ps, ms)]
    rs = [eye - p - _dot(low, p, _NN, True) for low, p in zip(lows, ps)]
    return [p + _dot(p, r, _NN, True) for p, r in zip(ps, rs)]


@jax.custom_vjp
def _tri_inv(lows):
    return _tri_inv_impl(lows)


def _tri_inv_fwd(lows):
    xs = _tri_inv_impl(lows)
    return xs, xs


def _tri_inv_bwd(xs, dxs):
    ts = [_dot(x, dx, _TN) for x, dx in zip(xs, dxs)]
    return ([-_dot(t, x, _NT) for t, x in zip(ts, xs)],)


_tri_inv.defvjp(_tri_inv_fwd, _tri_inv_bwd)


def _sigmoid(x):
    return jax.nn.sigmoid(x)


def _silu(x):
    return x * jax.nn.sigmoid(x)


def _softplus(x):
    u = jnp.exp(-jnp.abs(x))
    log1p_u = jnp.where(u < 1e-3, u * (1.0 - u * (0.5 - u * (1.0 / 3.0))), jnp.log(1.0 + u))
    return jnp.maximum(x, 0.0) + log1p_u


def _layernorm(x, g, b):
    mu = jnp.mean(x, axis=-1, keepdims=True)
    xc = x - mu
    var = jnp.mean(xc * xc, axis=-1, keepdims=True)
    return xc * lax.rsqrt(var + LN_EPS) * g + b


def _pick_lane(x, lane):
    idx = lax.broadcasted_iota(jnp.int32, x.shape, 1)
    return jnp.sum(jnp.where(idx == lane, x, 0.0), axis=1, keepdims=True)


def _gdn_chunk(q_list, k_list, v_list, ba_list, gz_list, s_list, a_row, dt_row, ng_row):
    c = ba_list[0].shape[0]
    n_chunks = len(ba_list)
    pairs = [(ci, h) for ci in range(n_chunks) for h in range(HEADS)]
    every = range(len(pairs))
    rows = lax.broadcasted_iota(jnp.int32, (c, c), 0)
    cols = lax.broadcasted_iota(jnp.int32, (c, c), 1)
    causal = rows >= cols
    strict = rows > cols
    tril = causal.astype(F32)
    triu = (rows <= cols).astype(F32)
    last_row = lax.broadcasted_iota(jnp.int32, (c, 1), 0) == c - 1
    sub8 = lax.broadcasted_iota(jnp.int32, (HEADS, c), 0)

    beta_all = [_sigmoid(ba) for ba in ba_list]
    g_all = [-jnp.exp(a_row) * _softplus(ba + dt_row) for ba in ba_list]
    gc_all = [_mm_nn_hi(tril, g) for g in g_all]
    gc_t = [_mm_tn_hi(g, triu)[HEADS:2 * HEADS, :] for g in g_all]

    q = [_silu(a) for a in q_list]
    k = [_silu(a) for a in k_list]
    v = [_silu(a) for a in v_list]
    q = [a * lax.rsqrt(jnp.sum(a * a, axis=-1, keepdims=True) + L2_EPS) * (HEAD_DIM ** -0.5) for a in q]
    k = [a * lax.rsqrt(jnp.sum(a * a, axis=-1, keepdims=True) + L2_EPS) for a in k]
    beta = [_pick_lane(beta_all[ci], h) for ci, h in pairs]
    gc = [_pick_lane(gc_all[ci], HEADS + h) for ci, h in pairs]
    gc_cols = [jnp.sum(jnp.where(sub8 == h, gc_t[ci], 0.0), axis=0, keepdims=True) for ci, h in pairs]
    decay = [jnp.where(causal, jnp.exp(jnp.where(causal, gc[p] - gc_cols[p], 0.0)), 0.0) for p in every]
    kb = [k[p] * beta[p] for p in every]
    low = [jnp.where(strict, _mm_nt(kb[p], k[p]) * decay[p], 0.0) for p in every]
    x = _tri_inv(low)
    eg = [jnp.exp(gc[p]) for p in every]
    u = [_mm_nn(x[p], v[p] * beta[p]) for p in every]
    w = [_mm_nn(x[p], kb[p] * eg[p]) for p in every]
    intra = [_mm_nt(q[p], k[p]) * decay[p] for p in every]
    q_dec = [q[p] * eg[p] for p in every]
    g_last = [jnp.sum(jnp.where(last_row, gc[p], 0.0), axis=0, keepdims=True) for p in every]
    k_dec = [k[p] * jnp.exp(g_last[p] - gc[p]) for p in every]
    s_dec = [jnp.exp(g_last[p]) for p in every]

    o = []
    state = list(s_list)
    for ci in range(n_chunks):
        at = [ci * HEADS + h for h in range(HEADS)]
        v_new = [u[p] - _mm_nn(w[p], state[h]) for h, p in enumerate(at)]
        o += [_mm_nn(q_dec[p], state[h]) + _mm_nn(intra[p], v_new[h]) for h, p in enumerate(at)]
        state = [state[h] * s_dec[p] + _mm_tn(k_dec[p], v_new[h]) for h, p in enumerate(at)]
    o = [a * lax.rsqrt(jnp.mean(a * a, axis=-1, keepdims=True) + RMS_EPS) * ng_row for a in o]
    o = [o[p] * _silu(gz_list[p]) for p in every]
    return o, state


def _conf_post(cpre, cz, g, b):
    return _silu(_layernorm(cpre, g, b)) * _silu(cz)


def _matmul_nn(a, b, name, tm=2048, tn=1024):
    m, k = a.shape
    n = b.shape[1]
    tn = min(tn, n)

    def body(a_ref, b_ref, o_ref):
        o_ref[...] = jnp.dot(a_ref[...], b_ref[...], preferred_element_type=F32)

    return pl.pallas_call(
        body, name=name, grid=(n // tn, m // tm),
        in_specs=[pl.BlockSpec((tm, k), lambda j, i: (i, 0)), pl.BlockSpec((k, tn), lambda j, i: (0, j))],
        out_specs=pl.BlockSpec((tm, tn), lambda j, i: (i, j)),
        out_shape=jax.ShapeDtypeStruct((m, n), F32),
        compiler_params=_cparams(("parallel", "parallel")),
    )(a, b)


def _matmul_tn(a, b, name, tt=2048, tn=1024):
    t, k1 = a.shape
    n = b.shape[1]
    tn = min(tn, n)
    n_t = t // tt

    def body(a_ref, b_ref, o_ref, acc_ref):
        @pl.when(pl.program_id(1) == 0)
        def _():
            acc_ref[...] = jnp.zeros_like(acc_ref)

        acc_ref[...] += lax.dot_general(a_ref[...], b_ref[...], (_TN, ((), ())), preferred_element_type=F32)

        @pl.when(pl.program_id(1) == n_t - 1)
        def _():
            o_ref[...] = acc_ref[...].astype(BF16)

    return pl.pallas_call(
        body, name=name, grid=(n // tn, n_t),
        in_specs=[pl.BlockSpec((tt, k1), lambda j, i: (i, 0)), pl.BlockSpec((tt, tn), lambda j, i: (i, j))],
        out_specs=pl.BlockSpec((k1, tn), lambda j, i: (0, j)),
        out_shape=jax.ShapeDtypeStruct((k1, n), BF16),
        scratch_shapes=[pltpu.VMEM((k1, tn), F32)],
        compiler_params=_cparams(("parallel", "arbitrary")),
    )(a, b)


def _build_bank(bank_ref, shifts):
    ext = bank_ref[0]
    rows = ext.shape[0]
    for s in shifts:
        if s:
            bank_ref[s] = pltpu.roll(ext, rows - s, axis=0)


def _conv_taps(bank_ref, w_ref, offsets, n_rows, width, emit):
    def piece(rc, carry):
        r0 = pl.multiple_of(rc * CONV_PIECE, CONV_PIECE)
        for cb in range(width // 128):
            lanes = slice(cb * 128, (cb + 1) * 128)
            acc = jnp.zeros((CONV_PIECE, 128), F32)
            for k, off in enumerate(offsets):
                m, s = divmod(off, 8)
                acc = acc + bank_ref[s, pl.ds(r0 + 8 * m, CONV_PIECE), lanes] * w_ref[k:k + 1, lanes]
            emit(r0, lanes, acc)
        return carry

    lax.fori_loop(0, n_rows // CONV_PIECE, piece, 0)


def _conv_dw(bank_ref, d_ref, offsets, n_rows, width, emit):
    ms = [divmod(off, 8) for off in offsets]
    n_taps = len(offsets)
    group = max(1, 32 // n_taps)
    blocks = [slice(cb * 128, (cb + 1) * 128) for cb in range(width // 128)]
    for g0 in range(0, len(blocks), group):
        lane_group = blocks[g0:g0 + group]

        def piece(rc, accs, lane_group=lane_group):
            r0 = pl.multiple_of(rc * 8, 8)
            out = []
            for b, lanes in enumerate(lane_group):
                d = d_ref[pl.ds(r0, 8), lanes]
                out += [accs[b * n_taps + k] + d * bank_ref[s, pl.ds(r0 + 8 * m, 8), lanes]
                        for k, (m, s) in enumerate(ms)]
            return tuple(out)

        init = tuple(jnp.zeros((8, 128), F32) for _ in range(len(lane_group) * n_taps))
        accs = lax.fori_loop(0, n_rows // 8, piece, init)
        for b, lanes in enumerate(lane_group):
            for k in range(n_taps):
                emit(k, lanes, jnp.sum(accs[b * n_taps + k], axis=0, keepdims=True))


def _conf_fwd(proj_conf, dw_w, dw_b, ln_g, ln_b, tt=256):
    t = proj_conf.shape[0]
    hb = tt // HALO_CONF
    offsets = [HALO_CONF - (K_CONF - 1) + k for k in range(K_CONF)]

    def body(cv_ref, cg_ref, cz_ref, cvh_ref, cgh_ref, w_ref, b_ref, g_ref, bb_ref, cpre_ref, aout_ref, bank_ref):
        first = pl.program_id(0) == 0
        halo = cvh_ref[...] * _sigmoid(cgh_ref[...])
        bank_ref[0, 0:HALO_CONF, :] = jnp.where(first, 0.0, halo)
        bank_ref[0, HALO_CONF:, :] = cv_ref[...] * _sigmoid(cg_ref[...])
        _build_bank(bank_ref, range(8))

        def emit(r0, lanes, acc):
            cpre_ref[pl.ds(r0, CONV_PIECE), lanes] = acc + b_ref[0:1, lanes]

        _conv_taps(bank_ref, w_ref, offsets, tt, D, emit)
        aout_ref[...] = _conf_post(cpre_ref[...], cz_ref[...], g_ref[...], bb_ref[...]).astype(BF16)

    row = pl.BlockSpec((1, D), lambda i: (0, 0))
    return pl.pallas_call(
        body, name="conf_fwd", grid=(t // tt,),
        in_specs=[pl.BlockSpec((tt, D), lambda i: (i, 0)), pl.BlockSpec((tt, D), lambda i: (i, 1)),
                  pl.BlockSpec((tt, D), lambda i: (i, 2)),
                  pl.BlockSpec((HALO_CONF, D), lambda i: (jnp.maximum(i * hb - 1, 0), 0)),
                  pl.BlockSpec((HALO_CONF, D), lambda i: (jnp.maximum(i * hb - 1, 0), 1)),
                  pl.BlockSpec((32, D), lambda i: (0, 0)), row, row, row],
        out_specs=[pl.BlockSpec((tt, D), lambda i: (i, 0)), pl.BlockSpec((tt, D), lambda i: (i, 0))],
        out_shape=[jax.ShapeDtypeStruct((t, D), F32), jax.ShapeDtypeStruct((t, D), BF16)],
        scratch_shapes=[pltpu.VMEM((8, tt + HALO_CONF, D), F32)],
        compiler_params=_cparams(("parallel",)),
    )(proj_conf, proj_conf, proj_conf, proj_conf, proj_conf, dw_w, dw_b, ln_g, ln_b)


def _conf_bwd_post(cpre, proj_conf, da_out, ln_g, ln_b, tt=256):
    t = cpre.shape[0]

    def body(c_ref, z_ref, da_ref, g_ref, b_ref, dc_ref, dz_ref, dg_ref, db_ref):
        @pl.when(pl.program_id(0) == 0)
        def _():
            dg_ref[...] = jnp.zeros_like(dg_ref)
            db_ref[...] = jnp.zeros_like(db_ref)

        _, vjp = jax.vjp(_conf_post, c_ref[...], z_ref[...], g_ref[...], b_ref[...])
        dc, dz, dg, db = vjp(da_ref[...])
        dc_ref[...] = dc
        dz_ref[...] = dz.astype(BF16)
        dg_ref[0:1, :] += dg
        db_ref[0:1, :] += db

    row = pl.BlockSpec((1, D), lambda i: (0, 0))
    acc = pl.BlockSpec((8, D), lambda i: (0, 0))
    return pl.pallas_call(
        body, name="conf_bwd_post", grid=(t // tt,),
        in_specs=[pl.BlockSpec((tt, D), lambda i: (i, 0)), pl.BlockSpec((tt, D), lambda i: (i, 2)),
                  pl.BlockSpec((tt, D), lambda i: (i, 0)), row, row],
        out_specs=[pl.BlockSpec((tt, D), lambda i: (i, 0)), pl.BlockSpec((tt, D), lambda i: (i, 0)), acc, acc],
        out_shape=[jax.ShapeDtypeStruct((t, D), F32), jax.ShapeDtypeStruct((t, D), BF16),
                   jax.ShapeDtypeStruct((8, D), F32), jax.ShapeDtypeStruct((8, D), F32)],
        compiler_params=_cparams(("arbitrary",)),
    )(cpre, proj_conf, da_out, ln_g, ln_b)


def _conf_bwd_conv(dcpre, proj_conf, dcz, dw_w, tt=256):
    t = dcpre.shape[0]
    n_tiles = t // tt
    hb = tt // HALO_CONF
    n_hb = t // HALO_CONF
    offsets = [K_CONF - 1 - k for k in range(K_CONF)]

    def body(d_ref, dn_ref, cv_ref, cg_ref, dz_ref, w_ref, dp_ref, dw_ref, db_ref, bank_d, a_scr, da_scr):
        i = pl.program_id(0)

        @pl.when(i == 0)
        def _():
            dw_ref[...] = jnp.zeros_like(dw_ref)
            db_ref[...] = jnp.zeros_like(db_ref)

        cv = cv_ref[...]
        sg = _sigmoid(cg_ref[...])
        a_scr[...] = cv * sg
        bank_d[0, 0:tt, :] = d_ref[...]
        bank_d[0, tt:, :] = jnp.where(i == n_tiles - 1, 0.0, dn_ref[...])
        _build_bank(bank_d, range(8))

        def emit_da(r0, lanes, acc):
            da_scr[pl.ds(r0, CONV_PIECE), lanes] = acc

        _conv_taps(bank_d, w_ref, offsets, tt, D, emit_da)
        da = da_scr[...]
        dp_ref[:, 0:D] = (da * sg).astype(BF16)
        dp_ref[:, D:2 * D] = (da * cv * sg * (1.0 - sg)).astype(BF16)
        dp_ref[:, 2 * D:3 * D] = dz_ref[...]

        def emit_dw(k, lanes, row):
            dw_ref[k:k + 1, lanes] += row

        _conv_dw(bank_d, a_scr, offsets, tt, D, emit_dw)
        db_ref[0:1, :] += jnp.sum(d_ref[...], axis=0, keepdims=True)

    nxt = lambda i: jnp.minimum((i + 1) * hb, n_hb - 1)
    return pl.pallas_call(
        body, name="conf_bwd_conv", grid=(n_tiles,),
        in_specs=[pl.BlockSpec((tt, D), lambda i: (i, 0)), pl.BlockSpec((HALO_CONF, D), lambda i: (nxt(i), 0)),
                  pl.BlockSpec((tt, D), lambda i: (i, 0)), pl.BlockSpec((tt, D), lambda i: (i, 1)),
                  pl.BlockSpec((tt, D), lambda i: (i, 0)), pl.BlockSpec((32, D), lambda i: (0, 0))],
        out_specs=[pl.BlockSpec((tt, 3 * D), lambda i: (i, 0)), pl.BlockSpec((32, D), lambda i: (0, 0)),
                   pl.BlockSpec((8, D), lambda i: (0, 0))],
        out_shape=[jax.ShapeDtypeStruct((t, 3 * D), BF16), jax.ShapeDtypeStruct((32, D), F32),
                   jax.ShapeDtypeStruct((8, D), F32)],
        scratch_shapes=[pltpu.VMEM((8, tt + HALO_CONF, D), F32), pltpu.VMEM((tt, D), F32), pltpu.VMEM((tt, D), F32)],
        compiler_params=_cparams(("arbitrary",)),
    )(dcpre, dcpre, proj_conf, proj_conf, dcz, dw_w)


def _gdn_conv_fwd(proj_qkv, conv_w, tt=512):
    t, width = proj_qkv.shape
    hb = tt // HALO_GDN
    offsets = [HALO_GDN - (K_GDN - 1) + k for k in range(K_GDN)]
    shifts = sorted({off % 8 for off in offsets})

    def body(x_ref, xh_ref, w_ref, o_ref, bank_ref):
        bank_ref[0, 0:HALO_GDN, :] = jnp.where(pl.program_id(1) == 0, 0.0, xh_ref[...])
        bank_ref[0, HALO_GDN:, :] = x_ref[...]
        _build_bank(bank_ref, shifts)

        def emit(r0, lanes, acc):
            o_ref[pl.ds(r0, CONV_PIECE), lanes] = acc

        _conv_taps(bank_ref, w_ref, offsets, tt, D, emit)

    return pl.pallas_call(
        body, name="gdn_conv_fwd", grid=(width // D, t // tt),
        in_specs=[pl.BlockSpec((tt, D), lambda j, i: (i, j)),
                  pl.BlockSpec((HALO_GDN, D), lambda j, i: (jnp.maximum(i * hb - 1, 0), j)),
                  pl.BlockSpec((8, D), lambda j, i: (0, j))],
        out_specs=pl.BlockSpec((tt, D), lambda j, i: (i, j)),
        out_shape=jax.ShapeDtypeStruct((t, width), F32),
        scratch_shapes=[pltpu.VMEM((8, tt + HALO_GDN, D), F32)],
        compiler_params=_cparams(("parallel", "parallel")),
    )(proj_qkv, proj_qkv, conv_w)


def _gdn_conv_bwd(dqkv_c, proj_qkv, conv_w, tt=512):
    t, width = proj_qkv.shape
    n_tiles = t // tt
    hb = tt // HALO_GDN
    n_hb = t // HALO_GDN
    offsets = [K_GDN - 1 - k for k in range(K_GDN)]

    def body(d_ref, dn_ref, x_ref, w_ref, dx_ref, dw_ref, bank_d):
        i = pl.program_id(1)

        @pl.when(i == 0)
        def _():
            dw_ref[...] = jnp.zeros_like(dw_ref)

        bank_d[0, 0:tt, :] = d_ref[...]
        bank_d[0, tt:, :] = jnp.where(i == n_tiles - 1, 0.0, dn_ref[...])
        _build_bank(bank_d, sorted({off % 8 for off in offsets}))

        def emit_dx(r0, lanes, acc):
            dx_ref[pl.ds(r0, CONV_PIECE), lanes] = acc.astype(BF16)

        _conv_taps(bank_d, w_ref, offsets, tt, D, emit_dx)

        def emit_dw(k, lanes, row):
            dw_ref[k:k + 1, lanes] += row

        _conv_dw(bank_d, x_ref, offsets, tt, D, emit_dw)

    return pl.pallas_call(
        body, name="gdn_conv_bwd", grid=(width // D, n_tiles),
        in_specs=[pl.BlockSpec((tt, D), lambda j, i: (i, j)),
                  pl.BlockSpec((HALO_GDN, D), lambda j, i: (jnp.minimum((i + 1) * hb, n_hb - 1), j)),
                  pl.BlockSpec((tt, D), lambda j, i: (i, j)),
                  pl.BlockSpec((8, D), lambda j, i: (0, j))],
        out_specs=[pl.BlockSpec((tt, D), lambda j, i: (i, j)), pl.BlockSpec((8, D), lambda j, i: (0, j))],
        out_shape=[jax.ShapeDtypeStruct((t, width), BF16), jax.ShapeDtypeStruct((8, width), F32)],
        scratch_shapes=[pltpu.VMEM((8, tt + HALO_GDN, D), F32)],
        compiler_params=_cparams(("parallel", "arbitrary")),
    )(dqkv_c, dqkv_c, proj_qkv, conv_w)


def _pair_rows(ci):
    return slice(ci * CHUNK, (ci + 1) * CHUNK)


def _pair_lanes(h, base=0):
    return slice(base + h * HEAD_DIM, base + (h + 1) * HEAD_DIM)


def _pair_slices(ref):
    return [ref[_pair_rows(ci), _pair_lanes(h)] for ci in range(CHUNKS_PER_STEP) for h in range(HEADS)]


def _gdn_chunk_fwd(qkv_c, proj_ba, proj_gz, a_row, dt_row, ng_row):
    t = qkv_c.shape[0]
    rows = CHUNKS_PER_STEP * CHUNK
    n_steps = t // rows

    def body(q_ref, k_ref, v_ref, ba_ref, gz_ref, a_ref, dt_ref, ng_ref, o_ref, ssave_ref, s_scr):
        @pl.when(pl.program_id(0) == 0)
        def _():
            s_scr[...] = jnp.zeros_like(s_scr)

        s_list = [s_scr[h] for h in range(HEADS)]
        for h in range(HEADS):
            ssave_ref[0, h] = s_list[h]
        ba_list = [ba_ref[_pair_rows(ci), :] for ci in range(CHUNKS_PER_STEP)]
        o_list, s_new = _gdn_chunk(_pair_slices(q_ref), _pair_slices(k_ref), _pair_slices(v_ref), ba_list,
                                   _pair_slices(gz_ref), s_list, a_ref[...], dt_ref[...], ng_ref[...])
        for ci in range(CHUNKS_PER_STEP):
            for h in range(HEADS):
                o_ref[_pair_rows(ci), _pair_lanes(h)] = o_list[ci * HEADS + h].astype(BF16)
        for h in range(HEADS):
            s_scr[h] = s_new[h]

    row = pl.BlockSpec((1, HEAD_DIM), lambda i: (0, 0))
    return pl.pallas_call(
        body, name="gdn_chunk_fwd", grid=(n_steps,),
        in_specs=[pl.BlockSpec((rows, D), lambda i: (i, 0)), pl.BlockSpec((rows, D), lambda i: (i, 1)),
                  pl.BlockSpec((rows, D), lambda i: (i, 2)), pl.BlockSpec((rows, HEAD_DIM), lambda i: (i, 0)),
                  pl.BlockSpec((rows, D), lambda i: (i, 0)), row, row, row],
        out_specs=[pl.BlockSpec((rows, D), lambda i: (i, 0)),
                   pl.BlockSpec((1, HEADS, HEAD_DIM, HEAD_DIM), lambda i: (i, 0, 0, 0))],
        out_shape=[jax.ShapeDtypeStruct((t, D), BF16),
                   jax.ShapeDtypeStruct((n_steps, HEADS, HEAD_DIM, HEAD_DIM), F32)],
        scratch_shapes=[pltpu.VMEM((HEADS, HEAD_DIM, HEAD_DIM), F32)],
        compiler_params=_cparams(("arbitrary",)),
    )(qkv_c, qkv_c, qkv_c, proj_ba, proj_gz, a_row, dt_row, ng_row)


def _gdn_chunk_bwd(qkv_c, proj_ba, proj_gz, s_saved, do_gated, a_row, dt_row, ng_row):
    t = qkv_c.shape[0]
    rows = CHUNKS_PER_STEP * CHUNK
    n_steps = t // rows

    def body(q_ref, k_ref, v_ref, ba_ref, gz_ref, s_ref, do_ref, a_ref, dt_ref, ng_ref,
             dqkv_ref, dba_ref, dgz_ref, da_ref, ddt_ref, dng_ref, ds_scr):
        @pl.when(pl.program_id(0) == 0)
        def _():
            ds_scr[...] = jnp.zeros_like(ds_scr)
            da_ref[...] = jnp.zeros_like(da_ref)
            ddt_ref[...] = jnp.zeros_like(ddt_ref)
            dng_ref[...] = jnp.zeros_like(dng_ref)

        s_list = [s_ref[0, h] for h in range(HEADS)]
        ba_list = [ba_ref[_pair_rows(ci), :] for ci in range(CHUNKS_PER_STEP)]
        _, vjp = jax.vjp(_gdn_chunk, _pair_slices(q_ref), _pair_slices(k_ref), _pair_slices(v_ref), ba_list,
                         _pair_slices(gz_ref), s_list, a_ref[...], dt_ref[...], ng_ref[...])
        ds_list = [ds_scr[h] for h in range(HEADS)]
        dq, dk, dv, dba, dgz, ds_in, da, ddt, dng = vjp((_pair_slices(do_ref), ds_list))
        for ci in range(CHUNKS_PER_STEP):
            for h in range(HEADS):
                p = ci * HEADS + h
                dqkv_ref[_pair_rows(ci), _pair_lanes(h)] = dq[p]
                dqkv_ref[_pair_rows(ci), _pair_lanes(h, D)] = dk[p]
                dqkv_ref[_pair_rows(ci), _pair_lanes(h, 2 * D)] = dv[p]
                dgz_ref[_pair_rows(ci), _pair_lanes(h)] = dgz[p].astype(BF16)
            dba_ref[_pair_rows(ci), :] = dba[ci].astype(BF16)
        for h in range(HEADS):
            ds_scr[h] = ds_in[h]
        da_ref[0:1, :] += da
        ddt_ref[0:1, :] += ddt
        dng_ref[0:1, :] += dng

    rev = lambda i: n_steps - 1 - i
    row = pl.BlockSpec((1, HEAD_DIM), lambda i: (0, 0))
    acc = pl.BlockSpec((8, HEAD_DIM), lambda i: (0, 0))
    outs = pl.pallas_call(
        body, name="gdn_chunk_bwd", grid=(n_steps,),
        in_specs=[pl.BlockSpec((rows, D), lambda i: (rev(i), 0)), pl.BlockSpec((rows, D), lambda i: (rev(i), 1)),
                  pl.BlockSpec((rows, D), lambda i: (rev(i), 2)),
                  pl.BlockSpec((rows, HEAD_DIM), lambda i: (rev(i), 0)),
                  pl.BlockSpec((rows, D), lambda i: (rev(i), 0)),
                  pl.BlockSpec((1, HEADS, HEAD_DIM, HEAD_DIM), lambda i: (rev(i), 0, 0, 0)),
                  pl.BlockSpec((rows, D), lambda i: (rev(i), 0)), row, row, row],
        out_specs=[pl.BlockSpec((rows, 3 * D), lambda i: (rev(i), 0)),
                   pl.BlockSpec((rows, HEAD_DIM), lambda i: (rev(i), 0)),
                   pl.BlockSpec((rows, D), lambda i: (rev(i), 0)), acc, acc, acc],
        out_shape=[jax.ShapeDtypeStruct((t, 3 * D), F32)]
        + [jax.ShapeDtypeStruct((t, HEAD_DIM), BF16), jax.ShapeDtypeStruct((t, D), BF16)]
        + [jax.ShapeDtypeStruct((8, HEAD_DIM), F32)] * 3,
        scratch_shapes=[pltpu.VMEM((HEADS, HEAD_DIM, HEAD_DIM), F32)],
        compiler_params=_cparams(("arbitrary",)),
    )(qkv_c, qkv_c, qkv_c, proj_ba, proj_gz, s_saved, do_gated, a_row, dt_row, ng_row)
    return outs


def _merge(a_out, o_gated, proj_gate, x, target, w_conf, w_gdn, w_o, pg, pb, tt=256):
    t = x.shape[0]

    def body(a_ref, o_ref, gt_ref, x_ref, y_ref, wc_ref, wg_ref, wo_ref, pg_ref, pb_ref,
             loss_ref, dpg_ref, dpb_ref, dx_ref, dgt_ref, da_ref, do_ref, h_ref, ds_ref, dyc_ref, dyg_ref):
        @pl.when(pl.program_id(0) == 0)
        def _():
            loss_ref[...] = jnp.zeros_like(loss_ref)
            dpg_ref[...] = jnp.zeros_like(dpg_ref)
            dpb_ref[...] = jnp.zeros_like(dpb_ref)

        wc, wg, wo = wc_ref[...], wg_ref[...], wo_ref[...]
        y_conf = _dot(a_ref[...], wc, _NN)
        y_gdn = _dot(o_ref[...], wg, _NN)
        sc = _sigmoid(gt_ref[:, 0:D])
        sg = _sigmoid(gt_ref[:, D:2 * D])
        h = sc * y_conf + sg * y_gdn
        z = DN_ALPHA * x_ref[...] + _dot(h, wo, _NN)
        mu = jnp.mean(z, axis=-1, keepdims=True)
        zc = z - mu
        rstd = lax.rsqrt(jnp.mean(zc * zc, axis=-1, keepdims=True) + LN_EPS)
        xhat = zc * rstd
        gain = pg_ref[...]
        err = xhat * gain + pb_ref[...] - y_ref[...]
        tok = jnp.mean(err * err, axis=-1, keepdims=True)
        loss_ref[...] += 0.5 * jnp.sum(tok, axis=0, keepdims=True)

        dy = err * (1.0 / D)
        dpg_ref[0:1, :] += jnp.sum(dy * xhat, axis=0, keepdims=True)
        dpb_ref[0:1, :] += jnp.sum(dy, axis=0, keepdims=True)
        dxh = dy * gain
        dz = rstd * (dxh - jnp.mean(dxh, axis=-1, keepdims=True)
                     - xhat * jnp.mean(dxh * xhat, axis=-1, keepdims=True))
        dx_ref[...] = DN_ALPHA * dz
        dh = _dot(dz, wo, _NT)
        dyc = dh * sc
        dyg = dh * sg
        dgt_ref[:, 0:D] = (dh * y_conf * sc * (1.0 - sc)).astype(BF16)
        dgt_ref[:, D:2 * D] = (dh * y_gdn * sg * (1.0 - sg)).astype(BF16)
        da_ref[...] = _dot(dyc, wc, _NT)
        do_ref[...] = _dot(dyg, wg, _NT)
        h_ref[...] = h.astype(BF16)
        ds_ref[...] = dz.astype(BF16)
        dyc_ref[...] = dyc.astype(BF16)
        dyg_ref[...] = dyg.astype(BF16)

    tile = pl.BlockSpec((tt, D), lambda i: (i, 0))
    wide = pl.BlockSpec((tt, 2 * D), lambda i: (i, 0))
    mat = pl.BlockSpec((D, D), lambda i: (0, 0))
    row = pl.BlockSpec((1, D), lambda i: (0, 0))
    acc = pl.BlockSpec((8, D), lambda i: (0, 0))
    act = lambda dt: jax.ShapeDtypeStruct((t, D), dt)
    return pl.pallas_call(
        body, name="merge", grid=(t // tt,),
        in_specs=[tile, tile, wide, tile, tile, mat, mat, mat, row, row],
        out_specs=[pl.BlockSpec((8, 128), lambda i: (0, 0)), acc, acc, tile, wide, tile, tile, tile, tile, tile, tile],
        out_shape=[jax.ShapeDtypeStruct((8, 128), F32), jax.ShapeDtypeStruct((8, D), F32),
                   jax.ShapeDtypeStruct((8, D), F32), act(F32), jax.ShapeDtypeStruct((t, 2 * D), BF16),
                   act(F32), act(F32), act(BF16), act(BF16), act(BF16), act(BF16)],
        compiler_params=_cparams(("arbitrary",)),
    )(a_out, o_gated, proj_gate, x, target, w_conf, w_gdn, w_o, pg, pb)


def _mesh_place():
    x, y, c = lax.axis_index("x"), lax.axis_index("y"), lax.axis_index("c")
    return x, y, c


def _flat(px, py, pc):
    return 4 * px + 2 * py + pc


def _all_gather(shards):
    n = len(shards)

    def body(*refs):
        ins, outs = refs[:n], refs[n:2 * n]
        send_sems, recv_sems, local_sems = refs[2 * n:]
        x, y, c = _mesh_place()
        me, sibling = (x, y, c), (x, y, 1 - c)
        chips = [(1 - x, y), (x, 1 - y), (1 - x, 1 - y)]

        def copy(a, k, block, to, src=None):
            dst = outs[a].at[_flat(*block)]
            return pltpu.make_async_remote_copy(
                src_ref=dst if src is None else src, dst_ref=dst,
                send_sem=send_sems.at[a, k], recv_sem=recv_sems.at[a, k],
                device_id=to, device_id_type=pl.DeviceIdType.MESH)

        mine = [pltpu.make_async_copy(ins[a], outs[a].at[_flat(*me)], local_sems.at[a]) for a in range(n)]
        for cp in mine:
            cp.start()
        first = []
        for a in range(n):
            first.append(copy(a, 0, me, sibling, src=ins[a]))
            first += [copy(a, 1 + j, me, (*chip, c), src=ins[a]) for j, chip in enumerate(chips)]
        for cp in first:
            cp.start()
        passed = []
        for j, chip in enumerate(chips):
            for a in range(n):
                copy(a, 1 + j, (*chip, c), me).wait_recv()
                fwd = copy(a, 4 + j, (*chip, c), sibling)
                fwd.start()
                passed.append(fwd)
        for a in range(n):
            copy(a, 0, sibling, me).wait_recv()
            for j, chip in enumerate(chips):
                copy(a, 4 + j, (*chip, 1 - c), me).wait_recv()
        for cp in first + passed:
            cp.wait_send()
        for cp in mine:
            cp.wait()

    any_spec = pl.BlockSpec(memory_space=pl.ANY)
    return pl.pallas_call(
        body, name="all_gather_weights",
        in_specs=[any_spec] * n, out_specs=[any_spec] * n,
        out_shape=[jax.ShapeDtypeStruct((N_DEV,) + s.shape, s.dtype) for s in shards],
        scratch_shapes=[pltpu.SemaphoreType.DMA((n, 7)), pltpu.SemaphoreType.DMA((n, 7)),
                        pltpu.SemaphoreType.DMA((n,))],
    )(*shards)


def _exchange_and_grad_x(block_arrays, small, init, segments, tm=512, tk=1024):
    nb = len(block_arrays)
    ns = len(segments)
    m, k1 = init.shape
    widths = [min(tk, a.shape[1]) for a, _ in segments]
    counts = [a.shape[1] // wd for (a, _), wd in zip(segments, widths)]
    starts = [sum(counts[:s]) for s in range(ns)]
    n_j = sum(counts)
    n_i = m // tm

    def body(*refs):
        g_refs, s_ref, i_ref = refs[:nb], refs[nb], refs[nb + 1]
        seg_refs = refs[nb + 2:nb + 2 + 2 * ns]
        outs = refs[nb + 2 + 2 * ns:]
        land_refs, sall_ref, o_ref = outs[:nb], outs[nb], outs[nb + 1]
        send_sems, recv_sems, local_sems = outs[nb + 2:]
        i, j = pl.program_id(0), pl.program_id(1)

        def copies(with_arrivals):
            x, y, c = _mesh_place()
            me = _flat(x, y, c)
            mine = [pltpu.make_async_copy(g_refs[a].at[me], land_refs[a].at[me], local_sems.at[a]) for a in range(nb)]
            mine.append(pltpu.make_async_copy(s_ref, sall_ref.at[me], local_sems.at[nb]))
            sends, recvs = [], []
            for k in range(7):
                mask = k + 1
                px = 1 - x if mask & 4 else x
                py = 1 - y if mask & 2 else y
                pc = 1 - c if mask & 1 else c
                peer = _flat(px, py, pc)
                for a in range(nb + 1):
                    kw = dict(send_sem=send_sems.at[a, k], recv_sem=recv_sems.at[a, k],
                              device_id=(px, py, pc), device_id_type=pl.DeviceIdType.MESH)
                    src = g_refs[a].at[peer] if a < nb else s_ref
                    land = land_refs[a] if a < nb else sall_ref
                    sends.append(pltpu.make_async_remote_copy(src_ref=src, dst_ref=land.at[me], **kw))
                    if with_arrivals:
                        recvs.append(pltpu.make_async_remote_copy(src_ref=src, dst_ref=land.at[peer], **kw))
            return mine, sends, recvs

        @pl.when((i == 0) & (j == 0))
        def _():
            mine, sends, _ = copies(False)
            for cp in mine + sends:
                cp.start()

        @pl.when(j == 0)
        def _():
            o_ref[...] = i_ref[...]

        for s in range(ns):
            @pl.when((j >= starts[s]) & (j < starts[s] + counts[s]))
            def _(s=s):
                o_ref[...] += lax.dot_general(seg_refs[2 * s][...], seg_refs[2 * s + 1][...], (_NT, ((), ())),
                                              preferred_element_type=F32)

        @pl.when((i == n_i - 1) & (j == n_j - 1))
        def _():
            mine, sends, recvs = copies(True)
            for cp in recvs:
                cp.wait_recv()
            for cp in sends:
                cp.wait_send()
            for cp in mine:
                cp.wait()

    any_spec = pl.BlockSpec(memory_space=pl.ANY)
    seg_specs = []
    for s in range(ns):
        col = lambda i, j, s=s: jnp.clip(j - starts[s], 0, counts[s] - 1)
        seg_specs.append(pl.BlockSpec((tm, widths[s]), lambda i, j, col=col: (i, col(i, j))))
        seg_specs.append(pl.BlockSpec((k1, widths[s]), lambda i, j, col=col: (0, col(i, j))))
    tile = pl.BlockSpec((tm, k1), lambda i, j: (i, 0))
    outs = pl.pallas_call(
        body, name="exchange_grads_and_grad_x", grid=(n_i, n_j),
        in_specs=[any_spec] * (nb + 1) + [tile] + seg_specs,
        out_specs=[any_spec] * (nb + 1) + [tile],
        out_shape=[jax.ShapeDtypeStruct(b.shape, b.dtype) for b in block_arrays]
        + [jax.ShapeDtypeStruct((N_DEV,) + small.shape, small.dtype), jax.ShapeDtypeStruct((m, k1), F32)],
        scratch_shapes=[pltpu.SemaphoreType.DMA((nb + 1, 7)), pltpu.SemaphoreType.DMA((nb + 1, 7)),
                        pltpu.SemaphoreType.DMA((nb + 1,))],
        compiler_params=_cparams(("arbitrary", "arbitrary")),
    )(*block_arrays, small, init, *[r for seg in segments for r in seg])
    return outs[:nb], outs[nb], outs[nb + 1]


def _adamw(parts, w, m, v, name, tile):
    rows, cols = w.shape

    def body(p_ref, w_ref, m_ref, v_ref, g_ref, d_ref, nm_ref, nv_ref):
        g = p_ref[0].astype(F32)
        for s in range(1, N_DEV):
            g = g + p_ref[s].astype(F32)
        nm = ADAM_B1 * m_ref[...] + (1.0 - ADAM_B1) * g
        nv = ADAM_B2 * v_ref[...] + (1.0 - ADAM_B2) * jnp.square(g)
        m_hat = nm / (1.0 - ADAM_B1 ** ADAM_STEP)
        v_hat = nv / (1.0 - ADAM_B2 ** ADAM_STEP)
        g_ref[...] = g
        d_ref[...] = -ADAM_LR * (m_hat / (jnp.sqrt(v_hat) + ADAM_EPS) + ADAM_WD * w_ref[...])
        nm_ref[...] = nm
        nv_ref[...] = nv

    blk = pl.BlockSpec((tile, cols), lambda i: (i, 0))
    out = jax.ShapeDtypeStruct((rows, cols), F32)
    return pl.pallas_call(
        body, name=name, grid=(rows // tile,),
        in_specs=[pl.BlockSpec((N_DEV, tile, cols), lambda i: (0, i, 0)), blk, blk, blk],
        out_specs=[blk, blk, blk, blk], out_shape=[out, out, out, out],
        compiler_params=_cparams(("parallel",)),
    )(parts, w, m, v)


def _rows_of(flat, n_rows):
    flat = flat.reshape(-1)
    return jnp.pad(flat, (0, n_rows * D - flat.shape[0])).reshape(n_rows, D)


def _pack_shards(conf_w_out, gdn_w_out, w_o, conf_dw_w, gdn_conv_w):
    return jnp.concatenate([conf_w_out, gdn_w_out, w_o, _rows_of(conf_dw_w, 16), _rows_of(gdn_conv_w, 16)], axis=0)


def _unpack_shards(p):
    dw = p[ROW_DW:ROW_DW + 4].reshape(-1)[:K_CONF * 128].reshape(K_CONF, 128)
    gc = p[ROW_GC:ROW_GC + 2].reshape(-1)[:K_GDN * 384].reshape(K_GDN, 384)
    return p[ROW_CWO:ROW_CWO + 128], p[ROW_GWO:ROW_GWO + 128], p[ROW_WO:ROW_WO + 128], dw, gc


def _pack_small(dw_b, ln_g, ln_b, pg, pb, ng, a_log, dt_bias, loss=None):
    s = jnp.zeros((SMALL_ROWS, D), F32)
    for r, val in enumerate((dw_b, ln_g, ln_b, pg, pb, ng, a_log, dt_bias)):
        s = s.at[r, :val.shape[0]].set(val)
    if loss is not None:
        s = s.at[8, 0].set(loss)
    return s


def _unpack_small(s):
    return (s[0], s[1], s[2], s[3], s[4], s[5, :HEAD_DIM], s[6, :HEADS], s[7, :HEADS])


def _scatter_blocks(g_cwo, g_gwo, g_wo, g_dw, g_gc):
    dw = g_dw.reshape(K_CONF, N_DEV, 128).transpose(1, 0, 2).reshape(N_DEV, K_CONF * 128)
    dw = jnp.pad(dw, ((0, 0), (0, 16 * D - K_CONF * 128))).reshape(N_DEV, 16, D)
    gc = g_gc.reshape(K_GDN, N_DEV, 384).transpose(1, 0, 2).reshape(N_DEV, K_GDN * 384)
    gc = jnp.pad(gc, ((0, 0), (0, 16 * D - K_GDN * 384))).reshape(N_DEV, 16, D)
    return jnp.concatenate([g_cwo.reshape(N_DEV, 128, D), g_gwo.reshape(N_DEV, 128, D),
                            g_wo.reshape(N_DEV, 128, D), dw, gc], axis=1)


def kernel(x, w_in, conf_dw_w, conf_dw_b, conf_ln_g, conf_ln_b, conf_w_out, gdn_conv_w, gdn_A_log, gdn_dt_bias, gdn_norm_g, gdn_w_out, w_o, post_ln_g, post_ln_b, loss_target, m_w_in, m_conf_dw_w, m_conf_dw_b, m_conf_ln_g, m_conf_ln_b, m_conf_w_out, m_gdn_conv_w, m_gdn_A_log, m_gdn_dt_bias, m_gdn_norm_g, m_gdn_w_out, m_w_o, m_post_ln_g, m_post_ln_b, v_w_in, v_conf_dw_w, v_conf_dw_b, v_conf_ln_g, v_conf_ln_b, v_conf_w_out, v_gdn_conv_w, v_gdn_A_log, v_gdn_dt_bias, v_gdn_norm_g, v_gdn_w_out, v_w_o, v_post_ln_g, v_post_ln_b):
    t = x.shape[1]
    x2 = x.reshape(t, D)
    target = loss_target.reshape(t, D)
    x_bf = x2.astype(BF16)

    w_pack = _pack_shards(conf_w_out, gdn_w_out, w_o, conf_dw_w, gdn_conv_w)
    convw = jnp.concatenate([_rows_of(conf_dw_w, 8), _rows_of(gdn_conv_w, 8)], axis=0)
    all_w_in, all_w, all_convw = _all_gather([w_in.astype(BF16), w_pack.astype(BF16), convw])
    w_full = all_w_in.transpose(1, 0, 2).reshape(D, W_IN_COLS)
    w_conf = w_full[:, 0:3 * D]
    w_qkv = w_full[:, 3 * D:6 * D]
    w_gz = w_full[:, 6 * D:7 * D]
    w_ba = jnp.pad(w_full[:, 7 * D:7 * D + 2 * HEADS], ((0, 0), (0, HEAD_DIM - 2 * HEADS)))
    w_gate = w_full[:, 7 * D + 2 * HEADS:]
    cwo_full = all_w[:, ROW_CWO:ROW_CWO + 128].reshape(D, D)
    gwo_full = all_w[:, ROW_GWO:ROW_GWO + 128].reshape(D, D)
    wo_full = all_w[:, ROW_WO:ROW_WO + 128].reshape(D, D)
    dw_full = all_convw[:, 0:4].reshape(N_DEV, 4 * D)[:, :K_CONF * 128].reshape(N_DEV, K_CONF, 128)
    dw_full = jnp.pad(dw_full.transpose(1, 0, 2).reshape(K_CONF, D), ((0, 32 - K_CONF), (0, 0)))
    gc_full = all_convw[:, 8:10].reshape(N_DEV, 2 * D)[:, :K_GDN * 384].reshape(N_DEV, K_GDN, 384)
    gc_full = jnp.pad(gc_full.transpose(1, 0, 2).reshape(K_GDN, 3 * D), ((0, 8 - K_GDN), (0, 0)))

    row = lambda vec: vec.reshape(1, -1)
    lane_row = lambda vec, at: jnp.zeros((1, HEAD_DIM), F32).at[0, at:at + vec.shape[0]].set(vec)
    a_row = lane_row(gdn_A_log, HEADS)
    dt_row = lane_row(gdn_dt_bias, HEADS)
    ng_row = row(gdn_norm_g)

    proj_conf = _matmul_nn(x_bf, w_conf, "proj_conf")
    proj_qkv = _matmul_nn(x_bf, w_qkv, "proj_qkv")
    proj_gz = _matmul_nn(x_bf, w_gz, "proj_gz")
    proj_gate = _matmul_nn(x_bf, w_gate, "proj_gate")
    proj_ba = _matmul_nn(x_bf, w_ba, "proj_ba")
    cpre, a_out = _conf_fwd(proj_conf, dw_full, row(conf_dw_b), row(conf_ln_g), row(conf_ln_b))
    qkv_c = _gdn_conv_fwd(proj_qkv, gc_full)
    o_gated, s_saved = _gdn_chunk_fwd(qkv_c, proj_ba, proj_gz, a_row, dt_row, ng_row)

    (loss_acc, d_pg, d_pb, dx, d_gate, da_out, do_gated, h_bf, dsub_bf, dyc_bf, dyg_bf) = _merge(
        a_out, o_gated, proj_gate, x2, target, cwo_full, gwo_full, wo_full, row(post_ln_g), row(post_ln_b))
    g_wo = _matmul_tn(h_bf, dsub_bf, "grad_w_o")
    g_cwo = _matmul_tn(a_out, dyc_bf, "grad_conf_w_out")
    g_gwo = _matmul_tn(o_gated, dyg_bf, "grad_gdn_w_out")

    dqkv_c, d_ba, d_gz, d_a_row, d_dt_row, d_ng_row = _gdn_chunk_bwd(
        qkv_c, proj_ba, proj_gz, s_saved, do_gated, a_row, dt_row, ng_row)
    d_qkv, g_gc = _gdn_conv_bwd(dqkv_c, proj_qkv, gc_full)

    dcpre, dcz, d_ln_g, d_ln_b = _conf_bwd_post(cpre, proj_conf, da_out, row(conf_ln_g), row(conf_ln_b))
    d_conf, g_dw, g_dwb = _conf_bwd_conv(dcpre, proj_conf, dcz, dw_full)

    segments = [(d_conf, w_conf, "conf"), (d_qkv, w_qkv, "qkv"), (d_gz, w_gz, "gz"),
                (d_ba, w_ba, "ba"), (d_gate, w_gate, "gate")]
    g_cols = {tag: _matmul_tn(x_bf, d_seg, "grad_w_in_" + tag) for d_seg, _, tag in segments}
    g_w_in = jnp.concatenate([g_cols["conf"], g_cols["qkv"], g_cols["gz"], g_cols["ba"][:, :2 * HEADS],
                              g_cols["gate"]], axis=1)

    w_in_blocks = g_w_in.reshape(D, N_DEV, W_IN_SHARD).transpose(1, 0, 2).astype(BF16)
    blocks = _scatter_blocks(g_cwo, g_gwo, g_wo, g_dw[:K_CONF], g_gc[:K_GDN]).astype(BF16)
    small = _pack_small(g_dwb[0], d_ln_g[0], d_ln_b[0], d_pg[0], d_pb[0], d_ng_row[0],
                        d_a_row[0, HEADS:2 * HEADS], d_dt_row[0, HEADS:2 * HEADS], loss_acc[0, 0])
    (landed_w_in, landed), small_all, dx = _exchange_and_grad_x(
        [w_in_blocks, blocks], small, dx, [(d_seg, w_seg) for d_seg, w_seg, _ in segments])

    m_pack = _pack_shards(m_conf_w_out, m_gdn_w_out, m_w_o, m_conf_dw_w, m_gdn_conv_w)
    v_pack = _pack_shards(v_conf_w_out, v_gdn_w_out, v_w_o, v_conf_dw_w, v_gdn_conv_w)
    big_w_in = _adamw(landed_w_in, w_in, m_w_in, v_w_in, "adamw_w_in", W_IN_TILE)
    big = _adamw(landed, w_pack, m_pack, v_pack, "adamw_shards", PACK_TILE)
    ws = _pack_small(conf_dw_b, conf_ln_g, conf_ln_b, post_ln_g, post_ln_b, gdn_norm_g, gdn_A_log, gdn_dt_bias)
    ms = _pack_small(m_conf_dw_b, m_conf_ln_g, m_conf_ln_b, m_post_ln_g, m_post_ln_b, m_gdn_norm_g, m_gdn_A_log,
                     m_gdn_dt_bias)
    vs = _pack_small(v_conf_dw_b, v_conf_ln_g, v_conf_ln_b, v_post_ln_g, v_post_ln_b, v_gdn_norm_g, v_gdn_A_log,
                     v_gdn_dt_bias)
    sml = _adamw(small_all, ws, ms, vs, "adamw_replicated", SMALL_ROWS)

    loss = sml[0][8, 0]
    outs = []
    for b_w_in, big_k, sml_k in zip(big_w_in, big, sml):
        b_cwo, b_gwo, b_wo, b_dw, b_gc = _unpack_shards(big_k)
        s_dwb, s_lng, s_lnb, s_pg, s_pb, s_ng, s_a, s_dt = _unpack_small(sml_k)
        outs.append([b_w_in, b_dw, s_dwb, s_lng, s_lnb, b_cwo, b_gc, s_a, s_dt, s_ng, b_gwo, b_wo, s_pg, s_pb])
    return (loss, dx.reshape(1, t, D), *outs[0], *outs[1], *outs[2], *outs[3])
```
